```python
import math
import jax
import jax.numpy as jnp
from jax import lax
import numpy as np

D_MODEL = 1024
BATCH = 4
SEQ = 8192
DEPTH = 2

GRID_W = 64
Q_BLOCK = 128
HEAD_DIM = 64
N_BRANCH = 4
BRANCH_W = 256

MLA_HEADS = 4
MLA_NOPE = 64
MLA_ROPE = 32
MLA_V = 64
MLA_Q_LORA = 192
MLA_KV_LORA = 128

DIL_GROUPS = ((128, 1), (512, 4), (2048, 16))
DIL_N_GROUPS = 3
DIL_HEADS = 4

GQA_Q_HEADS = 4
GQA_KV_HEADS = 2

DIFF_HEADS = 4
DIFF_D = 32
DIFF_V = 64

D_FF = 3584
N_EXPERTS = 8
TOP_K = 2
D_EXPERT = 3584

ROPE_THETA = 10000.0
LN_EPS = 1e-5
RMS_EPS = 1e-6
DIFF_NORM_EPS = 1e-5
DEEPNORM_ALPHA = (2 * DEPTH) ** 0.25
DEEPNORM_BETA = (8 * DEPTH) ** -0.25

IN_SPLITS = (
    MLA_Q_LORA,
    MLA_KV_LORA,
    MLA_ROPE,
    3 * DIL_N_GROUPS * DIL_HEADS * HEAD_DIM,
    GQA_Q_HEADS * HEAD_DIM,
    2 * GQA_KV_HEADS * HEAD_DIM,
    2 * DIFF_HEADS * 2 * DIFF_D,
    DIFF_HEADS * DIFF_V,
    N_BRANCH * D_MODEL,
)
N_IN = sum(IN_SPLITS)

kernel_name = 'hybrid_gated_mixer_encoder'


def _layernorm(x, g, b):
    xf = x.astype(jnp.float32)
    mu = jnp.mean(xf, axis=-1, keepdims=True)
    var = jnp.mean(jnp.square(xf - mu), axis=-1, keepdims=True)
    return ((xf - mu) * lax.rsqrt(var + LN_EPS) * g + b).astype(x.dtype)


def _rmsnorm(x, g, eps=RMS_EPS):
    xf = x.astype(jnp.float32)
    return (xf * lax.rsqrt(jnp.mean(xf * xf, axis=-1, keepdims=True) + eps) * g).astype(x.dtype)


def _rope_angles(pos, dim):
    inv = ROPE_THETA ** (-(jnp.arange(0, dim, 2, dtype=jnp.float32) / dim))
    return pos[:, None] * inv[None, :]


def _apply_rope(x, ang):
    cos = jnp.cos(ang)[None, :, None, :]
    sin = jnp.sin(ang)[None, :, None, :]
    x1, x2 = jnp.split(x.astype(jnp.float32), 2, axis=-1)
    return jnp.concatenate([x1 * cos - x2 * sin, x2 * cos + x1 * sin], axis=-1).astype(x.dtype)


def _axial_rope(x, ang_row, ang_col):
    half = x.shape[-1] // 2
    return jnp.concatenate([_apply_rope(x[..., :half], ang_row), _apply_rope(x[..., half:], ang_col)], axis=-1)


def _alibi_slopes(n):
    return 2.0 ** (-8.0 * jnp.arange(1, n + 1, dtype=jnp.float32) / n)


def _sweep_query_blocks(block_fn, seq):
    starts = jnp.arange(seq // Q_BLOCK, dtype=jnp.int32) * Q_BLOCK
    out = lax.map(block_fn, starts)
    out = jnp.moveaxis(out, 0, 1)
    return out.reshape((out.shape[0], seq) + out.shape[3:])


def _mla(c_q, c_kv, k_r, q_norm, kv_norm, w_uq, w_ukv, ang):
    B, S, _ = c_q.shape
    q = (_rmsnorm(c_q, q_norm) @ w_uq).reshape(B, S, MLA_HEADS, MLA_NOPE + MLA_ROPE)
    q_nope = q[..., :MLA_NOPE]
    q_rope = _apply_rope(q[..., MLA_NOPE:], ang)
    kv = (_rmsnorm(c_kv, kv_norm) @ w_ukv).reshape(B, S, MLA_HEADS, MLA_NOPE + MLA_V)
    k_nope = kv[..., :MLA_NOPE]
    v = kv[..., MLA_NOPE:]
    k_rope = _apply_rope(k_r[:, :, None, :], ang)[:, :, 0]
    scale = (MLA_NOPE + MLA_ROPE) ** -0.5

    def block(start):
        qn = lax.dynamic_slice_in_dim(q_nope, start, Q_BLOCK, axis=1)
        qr = lax.dynamic_slice_in_dim(q_rope, start, Q_BLOCK, axis=1)
        s = jnp.einsum('bqhd,bkhd->bhqk', qn, k_nope) + jnp.einsum('bqhd,bkd->bhqk', qr, k_rope)
        p = jax.nn.softmax(s.astype(jnp.float32) * scale, axis=-1)
        return jnp.einsum('bhqk,bkhd->bqhd', p.astype(v.dtype), v)

    return _sweep_query_blocks(block, S).reshape(B, S, MLA_HEADS * MLA_V)


def _dilated_attention(q, k, v):
    B, S = q.shape[:2]
    scale = HEAD_DIM ** -0.5
    slopes = _alibi_slopes(DIL_N_GROUPS * DIL_HEADS).reshape(DIL_N_GROUPS, DIL_HEADS)
    k_groups = [k[:, :, g] for g in range(DIL_N_GROUPS)]
    v_groups = [v[:, :, g] for g in range(DIL_N_GROUPS)]

    def block(start):
        t = start + jnp.arange(Q_BLOCK, dtype=jnp.int32)
        qb = lax.dynamic_slice_in_dim(q, start, Q_BLOCK, axis=1)
        outs, lses = [], []
        for g, (window, dil) in enumerate(DIL_GROUPS):
            off = np.arange(-(window // 2), window // 2 + 1, dil, dtype=np.int32)
            idx = t[:, None] + off[None, :]
            valid = (idx >= 0) & (idx < S)
            idx_c = jnp.clip(idx, 0, S - 1)
            kg = jnp.take(k_groups[g], idx_c, axis=1)
            vg = jnp.take(v_groups[g], idx_c, axis=1)
            s = jnp.einsum('bqhd,bqjhd->bhqj', qb[:, :, g], kg).astype(jnp.float32) * scale
            s = s - slopes[g][None, :, None, None] * jnp.abs(off).astype(jnp.float32)[None, None, None, :]
            s = jnp.where(valid[None, None], s, -jnp.inf)
            lse = jax.nn.logsumexp(s, axis=-1)
            p = jnp.exp(s - lse[..., None])
            outs.append(jnp.einsum('bhqj,bqjhd->bqhd', p.astype(vg.dtype), vg).astype(jnp.float32))
            lses.append(lse)
        wts = jax.nn.softmax(jnp.stack(lses, axis=0), axis=0)
        wts = jnp.transpose(wts, (0, 1, 3, 2))[..., None]
        return jnp.sum(wts * jnp.stack(outs, axis=0), axis=0).astype(q.dtype)

    return _sweep_query_blocks(block, S).reshape(B, S, DIL_HEADS * HEAD_DIM)


def _axial_gqa(q, k, v, q_norm, k_norm, ang_row, ang_col):
    B, S = q.shape[:2]
    group = GQA_Q_HEADS // GQA_KV_HEADS
    q = _axial_rope(_rmsnorm(q, q_norm), ang_row, ang_col).reshape(B, S, GQA_KV_HEADS, group, HEAD_DIM)
    k = _axial_rope(_rmsnorm(k, k_norm), ang_row, ang_col)
    scale = HEAD_DIM ** -0.5

    def block(start):
        qb = lax.dynamic_slice_in_dim(q, start, Q_BLOCK, axis=1)
        s = jnp.einsum('bqkgd,bskd->bkgqs', qb, k).astype(jnp.float32) * scale
        p = jax.nn.softmax(s, axis=-1)
        return jnp.einsum('bkgqs,bskd->bqkgd', p.astype(v.dtype), v)

    return _sweep_query_blocks(block, S).reshape(B, S, GQA_Q_HEADS * HEAD_DIM)


def _diff_attention(q, k, v, lam_params, subln, lam_init):
    B, S = q.shape[:2]
    lp = lam_params.astype(jnp.float32)
    lam = jnp.exp(jnp.sum(lp[0] * lp[1])) - jnp.exp(jnp.sum(lp[2] * lp[3])) + lam_init
    slopes = _alibi_slopes(DIFF_HEADS)
    scale = DIFF_D ** -0.5
    keys = jnp.arange(S, dtype=jnp.int32)

    def block(start):
        t = start + jnp.arange(Q_BLOCK, dtype=jnp.int32)
        qb = lax.dynamic_slice_in_dim(q, start, Q_BLOCK, axis=1)
        s = jnp.einsum('bqhmd,bkhmd->bhmqk', qb, k).astype(jnp.float32) * scale
        dist = jnp.abs(t[:, None] - keys[None, :]).astype(jnp.float32)
        s = s - slopes[None, :, None, None, None] * dist[None, None, None]
        p = jax.nn.softmax(s, axis=-1)
        a = p[:, :, 0] - lam * p[:, :, 1]
        return jnp.einsum('bhqk,bkhd->bqhd', a.astype(v.dtype), v)

    o = _sweep_query_blocks(block, S)
    o = _rmsnorm(o, subln, DIFF_NORM_EPS) * (1.0 - lam_init)
    return o.reshape(B, S, DIFF_HEADS * DIFF_V)


def _token_mixers(h, w_in, b_gate, mla_q_norm, mla_kv_norm, mla_w_uq, mla_w_ukv, gqa_q_norm, gqa_k_norm,
                  diff_lambda, diff_subln, w_branch, w_out, ang_1d, ang_row, ang_col, lam_init):
    B, S, _ = h.shape
    proj = h @ w_in
    c_q, c_kv, k_r, dil_qkv, g_q, g_kv, d_qk, d_v, gate_lin = jnp.split(
        proj, np.cumsum(IN_SPLITS)[:-1].tolist(), axis=-1)
    y_a = _mla(c_q, c_kv, k_r, mla_q_norm, mla_kv_norm, mla_w_uq, mla_w_ukv, ang_1d)
    dil = dil_qkv.reshape(B, S, 3, DIL_N_GROUPS, DIL_HEADS, HEAD_DIM)
    y_b = _dilated_attention(dil[:, :, 0], dil[:, :, 1], dil[:, :, 2])
    g_kv = g_kv.reshape(B, S, 2, GQA_KV_HEADS, HEAD_DIM)
    y_c = _axial_gqa(g_q.reshape(B, S, GQA_Q_HEADS, HEAD_DIM), g_kv[:, :, 0], g_kv[:, :, 1],
                     gqa_q_norm, gqa_k_norm, ang_row, ang_col)
    d_qk = d_qk.reshape(B, S, 2, DIFF_HEADS, 2, DIFF_D)
    y_d = _diff_attention(d_qk[:, :, 0], d_qk[:, :, 1], d_v.reshape(B, S, DIFF_HEADS, DIFF_V),
                          diff_lambda, diff_subln, lam_init)
    branches = jnp.stack([y_a, y_b, y_c, y_d], axis=2)
    branches = jnp.einsum('bsnc,ncd->bsnd', branches, w_branch)
    gates = jax.nn.sigmoid(gate_lin.reshape(B, S, N_BRANCH, D_MODEL) + b_gate)
    return jnp.sum(gates * branches, axis=2) @ w_out


def _swiglu(x, w13, w2):
    a, b = jnp.split(x @ w13, 2, axis=-1)
    return (jax.nn.silu(a) * b) @ w2


def _moe_swiglu(h, w_router, w13, w2):
    B, S, D = h.shape
    xt = h.reshape(B * S, D)
    logits = (xt @ w_router).astype(jnp.float32)
    top_val, top_idx = lax.top_k(logits, TOP_K)
    top_w = jax.nn.softmax(top_val, axis=-1)
    gate = jnp.sum(jax.nn.one_hot(top_idx, N_EXPERTS, dtype=jnp.float32) * top_w[..., None], axis=1)
    out = jnp.zeros((B * S, D), jnp.float32)
    for e in range(N_EXPERTS):
        out = out + gate[:, e:e + 1] * _swiglu(xt, w13[e], w2[e]).astype(jnp.float32)
    return out.astype(h.dtype).reshape(B, S, D)


def setup_inputs(seed: int = 0) -> dict:
    key = jax.random.key(seed)
    ks = iter(jax.random.split(key, 32))
    L = DEPTH
    n_even = (DEPTH + 1) // 2
    n_odd = DEPTH // 2

    def nrm(shape, scale):
        return jax.random.normal(next(ks), shape, jnp.float32) * scale

    def gain(shape):
        return 1.0 + nrm(shape, 0.02)

    return {
        'x': nrm((BATCH, SEQ, D_MODEL), 1.0),
        'ln_emb_g': gain((D_MODEL,)),
        'ln_emb_b': nrm((D_MODEL,), 0.02),
        'w_in': nrm((L, D_MODEL, N_IN), D_MODEL ** -0.5),
        'b_gate': nrm((L, N_BRANCH, D_MODEL), 0.02),
        'mla_q_norm': gain((L, MLA_Q_LORA)),
        'mla_kv_norm': gain((L, MLA_KV_LORA)),
        'mla_w_uq': nrm((L, MLA_Q_LORA, MLA_HEADS * (MLA_NOPE + MLA_ROPE)), MLA_Q_LORA ** -0.5),
        'mla_w_ukv': nrm((L, MLA_KV_LORA, MLA_HEADS * (MLA_NOPE + MLA_V)), MLA_KV_LORA ** -0.5),
        'gqa_q_norm': gain((L, HEAD_DIM)),
        'gqa_k_norm': gain((L, HEAD_DIM)),
        'diff_lambda': nrm((L, 4, DIFF_D), 0.1),
        'diff_subln': gain((L, DIFF_V)),
        'w_branch': nrm((L, N_BRANCH, BRANCH_W, D_MODEL), BRANCH_W ** -0.5),
        'w_out': nrm((L, D_MODEL, D_MODEL), D_MODEL ** -0.5 * DEEPNORM_BETA),
        'ln1_g': gain((L, D_MODEL)),
        'ln1_b': nrm((L, D_MODEL), 0.02),
        'ffn_w13': nrm((n_even, D_MODEL, 2 * D_FF), D_MODEL ** -0.5),
        'ffn_w2': nrm((n_even, D_FF, D_MODEL), D_FF ** -0.5 * DEEPNORM_BETA),
        'moe_router': nrm((n_odd, D_MODEL, N_EXPERTS), D_MODEL ** -0.5),
        'moe_w13': nrm((n_odd, N_EXPERTS, D_MODEL, 2 * D_EXPERT), D_MODEL ** -0.5),
        'moe_w2': nrm((n_odd, N_EXPERTS, D_EXPERT, D_MODEL), D_EXPERT ** -0.5 * DEEPNORM_BETA),
        'ln2_g': gain((L, D_MODEL)),
        'ln2_b': nrm((L, D_MODEL), 0.02),
    }


def reference(x, ln_emb_g, ln_emb_b, w_in, b_gate, mla_q_norm, mla_kv_norm, mla_w_uq, mla_w_ukv,
              gqa_q_norm, gqa_k_norm, diff_lambda, diff_subln, w_branch, w_out, ln1_g, ln1_b,
              ffn_w13, ffn_w2, moe_router, moe_w13, moe_w2, ln2_g, ln2_b):
    S = x.shape[1]
    rows = S // GRID_W
    row_idx = jnp.repeat(jnp.arange(rows, dtype=jnp.float32), GRID_W)
    col_idx = jnp.tile(jnp.arange(GRID_W, dtype=jnp.float32), rows)
    ang_1d = _rope_angles(jnp.arange(S, dtype=jnp.float32), MLA_ROPE)
    ang_row = _rope_angles(row_idx, HEAD_DIM // 2)
    ang_col = _rope_angles(col_idx, HEAD_DIM // 2)
    alpha = DEEPNORM_ALPHA

    h = _layernorm(x, ln_emb_g, ln_emb_b)
    for l in range(DEPTH):
        lam_init = 0.8 - 0.6 * math.exp(-0.3 * l)
        m = _token_mixers(h, w_in[l], b_gate[l], mla_q_norm[l], mla_kv_norm[l], mla_w_uq[l], mla_w_ukv[l],
                          gqa_q_norm[l], gqa_k_norm[l], diff_lambda[l], diff_subln[l], w_branch[l], w_out[l],
                          ang_1d, ang_row, ang_col, lam_init)
        h = _layernorm(alpha * h + m, ln1_g[l], ln1_b[l])
        if l % 2 == 0:
            f = _swiglu(h, ffn_w13[l // 2], ffn_w2[l // 2])
        else:
            f = _moe_swiglu(h, moe_router[l // 2], moe_w13[l // 2], moe_w2[l // 2])
        h = _layernorm(alpha * h + f, ln2_g[l], ln2_b[l])
    return h
```

```python
import functools

import jax
import jax.numpy as jnp
import numpy as np
from jax import lax
from jax.experimental import pallas as pl
from jax.experimental.pallas import tpu as pltpu

D_MODEL = 1024
GRID_W = 64
HEAD_DIM = 64
N_BRANCH = 4
BRANCH_W = 256
MLA_HEADS = 4
MLA_NOPE = 64
MLA_ROPE = 32
MLA_V = 64
MLA_Q_LORA = 192
MLA_KV_LORA = 128
DIL_GROUPS = ((128, 1), (512, 4), (2048, 16))
DIL_HEADS = 4
GQA_Q_HEADS = 4
GQA_KV_HEADS = 2
DIFF_HEADS = 4
DIFF_D = 32
DIFF_V = 64
D_FF = 3584
N_EXPERTS = 8
TOP_K = 2
ROPE_THETA = 10000.0
LN_EPS = 1e-5
RMS_EPS = 1e-6
DIFF_NORM_EPS = 1e-5
DEPTH = 2
DEEPNORM_ALPHA = (2 * DEPTH) ** 0.25

LANES = 128
HALF = LANES // 2
VMEM_LIMIT = 56 * 1024 * 1024

LOG2E = 1.4426950408889634
NEG = -1e30
BF16 = jnp.bfloat16
F32 = jnp.float32

_MLA_COLS = 640
_GQA_COLS = 896
_DIFF_COLS = 768
_DIL_COLS = 2304
_OFF_GQA = _MLA_COLS
_OFF_DIFF = _OFF_GQA + _GQA_COLS
_OFF_DIL = _OFF_DIFF + _DIFF_COLS
_N_ATT = _OFF_DIL + _DIL_COLS


def _cparams(sem):
    return pltpu.CompilerParams(dimension_semantics=sem, vmem_limit_bytes=VMEM_LIMIT)


def _resident(shape, index_map):
    return pl.BlockSpec(shape, index_map, pipeline_mode=pl.Buffered(1))


def _layernorm(x, g, b):
    mu = jnp.mean(x, axis=-1, keepdims=True)
    xc = x - mu
    var = jnp.mean(xc * xc, axis=-1, keepdims=True)
    return xc * lax.rsqrt(var + LN_EPS) * g + b


def _lane(shape):
    return lax.broadcasted_iota(jnp.int32, shape, len(shape) - 1)


def _split_pair(x):
    lo_mask = _lane(x.shape) < HALF
    lo = jnp.where(lo_mask, x, 0.0)
    hi = jnp.where(lo_mask, pltpu.roll(x, HALF, axis=1), 0.0)
    return lo, hi


def _join_pair(lo, hi):
    return jnp.where(_lane(lo.shape) < HALF, lo, pltpu.roll(hi, HALF, axis=1))


def _with_ones(v):
    return jnp.where(_lane(v.shape) == HALF, 1.0, v)


def _proj_kernel(apply_ln, *refs):
    (x_ref, lng_ref, lnb_ref, w1_ref, wuq_ref, wukv_ref, nq_ref, nkv_ref,
     mcq_ref, msq_ref, mck_ref, msk_ref, gaq_ref, gbq_ref, gak_ref, gbk_ref) = refs[:16]
    outs = refs[16:]
    if apply_ln:
        h_ref, outs = outs[0], outs[1:]
    (mq_ref, mk_ref, mv_ref, gq_ref, gk_ref, gv_ref, dq_ref, dk_ref, dv_ref) = outs[:9]
    dil_refs = outs[9:]

    x = x_ref[...]
    if apply_ln:
        x = _layernorm(x, lng_ref[...], lnb_ref[...])
        h_ref[...] = x
    hb = x.astype(BF16)

    def proj(off, n):
        return jnp.dot(hb, w1_ref[:, off:off + n], preferred_element_type=F32)

    p = proj(0, _MLA_COLS)
    cq, ckv, kr, krr = p[:, 0:256], p[:, 256:384], p[:, 384:512], p[:, 512:640]
    rq = lax.rsqrt(jnp.sum(cq * cq, axis=-1, keepdims=True) * (1.0 / MLA_Q_LORA) + RMS_EPS)
    cqn = (cq * rq * nq_ref[...]).astype(BF16)
    q2 = jnp.dot(cqn, wuq_ref[...], preferred_element_type=F32)
    rkv = lax.rsqrt(jnp.sum(ckv * ckv, axis=-1, keepdims=True) * (1.0 / MLA_KV_LORA) + RMS_EPS)
    ckvn = (ckv * rkv * nkv_ref[...]).astype(BF16)
    kv2 = jnp.dot(ckvn, wukv_ref[...], preferred_element_type=F32)
    k_rope = kr * mck_ref[...] + krr * msk_ref[...]
    cq_t, sq_t = mcq_ref[...], msq_ref[...]
    for h in range(MLA_HEADS):
        qh = q2[:, h * LANES:(h + 1) * LANES]
        qrh = q2[:, (MLA_HEADS + h) * LANES:(MLA_HEADS + h + 1) * LANES]
        mq_ref[h] = (qh * cq_t + qrh * sq_t).astype(BF16)
        mk_ref[h] = (kv2[:, h * LANES:(h + 1) * LANES] + k_rope).astype(BF16)
        vh = kv2[:, (MLA_HEADS + h) * LANES:(MLA_HEADS + h + 1) * LANES]
        mv_ref[h] = _with_ones(vh).astype(BF16)

    p = proj(_OFF_GQA, _GQA_COLS)

    def head_rms(xs):
        ss = xs * xs
        lo_mask = _lane(xs.shape) < HALF
        s_all = jnp.sum(ss, axis=-1, keepdims=True)
        s_lo = jnp.sum(jnp.where(lo_mask, ss, 0.0), axis=-1, keepdims=True)
        return lax.rsqrt(jnp.where(lo_mask, s_lo, s_all - s_lo) * (1.0 / HEAD_DIM) + RMS_EPS)

    aq, bq, ak, bk = gaq_ref[...], gbq_ref[...], gak_ref[...], gbk_ref[...]
    for s in range(2):
        qs = p[:, s * LANES:(s + 1) * LANES]
        qrs = p[:, 256 + s * LANES:256 + (s + 1) * LANES]
        qn = head_rms(qs) * (qs * aq + qrs * bq)
        lo, hi = _split_pair(qn)
        gq_ref[2 * s] = lo.astype(BF16)
        gq_ref[2 * s + 1] = hi.astype(BF16)
    ks, krs = p[:, 512:640], p[:, 640:768]
    kn = head_rms(ks) * (ks * ak + krs * bk)
    lo, hi = _split_pair(kn)
    gk_ref[0] = lo.astype(BF16)
    gk_ref[1] = hi.astype(BF16)
    lo, hi = _split_pair(p[:, 768:896])
    gv_ref[0] = _with_ones(lo).astype(BF16)
    gv_ref[1] = _with_ones(hi).astype(BF16)

    p = proj(_OFF_DIFF, _DIFF_COLS)
    dscale = DIFF_D ** -0.5 * LOG2E
    for s in range(2):
        lo, hi = _split_pair(p[:, s * LANES:(s + 1) * LANES] * dscale)
        dq_ref[2 * s] = lo.astype(BF16)
        dq_ref[2 * s + 1] = hi.astype(BF16)
        lo, hi = _split_pair(p[:, 256 + s * LANES:256 + (s + 1) * LANES])
        dk_ref[2 * s] = lo.astype(BF16)
        dk_ref[2 * s + 1] = hi.astype(BF16)
        lo, hi = _split_pair(p[:, 512 + s * LANES:512 + (s + 1) * LANES])
        dv_ref[2 * s] = _with_ones(lo).astype(BF16)
        dv_ref[2 * s + 1] = _with_ones(hi).astype(BF16)

    lscale = HEAD_DIM ** -0.5 * LOG2E
    for i in range(9):
        seg = proj(_OFF_DIL + i * BRANCH_W, BRANCH_W)
        if i < 3:
            seg = seg * lscale
        dil_refs[i][...] = seg.astype(BF16)


def _project(x2d, batch, seq, apply_ln, lng, lnb, w1, wuq, wukv, nq, nkv, tables, tm):
    T = x2d.shape[0]
    nst = seq // tm
    grid = (T // tm,)
    row = lambda i: (i, 0)
    const = lambda i: (0, 0)
    tab = lambda i: (i % nst, 0)
    head = lambda i: (i // nst, 0, i % nst, 0)

    in_specs = [
        pl.BlockSpec((tm, D_MODEL), row),
        pl.BlockSpec((1, D_MODEL), const),
        pl.BlockSpec((1, D_MODEL), const),
        _resident((D_MODEL, _N_ATT), const),
        _resident((256, 8 * LANES), const),
        _resident((LANES, 8 * LANES), const),
        pl.BlockSpec((1, 256), const),
        pl.BlockSpec((1, LANES), const),
    ] + [pl.BlockSpec((tm, LANES), tab)] * 8

    def heads_out(n):
        return (jax.ShapeDtypeStruct((batch, n, seq, LANES), BF16),
                pl.BlockSpec((None, n, tm, LANES), head))

    outs = []
    if apply_ln:
        outs.append((jax.ShapeDtypeStruct((T, D_MODEL), F32), pl.BlockSpec((tm, D_MODEL), row)))
    outs += [heads_out(MLA_HEADS)] * 3
    outs += [heads_out(GQA_Q_HEADS), heads_out(GQA_KV_HEADS), heads_out(GQA_KV_HEADS)]
    outs += [heads_out(DIFF_HEADS)] * 3
    outs += [(jax.ShapeDtypeStruct((T, BRANCH_W), BF16), pl.BlockSpec((tm, BRANCH_W), row))] * 9

    res = pl.pallas_call(
        functools.partial(_proj_kernel, apply_ln),
        grid=grid,
        in_specs=in_specs,
        out_specs=[o[1] for o in outs],
        out_shape=[o[0] for o in outs],
        compiler_params=_cparams(("parallel",)),
        name="proj_prep",
    )(x2d, lng, lnb, w1, wuq, wukv, nq, nkv, *tables)
    return res


def _normalize(acc):
    return acc / acc[:, HALF:HALF + 1]


def _flash_kernel(n_rep, tk, q_ref, k_ref, v_ref, o_ref, acc_ref, m_ref):
    n_heads, tq, _ = q_ref.shape
    seq = k_ref.shape[1]
    outs = []
    for h in range(n_heads):
        hk = h // n_rep
        q = q_ref[h]
        acc_ref[...] = jnp.zeros_like(acc_ref)
        m_ref[...] = jnp.full_like(m_ref, NEG)

        def body(c, carry, q=q, hk=hk):
            ks = pl.multiple_of(c * tk, tk)
            k = k_ref[hk, pl.ds(ks, tk), :]
            v = v_ref[hk, pl.ds(ks, tk), :]
            s = lax.dot_general(q, k, (((1,), (1,)), ((), ())), preferred_element_type=F32)
            m_prev = m_ref[...]
            m_new = jnp.maximum(m_prev, jnp.max(s, axis=-1, keepdims=True))
            p = jnp.exp2(s - m_new[:, 0:1])
            acc_ref[...] = (jnp.exp2(m_prev - m_new) * acc_ref[...]
                            + jnp.dot(p.astype(BF16), v, preferred_element_type=F32))
            m_ref[...] = m_new
            return carry

        lax.fori_loop(0, seq // tk, body, 0)
        outs.append(_normalize(acc_ref[...]))
    for s in range(n_heads // 2):
        o_ref[:, s * LANES:(s + 1) * LANES] = _join_pair(outs[2 * s], outs[2 * s + 1]).astype(o_ref.dtype)


def _flash(q, k, v, tq, tk):
    batch, n_heads, seq, _ = q.shape
    n_kv = k.shape[1]
    return pl.pallas_call(
        functools.partial(_flash_kernel, n_heads // n_kv, tk),
        grid=(batch, seq // tq),
        in_specs=[
            pl.BlockSpec((None, n_heads, tq, LANES), lambda b, i: (b, 0, i, 0)),
            _resident((None, n_kv, seq, LANES), lambda b, i: (b, 0, 0, 0)),
            _resident((None, n_kv, seq, LANES), lambda b, i: (b, 0, 0, 0)),
        ],
        out_specs=pl.BlockSpec((None, tq, n_heads * HALF), lambda b, i: (b, i, 0)),
        out_shape=jax.ShapeDtypeStruct((batch, seq, n_heads * HALF), BF16),
        scratch_shapes=[pltpu.VMEM((tq, LANES), F32), pltpu.VMEM((tq, LANES), F32)],
        compiler_params=_cparams(("parallel", "parallel")),
        name="flash_attn",
    )(q, k, v)


def _diff_kernel(tk, lam_init, q_ref, k_ref, v_ref, lp_ref, sub_ref, o_ref,
                 acc0_ref, acc1_ref, m0_ref, m1_ref):
    n_heads, tq, _ = q_ref.shape
    seq = k_ref.shape[1]
    t0 = pl.program_id(1) * tq
    lp = lp_ref[...]
    lam = (jnp.exp(jnp.sum(lp[0:1] * lp[1:2], keepdims=True))
           - jnp.exp(jnp.sum(lp[2:3] * lp[3:4], keepdims=True)) + lam_init)
    rel = (lax.broadcasted_iota(jnp.int32, (tq, tk), 0)
           - lax.broadcasted_iota(jnp.int32, (tq, tk), 1))
    lane = _lane((tq, LANES))
    map0 = lane < DIFF_D
    map1 = (lane >= DIFF_D) & (lane < 2 * DIFF_D)
    outs = []
    for h in range(n_heads):
        slope = 2.0 ** (-8.0 * (h + 1) / n_heads) * LOG2E
        q = q_ref[h]
        q0 = jnp.where(map0, q, jnp.zeros_like(q))
        q1 = jnp.where(map1, q, jnp.zeros_like(q))
        for r in (acc0_ref, acc1_ref):
            r[...] = jnp.zeros_like(r)
        for r in (m0_ref, m1_ref):
            r[...] = jnp.full_like(r, NEG)

        def body(c, carry, q0=q0, q1=q1, h=h, slope=slope):
            ks = pl.multiple_of(c * tk, tk)
            k = k_ref[h, pl.ds(ks, tk), :]
            v = v_ref[h, pl.ds(ks, tk), :]
            bias = slope * jnp.abs(rel + (t0 - ks)).astype(F32)
            for qm, acc_ref, m_ref in ((q0, acc0_ref, m0_ref), (q1, acc1_ref, m1_ref)):
                s = lax.dot_general(qm, k, (((1,), (1,)), ((), ())), preferred_element_type=F32) - bias
                m_prev = m_ref[...]
                m_new = jnp.maximum(m_prev, jnp.max(s, axis=-1, keepdims=True))
                p = jnp.exp2(s - m_new[:, 0:1])
                acc_ref[...] = (jnp.exp2(m_prev - m_new) * acc_ref[...]
                                + jnp.dot(p.astype(BF16), v, preferred_element_type=F32))
                m_ref[...] = m_new
            return carry

        lax.fori_loop(0, seq // tk, body, 0)
        o = _normalize(acc0_ref[...]) - lam * _normalize(acc1_ref[...])
        o = jnp.where(lane < DIFF_V, o, 0.0)
        ms = jnp.sum(o * o, axis=-1, keepdims=True) * (1.0 / DIFF_V)
        outs.append(o * lax.rsqrt(ms + DIFF_NORM_EPS) * sub_ref[...] * (1.0 - lam_init))
    for s in range(n_heads // 2):
        o_ref[:, s * LANES:(s + 1) * LANES] = _join_pair(outs[2 * s], outs[2 * s + 1]).astype(o_ref.dtype)


def _diff_attention(q, k, v, lam_params, subln, lam_init, tq, tk):
    batch, n_heads, seq, _ = q.shape
    return pl.pallas_call(
        functools.partial(_diff_kernel, tk, lam_init),
        grid=(batch, seq // tq),
        in_specs=[
            pl.BlockSpec((None, n_heads, tq, LANES), lambda b, i: (b, 0, i, 0)),
            _resident((None, n_heads, seq, LANES), lambda b, i: (b, 0, 0, 0)),
            _resident((None, n_heads, seq, LANES), lambda b, i: (b, 0, 0, 0)),
            pl.BlockSpec((4, DIFF_D), lambda b, i: (0, 0)),
            pl.BlockSpec((1, LANES), lambda b, i: (0, 0)),
        ],
        out_specs=pl.BlockSpec((None, tq, n_heads * HALF), lambda b, i: (b, i, 0)),
        out_shape=jax.ShapeDtypeStruct((batch, seq, n_heads * HALF), BF16),
        scratch_shapes=[pltpu.VMEM((tq, LANES), F32)] * 4,
        compiler_params=_cparams(("parallel", "parallel")),
        name="diff_attn",
    )(q, k, v, lam_params, subln)


def _dil_kernel(group, dil, q_ref, kp_ref, kc_ref, kn_ref, vp_ref, vc_ref, vn_ref, o_ref, lse_ref):
    tq = q_ref.shape[0]
    u = pl.program_id(2)
    n_u = pl.num_programs(2)
    half_w = DIL_GROUPS[group][0] // (2 * dil)
    rel = (lax.broadcasted_iota(jnp.int32, (tq, 3 * tq), 1) - tq
           - lax.broadcasted_iota(jnp.int32, (tq, 3 * tq), 0))
    kpos = u * tq + lax.broadcasted_iota(jnp.int32, (tq, 3 * tq), 1) - tq
    absrel = jnp.abs(rel)
    ok = jnp.where(absrel <= half_w, jnp.where(kpos >= 0, jnp.where(kpos < n_u * tq, 1, 0), 0), 0)
    absrel_f = absrel.astype(F32)
    mask_add = jnp.where(ok == 1, 0.0, NEG)
    k3 = jnp.concatenate([kp_ref[...], kc_ref[...], kn_ref[...]], axis=0)
    v3 = jnp.concatenate([vp_ref[...], vc_ref[...], vn_ref[...]], axis=0)
    q = q_ref[...]
    lo_mask = _lane((tq, LANES)) < HALF
    n_slopes = len(DIL_GROUPS) * DIL_HEADS
    for s in range(2):
        qs = q[:, s * LANES:(s + 1) * LANES]
        ksl = k3[:, s * LANES:(s + 1) * LANES]
        vsl = v3[:, s * LANES:(s + 1) * LANES]
        res = []
        for part in range(2):
            h = 2 * s + part
            slope = 2.0 ** (-8.0 * (group * DIL_HEADS + h + 1) / n_slopes) * dil * LOG2E
            keep = lo_mask if part == 0 else jnp.logical_not(lo_mask)
            qm = jnp.where(keep, qs, jnp.zeros_like(qs))
            sc = lax.dot_general(qm, ksl, (((1,), (1,)), ((), ())), preferred_element_type=F32)
            sc = sc - slope * absrel_f + mask_add
            m = jnp.max(sc, axis=-1, keepdims=True)
            p = jnp.exp2(sc - m)
            l = jnp.sum(p, axis=-1, keepdims=True)
            o = jnp.dot(p.astype(BF16), vsl, preferred_element_type=F32) / l
            res.append((o, m + jnp.log2(l)))
        o_ref[:, s * LANES:(s + 1) * LANES] = jnp.where(lo_mask, res[0][0], res[1][0])
        lse_ref[:, s * LANES:(s + 1) * LANES] = jnp.where(lo_mask, res[0][1], res[1][1])


def _dilated_group(q, k, v, batch, seq, group, tq):
    dil = DIL_GROUPS[group][1]
    rows = seq // dil
    n_u = rows // tq
    shape3 = (batch, rows, dil * BRANCH_W)
    q, k, v = (a.reshape(shape3) for a in (q, k, v))
    cur = lambda b, r, u: (b, u, r)
    prev = lambda b, r, u: (b, jnp.maximum(u - 1, 0), r)
    nxt = lambda b, r, u: (b, jnp.minimum(u + 1, n_u - 1), r)
    blk = (None, tq, BRANCH_W)
    o, lse = pl.pallas_call(
        functools.partial(_dil_kernel, group, dil),
        grid=(batch, dil, n_u),
        in_specs=[pl.BlockSpec(blk, cur),
                  pl.BlockSpec(blk, prev), pl.BlockSpec(blk, cur), pl.BlockSpec(blk, nxt),
                  pl.BlockSpec(blk, prev), pl.BlockSpec(blk, cur), pl.BlockSpec(blk, nxt)],
        out_specs=[pl.BlockSpec(blk, cur), pl.BlockSpec(blk, cur)],
        out_shape=[jax.ShapeDtypeStruct(shape3, F32), jax.ShapeDtypeStruct(shape3, F32)],
        compiler_params=_cparams(("parallel", "parallel", "parallel")),
        name=f"dilated_g{group}",
    )(q, k, k, k, v, v, v)
    return o.reshape(batch * seq, BRANCH_W), lse.reshape(batch * seq, BRANCH_W)


def _mix_kernel(h_ref, ya_ref, yc_ref, yd_ref, o0_ref, o1_ref, o2_ref, l0_ref, l1_ref, l2_ref,
                wg_ref, bg_ref, wb_ref, wo_ref, lng_ref, lnb_ref, out_ref):
    h = h_ref[...]
    hb = h.astype(BF16)
    l0, l1, l2 = l0_ref[...], l1_ref[...], l2_ref[...]
    mx = jnp.maximum(jnp.maximum(l0, l1), l2)
    e0, e1, e2 = jnp.exp2(l0 - mx), jnp.exp2(l1 - mx), jnp.exp2(l2 - mx)
    yb = (e0 * o0_ref[...] + e1 * o1_ref[...] + e2 * o2_ref[...]) / (e0 + e1 + e2)
    ys = (ya_ref[...], yb.astype(BF16), yc_ref[...], yd_ref[...])
    acc = None
    for n in range(N_BRANCH):
        logit = jnp.dot(hb, wg_ref[:, n * D_MODEL:(n + 1) * D_MODEL],
                        preferred_element_type=F32) + bg_ref[n:n + 1, :]
        gate = 1.0 / (1.0 + jnp.exp(-logit))
        term = gate * jnp.dot(ys[n], wb_ref[n], preferred_element_type=F32)
        acc = term if acc is None else acc + term
    m = jnp.dot(acc.astype(BF16), wo_ref[...], preferred_element_type=F32)
    out_ref[...] = _layernorm(DEEPNORM_ALPHA * h + m, lng_ref[...], lnb_ref[...])


def _mix(h, ya, yc, yd, dil_o, dil_lse, wg, bg, wb, wo, lng, lnb, tm):
    T = h.shape[0]
    row = lambda i: (i, 0)
    const = lambda i: (0, 0)
    wide = pl.BlockSpec((tm, D_MODEL), row)
    narrow = pl.BlockSpec((tm, BRANCH_W), row)
    return pl.pallas_call(
        _mix_kernel,
        grid=(T // tm,),
        in_specs=[wide] + [narrow] * 9 + [
            _resident((D_MODEL, N_BRANCH * D_MODEL), const),
            pl.BlockSpec((N_BRANCH, D_MODEL), const),
            _resident((N_BRANCH, BRANCH_W, D_MODEL), lambda i: (0, 0, 0)),
            _resident((D_MODEL, D_MODEL), const),
            pl.BlockSpec((1, D_MODEL), const),
            pl.BlockSpec((1, D_MODEL), const),
        ],
        out_specs=wide,
        out_shape=jax.ShapeDtypeStruct((T, D_MODEL), F32),
        compiler_params=_cparams(("parallel",)),
        name="gate_mix",
    )(h, ya, yc, yd, *dil_o, *dil_lse, wg, bg, wb, wo, lng, lnb)


def _swiglu_step(x_ref, w1_ref, w3_ref, w2_ref, acc_ref):
    @pl.when(pl.program_id(1) == 0)
    def _():
        acc_ref[...] = jnp.zeros_like(acc_ref)

    xb = x_ref[...].astype(BF16)
    a = jnp.dot(xb, w1_ref[...], preferred_element_type=F32)
    b = jnp.dot(xb, w3_ref[...], preferred_element_type=F32)
    g = (a / (1.0 + jnp.exp(-a)) * b).astype(BF16)
    acc_ref[...] += jnp.dot(g, w2_ref[...], preferred_element_type=F32)


def _ffn_dense_kernel(x_ref, w1_ref, w3_ref, w2_ref, lng_ref, lnb_ref, o_ref, acc_ref):
    _swiglu_step(x_ref, w1_ref, w3_ref, w2_ref, acc_ref)

    @pl.when(pl.program_id(1) == pl.num_programs(1) - 1)
    def _():
        o_ref[...] = _layernorm(DEEPNORM_ALPHA * x_ref[...] + acc_ref[...], lng_ref[...], lnb_ref[...])


def _ffn_dense(x, w13, w2, lng, lnb, tm, tf):
    T = x.shape[0]
    nf = D_FF // tf
    return pl.pallas_call(
        _ffn_dense_kernel,
        grid=(T // tm, nf),
        in_specs=[
            pl.BlockSpec((tm, D_MODEL), lambda i, j: (i, 0)),
            pl.BlockSpec((D_MODEL, tf), lambda i, j: (0, j)),
            pl.BlockSpec((D_MODEL, tf), lambda i, j: (0, nf + j)),
            pl.BlockSpec((tf, D_MODEL), lambda i, j: (j, 0)),
            pl.BlockSpec((1, D_MODEL), lambda i, j: (0, 0)),
            pl.BlockSpec((1, D_MODEL), lambda i, j: (0, 0)),
        ],
        out_specs=pl.BlockSpec((tm, D_MODEL), lambda i, j: (i, 0)),
        out_shape=jax.ShapeDtypeStruct((T, D_MODEL), F32),
        scratch_shapes=[pltpu.VMEM((tm, D_MODEL), F32)],
        compiler_params=_cparams(("parallel", "arbitrary")),
        name="ffn_dense",
    )(x, w13, w13, w2, lng, lnb)


def _ffn_expert_kernel(te_ref, x_ref, w1_ref, w3_ref, w2_ref, o_ref, acc_ref):
    del te_ref
    _swiglu_step(x_ref, w1_ref, w3_ref, w2_ref, acc_ref)

    @pl.when(pl.program_id(1) == pl.num_programs(1) - 1)
    def _():
        o_ref[...] = acc_ref[...]


def _ffn_experts(tile_expert, x_sorted, w13, w2, tm, tf):
    rows = x_sorted.shape[0]
    nf = D_FF // tf
    grid_spec = pltpu.PrefetchScalarGridSpec(
        num_scalar_prefetch=1,
        grid=(rows // tm, nf),
        in_specs=[
            pl.BlockSpec((tm, D_MODEL), lambda i, j, te: (i, 0)),
            pl.BlockSpec((None, D_MODEL, tf), lambda i, j, te: (te[i], 0, j)),
            pl.BlockSpec((None, D_MODEL, tf), lambda i, j, te: (te[i], 0, nf + j)),
            pl.BlockSpec((None, tf, D_MODEL), lambda i, j, te: (te[i], j, 0)),
        ],
        out_specs=pl.BlockSpec((tm, D_MODEL), lambda i, j, te: (i, 0)),
        scratch_shapes=[pltpu.VMEM((tm, D_MODEL), F32)],
    )
    return pl.pallas_call(
        _ffn_expert_kernel,
        grid_spec=grid_spec,
        out_shape=jax.ShapeDtypeStruct((rows, D_MODEL), F32),
        compiler_params=_cparams(("parallel", "arbitrary")),
        name="ffn_experts",
    )(tile_expert, x_sorted, w13, w13, w2)


def _router_kernel(x_ref, wr_ref, idx_ref, wt_ref):
    logits = jnp.dot(x_ref[...], wr_ref[...], preferred_element_type=F32,
                     precision=lax.Precision.HIGHEST)
    lane = _lane(logits.shape)
    logits = jnp.where(lane < N_EXPERTS, logits, -jnp.inf)
    v1 = jnp.max(logits, axis=-1, keepdims=True)
    i1 = jnp.min(jnp.where(logits == v1, lane, LANES), axis=-1, keepdims=True)
    rest = jnp.where(lane == i1, -jnp.inf, logits)
    v2 = jnp.max(rest, axis=-1, keepdims=True)
    i2 = jnp.min(jnp.where(rest == v2, lane, LANES), axis=-1, keepdims=True)
    e2 = jnp.exp(v2 - v1)
    w1 = 1.0 / (1.0 + e2)
    idx_ref[...] = jnp.where(lane == 0, i1, jnp.where(lane == 1, i2, 0))
    wt_ref[...] = jnp.where(lane == 0, w1, jnp.where(lane == 1, e2 * w1, 0.0))


def _router(x, w_router, tm):
    T = x.shape[0]
    wr = jnp.pad(w_router, ((0, 0), (0, LANES - N_EXPERTS)))
    row = lambda i: (i, 0)
    return pl.pallas_call(
        _router_kernel,
        grid=(T // tm,),
        in_specs=[pl.BlockSpec((tm, D_MODEL), row), pl.BlockSpec((D_MODEL, LANES), lambda i: (0, 0))],
        out_specs=[pl.BlockSpec((tm, LANES), row), pl.BlockSpec((tm, LANES), row)],
        out_shape=[jax.ShapeDtypeStruct((T, LANES), jnp.int32), jax.ShapeDtypeStruct((T, LANES), F32)],
        compiler_params=_cparams(("parallel",)),
        name="router",
    )(x, wr)


def _moe_out_kernel(h_ref, y0_ref, y1_ref, wt_ref, lng_ref, lnb_ref, o_ref):
    wt = wt_ref[...]
    f = wt[:, 0:1] * y0_ref[...] + wt[:, 1:2] * y1_ref[...]
    o_ref[...] = _layernorm(DEEPNORM_ALPHA * h_ref[...] + f, lng_ref[...], lnb_ref[...])


def _moe_out(h, y0, y1, wt, lng, lnb, tm):
    T = h.shape[0]
    row = lambda i: (i, 0)
    const = lambda i: (0, 0)
    wide = pl.BlockSpec((tm, D_MODEL), row)
    return pl.pallas_call(
        _moe_out_kernel,
        grid=(T // tm,),
        in_specs=[wide, wide, wide, pl.BlockSpec((tm, LANES), row),
                  pl.BlockSpec((1, D_MODEL), const), pl.BlockSpec((1, D_MODEL), const)],
        out_specs=wide,
        out_shape=jax.ShapeDtypeStruct((T, D_MODEL), F32),
        compiler_params=_cparams(("parallel",)),
        name="moe_out",
    )(h, y0, y1, wt, lng, lnb)


def _moe(h, w_router, w13, w2, lng, lnb, tm_router, tm_e, tf):
    T = h.shape[0]
    idx, wt = _router(h, w_router, tm_router)
    e_flat = idx[:, :TOP_K].reshape(-1)
    onehot = (e_flat[:, None] == jnp.arange(N_EXPERTS, dtype=jnp.int32)[None, :]).astype(jnp.int32)
    counts = jnp.sum(onehot, axis=0)
    rank = jnp.sum((jnp.cumsum(onehot, axis=0) - onehot) * onehot, axis=1)
    padded = (counts + tm_e - 1) // tm_e * tm_e
    ends = jnp.cumsum(padded)
    pos = (ends - padded)[e_flat] + rank
    rows = TOP_K * T + N_EXPERTS * tm_e
    token = jnp.arange(TOP_K * T, dtype=jnp.int32) // TOP_K
    src = jnp.zeros((rows,), jnp.int32).at[pos].set(token)
    tile_start = jnp.arange(rows // tm_e, dtype=jnp.int32) * tm_e
    tile_expert = jnp.minimum(jnp.searchsorted(ends, tile_start, side="right"), N_EXPERTS - 1).astype(jnp.int32)
    x_sorted = jnp.take(h.astype(BF16), src, axis=0)
    y = _ffn_experts(tile_expert, x_sorted, w13, w2, tm_e, tf)
    pos2 = pos.reshape(T, TOP_K)
    y0 = jnp.take(y, pos2[:, 0], axis=0)
    y1 = jnp.take(y, pos2[:, 1], axis=0)
    return _moe_out(h, y0, y1, wt, lng, lnb, tm_router)


def _rot_cols(w, block):
    d, n = w.shape
    w4 = w.reshape(d, n // block, 2, block // 2)
    return jnp.concatenate([-w4[:, :, 1], w4[:, :, 0]], axis=2).reshape(d, n)


def _attention_weights(w_in):
    o = np.cumsum([0, MLA_Q_LORA, MLA_KV_LORA, MLA_ROPE, 2304, 256, 256, 512, 256, N_BRANCH * D_MODEL])
    cq, ckv, kr, dil, gq, gkv, dqk, dv, gate = (w_in[:, o[i]:o[i + 1]] for i in range(9))
    z = lambda n: jnp.zeros((D_MODEL, n), w_in.dtype)
    mla = [cq, z(64), ckv, z(64), kr, z(32), z(64), _rot_cols(kr, MLA_ROPE), z(32)]
    gqa = [gq, _rot_cols(gq, 32), gkv[:, :128], _rot_cols(gkv[:, :128], 32), gkv[:, 128:]]
    w1 = jnp.concatenate(mla + gqa + [dqk, dv, dil], axis=1)
    return w1.astype(BF16), gate.astype(BF16)


def _mla_up_weights(w_uq, w_ukv):
    zq = lambda n: jnp.zeros((MLA_Q_LORA, n), w_uq.dtype)
    per = MLA_NOPE + MLA_ROPE
    plain, rot = [], []
    for h in range(MLA_HEADS):
        nope = w_uq[:, h * per:h * per + MLA_NOPE]
        rope = w_uq[:, h * per + MLA_NOPE:(h + 1) * per]
        plain += [nope, rope, zq(32)]
        rot += [zq(64), _rot_cols(rope, MLA_ROPE), zq(32)]
    wuq = jnp.concatenate(plain + rot, axis=1)
    wuq = jnp.pad(wuq, ((0, 256 - MLA_Q_LORA), (0, 0)))
    zk = jnp.zeros((MLA_KV_LORA, 64), w_ukv.dtype)
    per = MLA_NOPE + MLA_V
    ks, vs = [], []
    for h in range(MLA_HEADS):
        ks += [w_ukv[:, h * per:h * per + MLA_NOPE], zk]
        vs += [w_ukv[:, h * per + MLA_NOPE:(h + 1) * per], zk]
    wukv = jnp.concatenate(ks + vs, axis=1)
    return wuq.astype(BF16), wukv.astype(BF16)


def _rope_angles(pos, dim):
    inv = ROPE_THETA ** (-(jnp.arange(0, dim, 2, dtype=F32) / dim))
    return pos[:, None] * inv[None, :]


def _position_tables(seq):
    rows = seq // GRID_W
    row_idx = jnp.repeat(jnp.arange(rows, dtype=F32), GRID_W)
    col_idx = jnp.tile(jnp.arange(GRID_W, dtype=F32), rows)
    a1 = _rope_angles(jnp.arange(seq, dtype=F32), MLA_ROPE)
    ar = _rope_angles(row_idx, HEAD_DIM // 2)
    ac = _rope_angles(col_idx, HEAD_DIM // 2)
    cos1 = jnp.concatenate([jnp.cos(a1)] * 2, axis=1)
    sin1 = jnp.concatenate([jnp.sin(a1)] * 2, axis=1)
    cos_ax = jnp.concatenate([jnp.cos(ar)] * 2 + [jnp.cos(ac)] * 2, axis=1)
    sin_ax = jnp.concatenate([jnp.sin(ar)] * 2 + [jnp.sin(ac)] * 2, axis=1)
    return cos1, sin1, cos_ax, sin_ax


def _layer_tables(pos_tabs, gqa_q_norm, gqa_k_norm):
    cos1, sin1, cos_ax, sin_ax = pos_tabs
    seq = cos1.shape[0]
    z = lambda n: jnp.zeros((seq, n), F32)
    sm = (MLA_NOPE + MLA_ROPE) ** -0.5 * LOG2E
    mcq = sm * jnp.concatenate([jnp.ones((seq, MLA_NOPE), F32), cos1, z(32)], axis=1)
    msq = sm * jnp.concatenate([z(MLA_NOPE), sin1, z(32)], axis=1)
    mck = jnp.concatenate([z(MLA_NOPE), cos1, z(32)], axis=1)
    msk = jnp.concatenate([z(MLA_NOPE), sin1, z(32)], axis=1)

    def swap(g):
        return jnp.concatenate([g[16:32], g[0:16], g[48:64], g[32:48]])

    sg = HEAD_DIM ** -0.5 * LOG2E
    two = lambda t: jnp.concatenate([t, t], axis=1)
    gaq = sg * two(cos_ax * gqa_q_norm[None, :])
    gbq = sg * two(sin_ax * swap(gqa_q_norm)[None, :])
    gak = two(cos_ax * gqa_k_norm[None, :])
    gbk = two(sin_ax * swap(gqa_k_norm)[None, :])
    return [mcq, msq, mck, msk, gaq, gbq, gak, gbk]


def _pick(n, pref):
    t = min(n, pref)
    assert n % t == 0, (n, pref)
    return t


def kernel(x, ln_emb_g, ln_emb_b, w_in, b_gate, mla_q_norm, mla_kv_norm, mla_w_uq, mla_w_ukv,
           gqa_q_norm, gqa_k_norm, diff_lambda, diff_subln, w_branch, w_out, ln1_g, ln1_b,
           ffn_w13, ffn_w2, moe_router, moe_w13, moe_w2, ln2_g, ln2_b):
    batch, seq, _ = x.shape
    T = batch * seq
    tm = _pick(seq, 512)
    tq = _pick(seq, 512)
    tk = _pick(seq, 512)
    tq_dil = 128
    row = lambda v: v.reshape(1, -1)
    pos_tabs = _position_tables(seq)

    h = x.reshape(T, D_MODEL)
    for l in range(DEPTH):
        lam_init = 0.8 - 0.6 * float(np.exp(-0.3 * l))
        w1, wg = _attention_weights(w_in[l])
        wuq, wukv = _mla_up_weights(mla_w_uq[l], mla_w_ukv[l])
        tables = _layer_tables(pos_tabs, gqa_q_norm[l], gqa_k_norm[l])
        nq = jnp.pad(mla_q_norm[l], (0, 256 - MLA_Q_LORA)).reshape(1, 256)
        res = _project(h, batch, seq, l == 0, row(ln_emb_g), row(ln_emb_b), w1, wuq, wukv,
                       nq, row(mla_kv_norm[l]), tables, tm)
        if l == 0:
            h, res = res[0], res[1:]
        mq, mk, mv, gq, gk, gv, dq, dk, dv = res[:9]
        dil = res[9:]
        ya = _flash(mq, mk, mv, tq, tk).reshape(T, BRANCH_W)
        yc = _flash(gq, gk, gv, tq, tk).reshape(T, BRANCH_W)
        sub = jnp.pad(diff_subln[l], (0, LANES - DIFF_V)).reshape(1, LANES)
        yd = _diff_attention(dq, dk, dv, diff_lambda[l], sub, lam_init, tq, tk).reshape(T, BRANCH_W)
        dil_o, dil_lse = [], []
        for g in range(len(DIL_GROUPS)):
            o, lse = _dilated_group(dil[g], dil[3 + g], dil[6 + g], batch, seq, g, tq_dil)
            dil_o.append(o)
            dil_lse.append(lse)
        h = _mix(h, ya, yc, yd, dil_o, dil_lse, wg, b_gate[l], w_branch[l].astype(BF16),
                 w_out[l].astype(BF16), row(ln1_g[l]), row(ln1_b[l]), tm)
        if l % 2 == 0:
            h = _ffn_dense(h, ffn_w13[l // 2].astype(BF16), ffn_w2[l // 2].astype(BF16),
                           row(ln2_g[l]), row(ln2_b[l]), _pick(T, 1024), 512)
        else:
            h = _moe(h, moe_router[l // 2], moe_w13[l // 2].astype(BF16), moe_w2[l // 2].astype(BF16),
                     row(ln2_g[l]), row(ln2_b[l]), tm, _pick(T, 512), 512)
    return h.reshape(batch, seq, D_MODEL)
```

```python
import functools

import jax
import jax.numpy as jnp
import numpy as np
from jax import lax
from jax.experimental import pallas as pl
from jax.experimental.pallas import tpu as pltpu

D_MODEL = 1024
GRID_W = 64
HEAD_DIM = 64
N_BRANCH = 4
BRANCH_W = 256
MLA_HEADS = 4
MLA_NOPE = 64
MLA_ROPE = 32
MLA_V = 64
MLA_Q_LORA = 192
MLA_KV_LORA = 128
DIL_GROUPS = ((128, 1), (512, 4), (2048, 16))
DIL_HEADS = 4
GQA_Q_HEADS = 4
GQA_KV_HEADS = 2
DIFF_HEADS = 4
DIFF_D = 32
DIFF_V = 64
D_FF = 3584
N_EXPERTS = 8
TOP_K = 2
ROPE_THETA = 10000.0
LN_EPS = 1e-5
RMS_EPS = 1e-6
DIFF_NORM_EPS = 1e-5
DEPTH = 2
DEEPNORM_ALPHA = (2 * DEPTH) ** 0.25

LANES = 128
HALF = LANES // 2
VMEM_LIMIT = 56 * 1024 * 1024

LOG2E = 1.4426950408889634
NEG = -1e30
BF16 = jnp.bfloat16
F32 = jnp.float32

_MLA_COLS = 640
_GQA_COLS = 896
_DIFF_COLS = 768
_DIL_COLS = 2304
_OFF_GQA = _MLA_COLS
_OFF_DIFF = _OFF_GQA + _GQA_COLS
_OFF_DIL = _OFF_DIFF + _DIFF_COLS
_N_ATT = _OFF_DIL + _DIL_COLS


def _cparams(sem):
    return pltpu.CompilerParams(dimension_semantics=sem, vmem_limit_bytes=VMEM_LIMIT)


def _resident(shape, index_map):
    return pl.BlockSpec(shape, index_map, pipeline_mode=pl.Buffered(1))


def _layernorm(x, g, b):
    mu = jnp.mean(x, axis=-1, keepdims=True)
    xc = x - mu
    var = jnp.mean(xc * xc, axis=-1, keepdims=True)
    return xc * lax.rsqrt(var + LN_EPS) * g + b


def _lane(shape):
    return lax.broadcasted_iota(jnp.int32, shape, len(shape) - 1)


def _split_pair(x):
    lo_mask = _lane(x.shape) < HALF
    lo = jnp.where(lo_mask, x, 0.0)
    hi = jnp.where(lo_mask, pltpu.roll(x, HALF, axis=1), 0.0)
    return lo, hi


def _join_pair(lo, hi):
    return jnp.where(_lane(lo.shape) < HALF, lo, pltpu.roll(hi, HALF, axis=1))


def _with_ones(v):
    return jnp.where(_lane(v.shape) == HALF, 1.0, v)


def _proj_kernel(apply_ln, *refs):
    (x_ref, lng_ref, lnb_ref, w1_ref, wuq_ref, wukv_ref, nq_ref, nkv_ref,
     mcq_ref, msq_ref, mck_ref, msk_ref, gaq_ref, gbq_ref, gak_ref, gbk_ref) = refs[:16]
    outs = refs[16:]
    if apply_ln:
        h_ref, outs = outs[0], outs[1:]
    (mq_ref, mk_ref, mv_ref, gq_ref, gk_ref, gv_ref, dq_ref, dk_ref, dv_ref) = outs[:9]
    dil_refs = outs[9:]

    x = x_ref[...]
    if apply_ln:
        x = _layernorm(x, lng_ref[...], lnb_ref[...])
        h_ref[...] = x
    hb = x.astype(BF16)

    def proj(off, n):
        return jnp.dot(hb, w1_ref[:, off:off + n], preferred_element_type=F32)

    p = proj(0, _MLA_COLS)
    cq, ckv, kr, krr = p[:, 0:256], p[:, 256:384], p[:, 384:512], p[:, 512:640]
    rq = lax.rsqrt(jnp.sum(cq * cq, axis=-1, keepdims=True) * (1.0 / MLA_Q_LORA) + RMS_EPS)
    cqn = (cq * rq * nq_ref[...]).astype(BF16)
    q2 = jnp.dot(cqn, wuq_ref[...], preferred_element_type=F32)
    rkv = lax.rsqrt(jnp.sum(ckv * ckv, axis=-1, keepdims=True) * (1.0 / MLA_KV_LORA) + RMS_EPS)
    ckvn = (ckv * rkv * nkv_ref[...]).astype(BF16)
    kv2 = jnp.dot(ckvn, wukv_ref[...], preferred_element_type=F32)
    k_rope = kr * mck_ref[...] + krr * msk_ref[...]
    cq_t, sq_t = mcq_ref[...], msq_ref[...]
    for h in range(MLA_HEADS):
        qh = q2[:, h * LANES:(h + 1) * LANES]
        qrh = q2[:, (MLA_HEADS + h) * LANES:(MLA_HEADS + h + 1) * LANES]
        mq_ref[h] = (qh * cq_t + qrh * sq_t).astype(BF16)
        mk_ref[h] = (kv2[:, h * LANES:(h + 1) * LANES] + k_rope).astype(BF16)
        vh = kv2[:, (MLA_HEADS + h) * LANES:(MLA_HEADS + h + 1) * LANES]
        mv_ref[h] = _with_ones(vh).astype(BF16)

    p = proj(_OFF_GQA, _GQA_COLS)

    def head_rms(xs):
        ss = xs * xs
        lo_mask = _lane(xs.shape) < HALF
        s_all = jnp.sum(ss, axis=-1, keepdims=True)
        s_lo = jnp.sum(jnp.where(lo_mask, ss, 0.0), axis=-1, keepdims=True)
        return lax.rsqrt(jnp.where(lo_mask, s_lo, s_all - s_lo) * (1.0 / HEAD_DIM) + RMS_EPS)

    aq, bq, ak, bk = gaq_ref[...], gbq_ref[...], gak_ref[...], gbk_ref[...]
    for s in range(2):
        qs = p[:, s * LANES:(s + 1) * LANES]
        qrs = p[:, 256 + s * LANES:256 + (s + 1) * LANES]
        qn = head_rms(qs) * (qs * aq + qrs * bq)
        lo, hi = _split_pair(qn)
        gq_ref[2 * s] = lo.astype(BF16)
        gq_ref[2 * s + 1] = hi.astype(BF16)
    ks, krs = p[:, 512:640], p[:, 640:768]
    kn = head_rms(ks) * (ks * ak + krs * bk)
    lo, hi = _split_pair(kn)
    gk_ref[0] = lo.astype(BF16)
    gk_ref[1] = hi.astype(BF16)
    lo, hi = _split_pair(p[:, 768:896])
    gv_ref[0] = _with_ones(lo).astype(BF16)
    gv_ref[1] = _with_ones(hi).astype(BF16)

    p = proj(_OFF_DIFF, _DIFF_COLS)
    dscale = DIFF_D ** -0.5 * LOG2E
    for s in range(2):
        lo, hi = _split_pair(p[:, s * LANES:(s + 1) * LANES] * dscale)
        dq_ref[2 * s] = lo.astype(BF16)
        dq_ref[2 * s + 1] = hi.astype(BF16)
        lo, hi = _split_pair(p[:, 256 + s * LANES:256 + (s + 1) * LANES])
        dk_ref[2 * s] = lo.astype(BF16)
        dk_ref[2 * s + 1] = hi.astype(BF16)
        lo, hi = _split_pair(p[:, 512 + s * LANES:512 + (s + 1) * LANES])
        dv_ref[2 * s] = _with_ones(lo).astype(BF16)
        dv_ref[2 * s + 1] = _with_ones(hi).astype(BF16)

    lscale = HEAD_DIM ** -0.5 * LOG2E
    for i in range(9):
        seg = proj(_OFF_DIL + i * BRANCH_W, BRANCH_W)
        if i < 3:
            seg = seg * lscale
        dil_refs[i][...] = seg.astype(BF16)


def _project(x2d, batch, seq, apply_ln, lng, lnb, w1, wuq, wukv, nq, nkv, tables, tm):
    T = x2d.shape[0]
    nst = seq // tm
    grid = (T // tm,)
    row = lambda i: (i, 0)
    const = lambda i: (0, 0)
    tab = lambda i: (i % nst, 0)
    head = lambda i: (i // nst, 0, i % nst, 0)

    in_specs = [
        pl.BlockSpec((tm, D_MODEL), row),
        pl.BlockSpec((1, D_MODEL), const),
        pl.BlockSpec((1, D_MODEL), const),
        _resident((D_MODEL, _N_ATT), const),
        _resident((256, 8 * LANES), const),
        _resident((LANES, 8 * LANES), const),
        pl.BlockSpec((1, 256), const),
        pl.BlockSpec((1, LANES), const),
    ] + [pl.BlockSpec((tm, LANES), tab)] * 8

    def heads_out(n):
        return (jax.ShapeDtypeStruct((batch, n, seq, LANES), BF16),
                pl.BlockSpec((None, n, tm, LANES), head))

    outs = []
    if apply_ln:
        outs.append((jax.ShapeDtypeStruct((T, D_MODEL), F32), pl.BlockSpec((tm, D_MODEL), row)))
    outs += [heads_out(MLA_HEADS)] * 3
    outs += [heads_out(GQA_Q_HEADS), heads_out(GQA_KV_HEADS), heads_out(GQA_KV_HEADS)]
    outs += [heads_out(DIFF_HEADS)] * 3
    outs += [(jax.ShapeDtypeStruct((T, BRANCH_W), BF16), pl.BlockSpec((tm, BRANCH_W), row))] * 9

    res = pl.pallas_call(
        functools.partial(_proj_kernel, apply_ln),
        grid=grid,
        in_specs=in_specs,
        out_specs=[o[1] for o in outs],
        out_shape=[o[0] for o in outs],
        compiler_params=_cparams(("parallel",)),
        name="proj_prep",
    )(x2d, lng, lnb, w1, wuq, wukv, nq, nkv, *tables)
    return res


def _normalize(acc):
    return acc / acc[:, HALF:HALF + 1]


def _scores(q, k):
    return lax.dot_general(q, k, (((1,), (1,)), ((), ())), preferred_element_type=F32)


def _softmax_update(s, v, acc_ref, m_ref, idx):
    blocks = [s[:, j * LANES:(j + 1) * LANES] for j in range(s.shape[1] // LANES)]
    part = functools.reduce(jnp.maximum, blocks)
    m_prev = m_ref[idx]
    m_new = jnp.maximum(m_prev, jnp.max(part, axis=-1, keepdims=True))
    p = jnp.concatenate([jnp.exp2(b - m_new).astype(BF16) for b in blocks], axis=1)
    acc_ref[idx] = (jnp.exp2(m_prev - m_new) * acc_ref[idx]
                    + jnp.dot(p, v, preferred_element_type=F32))
    m_ref[idx] = m_new


def _flash_kernel(n_rep, tk, q_ref, k_ref, v_ref, o_ref, acc_ref, m_ref):
    n_heads = q_ref.shape[0]
    seq = k_ref.shape[1]
    acc_ref[...] = jnp.zeros_like(acc_ref)
    m_ref[...] = jnp.full_like(m_ref, NEG)

    def body(c, carry):
        ks = pl.multiple_of(c * tk, tk)
        scores = [_scores(q_ref[h], k_ref[h // n_rep, pl.ds(ks, tk), :]) for h in range(n_heads)]
        for h in range(n_heads):
            _softmax_update(scores[h], v_ref[h // n_rep, pl.ds(ks, tk), :], acc_ref, m_ref, h)
        return carry

    lax.fori_loop(0, seq // tk, body, 0)
    outs = [_normalize(acc_ref[h]) for h in range(n_heads)]
    for s in range(n_heads // 2):
        o_ref[:, s * LANES:(s + 1) * LANES] = _join_pair(outs[2 * s], outs[2 * s + 1]).astype(o_ref.dtype)


def _flash(q, k, v, tq, tk):
    batch, n_heads, seq, _ = q.shape
    n_kv = k.shape[1]
    return pl.pallas_call(
        functools.partial(_flash_kernel, n_heads // n_kv, tk),
        grid=(batch, seq // tq),
        in_specs=[
            pl.BlockSpec((None, n_heads, tq, LANES), lambda b, i: (b, 0, i, 0)),
            _resident((None, n_kv, seq, LANES), lambda b, i: (b, 0, 0, 0)),
            _resident((None, n_kv, seq, LANES), lambda b, i: (b, 0, 0, 0)),
        ],
        out_specs=pl.BlockSpec((None, tq, n_heads * HALF), lambda b, i: (b, i, 0)),
        out_shape=jax.ShapeDtypeStruct((batch, seq, n_heads * HALF), BF16),
        scratch_shapes=[pltpu.VMEM((n_heads, tq, LANES), F32), pltpu.VMEM((n_heads, tq, LANES), F32)],
        compiler_params=_cparams(("parallel", "parallel")),
        name="flash_attn",
    )(q, k, v)


def _diff_kernel(tk, lam_init, q_ref, k_ref, v_ref, lp_ref, sub_ref, o_ref, acc_ref, m_ref):
    n_heads, tq, _ = q_ref.shape
    seq = k_ref.shape[1]
    t0 = pl.program_id(1) * tq
    lp = lp_ref[...]
    lam = (jnp.exp(jnp.sum(lp[0:1] * lp[1:2], keepdims=True))
           - jnp.exp(jnp.sum(lp[2:3] * lp[3:4], keepdims=True)) + lam_init)
    rel = (lax.broadcasted_iota(jnp.int32, (tq, tk), 0)
           - lax.broadcasted_iota(jnp.int32, (tq, tk), 1))
    lane = _lane((tq, LANES))
    map0 = lane < DIFF_D
    map1 = (lane >= DIFF_D) & (lane < 2 * DIFF_D)
    acc_ref[...] = jnp.zeros_like(acc_ref)
    m_ref[...] = jnp.full_like(m_ref, NEG)

    def body(c, carry):
        ks = pl.multiple_of(c * tk, tk)
        dist = jnp.abs(rel + (t0 - ks)).astype(F32)
        scores = []
        for h in range(n_heads):
            q = q_ref[h]
            k = k_ref[h, pl.ds(ks, tk), :]
            scores.append(_scores(jnp.where(map0, q, jnp.zeros_like(q)), k))
            scores.append(_scores(jnp.where(map1, q, jnp.zeros_like(q)), k))
        for h in range(n_heads):
            bias = (2.0 ** (-8.0 * (h + 1) / n_heads) * LOG2E) * dist
            v = v_ref[h, pl.ds(ks, tk), :]
            for i in (2 * h, 2 * h + 1):
                _softmax_update(scores[i] - bias, v, acc_ref, m_ref, i)
        return carry

    lax.fori_loop(0, seq // tk, body, 0)
    outs = []
    for h in range(n_heads):
        o = _normalize(acc_ref[2 * h]) - lam * _normalize(acc_ref[2 * h + 1])
        o = jnp.where(lane < DIFF_V, o, 0.0)
        ms = jnp.sum(o * o, axis=-1, keepdims=True) * (1.0 / DIFF_V)
        outs.append(o * lax.rsqrt(ms + DIFF_NORM_EPS) * sub_ref[...] * (1.0 - lam_init))
    for s in range(n_heads // 2):
        o_ref[:, s * LANES:(s + 1) * LANES] = _join_pair(outs[2 * s], outs[2 * s + 1]).astype(o_ref.dtype)


def _diff_attention(q, k, v, lam_params, subln, lam_init, tq, tk):
    batch, n_heads, seq, _ = q.shape
    return pl.pallas_call(
        functools.partial(_diff_kernel, tk, lam_init),
        grid=(batch, seq // tq),
        in_specs=[
            pl.BlockSpec((None, n_heads, tq, LANES), lambda b, i: (b, 0, i, 0)),
            _resident((None, n_heads, seq, LANES), lambda b, i: (b, 0, 0, 0)),
            _resident((None, n_heads, seq, LANES), lambda b, i: (b, 0, 0, 0)),
            pl.BlockSpec((4, DIFF_D), lambda b, i: (0, 0)),
            pl.BlockSpec((1, LANES), lambda b, i: (0, 0)),
        ],
        out_specs=pl.BlockSpec((None, tq, n_heads * HALF), lambda b, i: (b, i, 0)),
        out_shape=jax.ShapeDtypeStruct((batch, seq, n_heads * HALF), BF16),
        scratch_shapes=[pltpu.VMEM((2 * n_heads, tq, LANES), F32)] * 2,
        compiler_params=_cparams(("parallel", "parallel")),
        name="diff_attn",
    )(q, k, v, lam_params, subln)


def _dil_kernel(group, dil, q_ref, kp_ref, kc_ref, kn_ref, vp_ref, vc_ref, vn_ref, o_ref, lse_ref):
    tq = q_ref.shape[0]
    u = pl.program_id(2)
    n_u = pl.num_programs(2)
    half_w = DIL_GROUPS[group][0] // (2 * dil)
    rel = (lax.broadcasted_iota(jnp.int32, (tq, 3 * tq), 1) - tq
           - lax.broadcasted_iota(jnp.int32, (tq, 3 * tq), 0))
    kpos = u * tq + lax.broadcasted_iota(jnp.int32, (tq, 3 * tq), 1) - tq
    absrel = jnp.abs(rel)
    ok = jnp.where(absrel <= half_w, jnp.where(kpos >= 0, jnp.where(kpos < n_u * tq, 1, 0), 0), 0)
    absrel_f = absrel.astype(F32)
    mask_add = jnp.where(ok == 1, 0.0, NEG)
    k3 = jnp.concatenate([kp_ref[...], kc_ref[...], kn_ref[...]], axis=0)
    v3 = jnp.concatenate([vp_ref[...], vc_ref[...], vn_ref[...]], axis=0)
    q = q_ref[...]
    lo_mask = _lane((tq, LANES)) < HALF
    n_slopes = len(DIL_GROUPS) * DIL_HEADS
    for s in range(2):
        qs = q[:, s * LANES:(s + 1) * LANES]
        ksl = k3[:, s * LANES:(s + 1) * LANES]
        vsl = v3[:, s * LANES:(s + 1) * LANES]
        res = []
        for part in range(2):
            h = 2 * s + part
            slope = 2.0 ** (-8.0 * (group * DIL_HEADS + h + 1) / n_slopes) * dil * LOG2E
            keep = lo_mask if part == 0 else jnp.logical_not(lo_mask)
            qm = jnp.where(keep, qs, jnp.zeros_like(qs))
            sc = lax.dot_general(qm, ksl, (((1,), (1,)), ((), ())), preferred_element_type=F32)
            sc = sc - slope * absrel_f + mask_add
            m = jnp.max(sc, axis=-1, keepdims=True)
            p = jnp.exp2(sc - m)
            l = jnp.sum(p, axis=-1, keepdims=True)
            o = jnp.dot(p.astype(BF16), vsl, preferred_element_type=F32) / l
            res.append((o, m + jnp.log2(l)))
        o_ref[:, s * LANES:(s + 1) * LANES] = jnp.where(lo_mask, res[0][0], res[1][0])
        lse_ref[:, s * LANES:(s + 1) * LANES] = jnp.where(lo_mask, res[0][1], res[1][1])


def _dilated_group(q, k, v, batch, seq, group, tq):
    dil = DIL_GROUPS[group][1]
    rows = seq // dil
    n_u = rows // tq
    shape3 = (batch, rows, dil * BRANCH_W)
    q, k, v = (a.reshape(shape3) for a in (q, k, v))
    cur = lambda b, r, u: (b, u, r)
    prev = lambda b, r, u: (b, jnp.maximum(u - 1, 0), r)
    nxt = lambda b, r, u: (b, jnp.minimum(u + 1, n_u - 1), r)
    blk = (None, tq, BRANCH_W)
    o, lse = pl.pallas_call(
        functools.partial(_dil_kernel, group, dil),
        grid=(batch, dil, n_u),
        in_specs=[pl.BlockSpec(blk, cur),
                  pl.BlockSpec(blk, prev), pl.BlockSpec(blk, cur), pl.BlockSpec(blk, nxt),
                  pl.BlockSpec(blk, prev), pl.BlockSpec(blk, cur), pl.BlockSpec(blk, nxt)],
        out_specs=[pl.BlockSpec(blk, cur), pl.BlockSpec(blk, cur)],
        out_shape=[jax.ShapeDtypeStruct(shape3, F32), jax.ShapeDtypeStruct(shape3, F32)],
        compiler_params=_cparams(("parallel", "parallel", "parallel")),
        name=f"dilated_g{group}",
    )(q, k, k, k, v, v, v)
    return o.reshape(batch * seq, BRANCH_W), lse.reshape(batch * seq, BRANCH_W)


def _mix_kernel(h_ref, ya_ref, yc_ref, yd_ref, o0_ref, o1_ref, o2_ref, l0_ref, l1_ref, l2_ref,
                wg_ref, bg_ref, wb_ref, wo_ref, lng_ref, lnb_ref, out_ref):
    h = h_ref[...]
    hb = h.astype(BF16)
    l0, l1, l2 = l0_ref[...], l1_ref[...], l2_ref[...]
    mx = jnp.maximum(jnp.maximum(l0, l1), l2)
    e0, e1, e2 = jnp.exp2(l0 - mx), jnp.exp2(l1 - mx), jnp.exp2(l2 - mx)
    yb = (e0 * o0_ref[...] + e1 * o1_ref[...] + e2 * o2_ref[...]) / (e0 + e1 + e2)
    ys = (ya_ref[...], yb.astype(BF16), yc_ref[...], yd_ref[...])
    acc = None
    for n in range(N_BRANCH):
        logit = jnp.dot(hb, wg_ref[:, n * D_MODEL:(n + 1) * D_MODEL],
                        preferred_element_type=F32) + bg_ref[n:n + 1, :]
        gate = 1.0 / (1.0 + jnp.exp(-logit))
        term = gate * jnp.dot(ys[n], wb_ref[n], preferred_element_type=F32)
        acc = term if acc is None else acc + term
    m = jnp.dot(acc.astype(BF16), wo_ref[...], preferred_element_type=F32)
    out_ref[...] = _layernorm(DEEPNORM_ALPHA * h + m, lng_ref[...], lnb_ref[...])


def _mix(h, ya, yc, yd, dil_o, dil_lse, wg, bg, wb, wo, lng, lnb, tm):
    T = h.shape[0]
    row = lambda i: (i, 0)
    const = lambda i: (0, 0)
    wide = pl.BlockSpec((tm, D_MODEL), row)
    narrow = pl.BlockSpec((tm, BRANCH_W), row)
    return pl.pallas_call(
        _mix_kernel,
        grid=(T // tm,),
        in_specs=[wide] + [narrow] * 9 + [
            _resident((D_MODEL, N_BRANCH * D_MODEL), const),
            pl.BlockSpec((N_BRANCH, D_MODEL), const),
            _resident((N_BRANCH, BRANCH_W, D_MODEL), lambda i: (0, 0, 0)),
            _resident((D_MODEL, D_MODEL), const),
            pl.BlockSpec((1, D_MODEL), const),
            pl.BlockSpec((1, D_MODEL), const),
        ],
        out_specs=wide,
        out_shape=jax.ShapeDtypeStruct((T, D_MODEL), F32),
        compiler_params=_cparams(("parallel",)),
        name="gate_mix",
    )(h, ya, yc, yd, *dil_o, *dil_lse, wg, bg, wb, wo, lng, lnb)


def _swiglu_step(x_ref, w1_ref, w3_ref, w2_ref, acc_ref):
    @pl.when(pl.program_id(1) == 0)
    def _():
        acc_ref[...] = jnp.zeros_like(acc_ref)

    xb = x_ref[...].astype(BF16)
    a = jnp.dot(xb, w1_ref[...], preferred_element_type=F32)
    b = jnp.dot(xb, w3_ref[...], preferred_element_type=F32)
    g = (a / (1.0 + jnp.exp(-a)) * b).astype(BF16)
    acc_ref[...] += jnp.dot(g, w2_ref[...], preferred_element_type=F32)


def _ffn_dense_kernel(x_ref, w1_ref, w3_ref, w2_ref, lng_ref, lnb_ref, o_ref, acc_ref):
    _swiglu_step(x_ref, w1_ref, w3_ref, w2_ref, acc_ref)

    @pl.when(pl.program_id(1) == pl.num_programs(1) - 1)
    def _():
        o_ref[...] = _layernorm(DEEPNORM_ALPHA * x_ref[...] + acc_ref[...], lng_ref[...], lnb_ref[...])


def _ffn_dense(x, w13, w2, lng, lnb, tm, tf):
    T = x.shape[0]
    nf = D_FF // tf
    return pl.pallas_call(
        _ffn_dense_kernel,
        grid=(T // tm, nf),
        in_specs=[
            pl.BlockSpec((tm, D_MODEL), lambda i, j: (i, 0)),
            pl.BlockSpec((D_MODEL, tf), lambda i, j: (0, j)),
            pl.BlockSpec((D_MODEL, tf), lambda i, j: (0, nf + j)),
            pl.BlockSpec((tf, D_MODEL), lambda i, j: (j, 0)),
            pl.BlockSpec((1, D_MODEL), lambda i, j: (0, 0)),
            pl.BlockSpec((1, D_MODEL), lambda i, j: (0, 0)),
        ],
        out_specs=pl.BlockSpec((tm, D_MODEL), lambda i, j: (i, 0)),
        out_shape=jax.ShapeDtypeStruct((T, D_MODEL), F32),
        scratch_shapes=[pltpu.VMEM((tm, D_MODEL), F32)],
        compiler_params=_cparams(("parallel", "arbitrary")),
        name="ffn_dense",
    )(x, w13, w13, w2, lng, lnb)


def _ffn_expert_kernel(te_ref, x_ref, w1_ref, w3_ref, w2_ref, o_ref, acc_ref):
    del te_ref
    _swiglu_step(x_ref, w1_ref, w3_ref, w2_ref, acc_ref)

    @pl.when(pl.program_id(1) == pl.num_programs(1) - 1)
    def _():
        o_ref[...] = acc_ref[...]


def _ffn_experts(tile_expert, x_sorted, w13, w2, tm, tf):
    rows = x_sorted.shape[0]
    nf = D_FF // tf
    grid_spec = pltpu.PrefetchScalarGridSpec(
        num_scalar_prefetch=1,
        grid=(rows // tm, nf),
        in_specs=[
            pl.BlockSpec((tm, D_MODEL), lambda i, j, te: (i, 0)),
            pl.BlockSpec((None, D_MODEL, tf), lambda i, j, te: (te[i], 0, j)),
            pl.BlockSpec((None, D_MODEL, tf), lambda i, j, te: (te[i], 0, nf + j)),
            pl.BlockSpec((None, tf, D_MODEL), lambda i, j, te: (te[i], j, 0)),
        ],
        out_specs=pl.BlockSpec((tm, D_MODEL), lambda i, j, te: (i, 0)),
        scratch_shapes=[pltpu.VMEM((tm, D_MODEL), F32)],
    )
    return pl.pallas_call(
        _ffn_expert_kernel,
        grid_spec=grid_spec,
        out_shape=jax.ShapeDtypeStruct((rows, D_MODEL), F32),
        compiler_params=_cparams(("parallel", "arbitrary")),
        name="ffn_experts",
    )(tile_expert, x_sorted, w13, w13, w2)


def _router_kernel(x_ref, wr_ref, idx_ref, wt_ref):
    logits = jnp.dot(x_ref[...], wr_ref[...], preferred_element_type=F32,
                     precision=lax.Precision.HIGHEST)
    lane = _lane(logits.shape)
    logits = jnp.where(lane < N_EXPERTS, logits, -jnp.inf)
    v1 = jnp.max(logits, axis=-1, keepdims=True)
    i1 = jnp.min(jnp.where(logits == v1, lane, LANES), axis=-1, keepdims=True)
    rest = jnp.where(lane == i1, -jnp.inf, logits)
    v2 = jnp.max(rest, axis=-1, keepdims=True)
    i2 = jnp.min(jnp.where(rest == v2, lane, LANES), axis=-1, keepdims=True)
    e2 = jnp.exp(v2 - v1)
    w1 = 1.0 / (1.0 + e2)
    idx_ref[...] = jnp.where(lane == 0, i1, jnp.where(lane == 1, i2, 0))
    wt_ref[...] = jnp.where(lane == 0, w1, jnp.where(lane == 1, e2 * w1, 0.0))


def _router(x, w_router, tm):
    T = x.shape[0]
    wr = jnp.pad(w_router, ((0, 0), (0, LANES - N_EXPERTS)))
    row = lambda i: (i, 0)
    return pl.pallas_call(
        _router_kernel,
        grid=(T // tm,),
        in_specs=[pl.BlockSpec((tm, D_MODEL), row), pl.BlockSpec((D_MODEL, LANES), lambda i: (0, 0))],
        out_specs=[pl.BlockSpec((tm, LANES), row), pl.BlockSpec((tm, LANES), row)],
        out_shape=[jax.ShapeDtypeStruct((T, LANES), jnp.int32), jax.ShapeDtypeStruct((T, LANES), F32)],
        compiler_params=_cparams(("parallel",)),
        name="router",
    )(x, wr)


def _moe_out_kernel(h_ref, y0_ref, y1_ref, wt_ref, lng_ref, lnb_ref, o_ref):
    wt = wt_ref[...]
    f = wt[:, 0:1] * y0_ref[...] + wt[:, 1:2] * y1_ref[...]
    o_ref[...] = _layernorm(DEEPNORM_ALPHA * h_ref[...] + f, lng_ref[...], lnb_ref[...])


def _moe_out(h, y0, y1, wt, lng, lnb, tm):
    T = h.shape[0]
    row = lambda i: (i, 0)
    const = lambda i: (0, 0)
    wide = pl.BlockSpec((tm, D_MODEL), row)
    return pl.pallas_call(
        _moe_out_kernel,
        grid=(T // tm,),
        in_specs=[wide, wide, wide, pl.BlockSpec((tm, LANES), row),
                  pl.BlockSpec((1, D_MODEL), const), pl.BlockSpec((1, D_MODEL), const)],
        out_specs=wide,
        out_shape=jax.ShapeDtypeStruct((T, D_MODEL), F32),
        compiler_params=_cparams(("parallel",)),
        name="moe_out",
    )(h, y0, y1, wt, lng, lnb)


def _moe(h, w_router, w13, w2, lng, lnb, tm_router, tm_e, tf):
    T = h.shape[0]
    idx, wt = _router(h, w_router, tm_router)
    e_flat = idx[:, :TOP_K].reshape(-1)
    onehot = (e_flat[:, None] == jnp.arange(N_EXPERTS, dtype=jnp.int32)[None, :]).astype(jnp.int32)
    counts = jnp.sum(onehot, axis=0)
    rank = jnp.sum((jnp.cumsum(onehot, axis=0) - onehot) * onehot, axis=1)
    padded = (counts + tm_e - 1) // tm_e * tm_e
    ends = jnp.cumsum(padded)
    pos = (ends - padded)[e_flat] + rank
    rows = TOP_K * T + N_EXPERTS * tm_e
    token = jnp.arange(TOP_K * T, dtype=jnp.int32) // TOP_K
    src = jnp.zeros((rows,), jnp.int32).at[pos].set(token)
    tile_start = jnp.arange(rows // tm_e, dtype=jnp.int32) * tm_e
    tile_expert = jnp.minimum(jnp.sum((tile_start[:, None] >= ends[None, :]).astype(jnp.int32), axis=1),
                              N_EXPERTS - 1)
    x_sorted = jnp.take(h.astype(BF16), src, axis=0)
    y = _ffn_experts(tile_expert, x_sorted, w13, w2, tm_e, tf)
    pos2 = pos.reshape(T, TOP_K)
    y0 = jnp.take(y, pos2[:, 0], axis=0)
    y1 = jnp.take(y, pos2[:, 1], axis=0)
    return _moe_out(h, y0, y1, wt, lng, lnb, tm_router)


def _rot_cols(w, block):
    d, n = w.shape
    w4 = w.reshape(d, n // block, 2, block // 2)
    return jnp.concatenate([-w4[:, :, 1], w4[:, :, 0]], axis=2).reshape(d, n)


def _attention_weights(w_in):
    o = np.cumsum([0, MLA_Q_LORA, MLA_KV_LORA, MLA_ROPE, 2304, 256, 256, 512, 256, N_BRANCH * D_MODEL])
    cq, ckv, kr, dil, gq, gkv, dqk, dv, gate = (w_in[:, o[i]:o[i + 1]] for i in range(9))
    z = lambda n: jnp.zeros((D_MODEL, n), w_in.dtype)
    mla = [cq, z(64), ckv, z(64), kr, z(32), z(64), _rot_cols(kr, MLA_ROPE), z(32)]
    gqa = [gq, _rot_cols(gq, 32), gkv[:, :128], _rot_cols(gkv[:, :128], 32), gkv[:, 128:]]
    w1 = jnp.concatenate(mla + gqa + [dqk, dv, dil], axis=1)
    return w1.astype(BF16), gate.astype(BF16)


def _mla_up_weights(w_uq, w_ukv):
    zq = lambda n: jnp.zeros((MLA_Q_LORA, n), w_uq.dtype)
    per = MLA_NOPE + MLA_ROPE
    plain, rot = [], []
    for h in range(MLA_HEADS):
        nope = w_uq[:, h * per:h * per + MLA_NOPE]
        rope = w_uq[:, h * per + MLA_NOPE:(h + 1) * per]
        plain += [nope, rope, zq(32)]
        rot += [zq(64), _rot_cols(rope, MLA_ROPE), zq(32)]
    wuq = jnp.concatenate(plain + rot, axis=1)
    wuq = jnp.pad(wuq, ((0, 256 - MLA_Q_LORA), (0, 0)))
    zk = jnp.zeros((MLA_KV_LORA, 64), w_ukv.dtype)
    per = MLA_NOPE + MLA_V
    ks, vs = [], []
    for h in range(MLA_HEADS):
        ks += [w_ukv[:, h * per:h * per + MLA_NOPE], zk]
        vs += [w_ukv[:, h * per + MLA_NOPE:(h + 1) * per], zk]
    wukv = jnp.concatenate(ks + vs, axis=1)
    return wuq.astype(BF16), wukv.astype(BF16)


def _rope_angles(pos, dim):
    inv = ROPE_THETA ** (-(jnp.arange(0, dim, 2, dtype=F32) / dim))
    return pos[:, None] * inv[None, :]


def _position_tables(seq):
    rows = seq // GRID_W
    row_idx = jnp.repeat(jnp.arange(rows, dtype=F32), GRID_W)
    col_idx = jnp.tile(jnp.arange(GRID_W, dtype=F32), rows)
    a1 = _rope_angles(jnp.arange(seq, dtype=F32), MLA_ROPE)
    ar = _rope_angles(row_idx, HEAD_DIM // 2)
    ac = _rope_angles(col_idx, HEAD_DIM // 2)
    cos1 = jnp.concatenate([jnp.cos(a1)] * 2, axis=1)
    sin1 = jnp.concatenate([jnp.sin(a1)] * 2, axis=1)
    cos_ax = jnp.concatenate([jnp.cos(ar)] * 2 + [jnp.cos(ac)] * 2, axis=1)
    sin_ax = jnp.concatenate([jnp.sin(ar)] * 2 + [jnp.sin(ac)] * 2, axis=1)
    return cos1, sin1, cos_ax, sin_ax


def _layer_tables(pos_tabs, gqa_q_norm, gqa_k_norm):
    cos1, sin1, cos_ax, sin_ax = pos_tabs
    seq = cos1.shape[0]
    z = lambda n: jnp.zeros((seq, n), F32)
    sm = (MLA_NOPE + MLA_ROPE) ** -0.5 * LOG2E
    mcq = sm * jnp.concatenate([jnp.ones((seq, MLA_NOPE), F32), cos1, z(32)], axis=1)
    msq = sm * jnp.concatenate([z(MLA_NOPE), sin1, z(32)], axis=1)
    mck = jnp.concatenate([z(MLA_NOPE), cos1, z(32)], axis=1)
    msk = jnp.concatenate([z(MLA_NOPE), sin1, z(32)], axis=1)

    def swap(g):
        return jnp.concatenate([g[16:32], g[0:16], g[48:64], g[32:48]])

    sg = HEAD_DIM ** -0.5 * LOG2E
    two = lambda t: jnp.concatenate([t, t], axis=1)
    gaq = sg * two(cos_ax * gqa_q_norm[None, :])
    gbq = sg * two(sin_ax * swap(gqa_q_norm)[None, :])
    gak = two(cos_ax * gqa_k_norm[None, :])
    gbk = two(sin_ax * swap(gqa_k_norm)[None, :])
    return [mcq, msq, mck, msk, gaq, gbq, gak, gbk]


def _pick(n, pref):
    t = min(n, pref)
    assert n % t == 0, (n, pref)
    return t


def kernel(x, ln_emb_g, ln_emb_b, w_in, b_gate, mla_q_norm, mla_kv_norm, mla_w_uq, mla_w_ukv,
           gqa_q_norm, gqa_k_norm, diff_lambda, diff_subln, w_branch, w_out, ln1_g, ln1_b,
           ffn_w13, ffn_w2, moe_router, moe_w13, moe_w2, ln2_g, ln2_b):
    batch, seq, _ = x.shape
    T = batch * seq
    tm = _pick(seq, 512)
    tq = _pick(seq, 512)
    tk = _pick(seq, 512)
    tq_dil = 128
    row = lambda v: v.reshape(1, -1)
    pos_tabs = _position_tables(seq)

    h = x.reshape(T, D_MODEL)
    for l in range(DEPTH):
        lam_init = 0.8 - 0.6 * float(np.exp(-0.3 * l))
        w1, wg = _attention_weights(w_in[l])
        wuq, wukv = _mla_up_weights(mla_w_uq[l], mla_w_ukv[l])
        tables = _layer_tables(pos_tabs, gqa_q_norm[l], gqa_k_norm[l])
        nq = jnp.pad(mla_q_norm[l], (0, 256 - MLA_Q_LORA)).reshape(1, 256)
        res = _project(h, batch, seq, l == 0, row(ln_emb_g), row(ln_emb_b), w1, wuq, wukv,
                       nq, row(mla_kv_norm[l]), tables, tm)
        if l == 0:
            h, res = res[0], res[1:]
        mq, mk, mv, gq, gk, gv, dq, dk, dv = res[:9]
        dil = res[9:]
        ya = _flash(mq, mk, mv, tq, tk).reshape(T, BRANCH_W)
        yc = _flash(gq, gk, gv, tq, tk).reshape(T, BRANCH_W)
        sub = jnp.pad(diff_subln[l], (0, LANES - DIFF_V)).reshape(1, LANES)
        yd = _diff_attention(dq, dk, dv, diff_lambda[l], sub, lam_init, tq, tk).reshape(T, BRANCH_W)
        dil_o, dil_lse = [], []
        for g in range(len(DIL_GROUPS)):
            o, lse = _dilated_group(dil[g], dil[3 + g], dil[6 + g], batch, seq, g, tq_dil)
            dil_o.append(o)
            dil_lse.append(lse)
        h = _mix(h, ya, yc, yd, dil_o, dil_lse, wg, b_gate[l], w_branch[l].astype(BF16),
                 w_out[l].astype(BF16), row(ln1_g[l]), row(ln1_b[l]), tm)
        if l % 2 == 0:
            h = _ffn_dense(h, ffn_w13[l // 2].astype(BF16), ffn_w2[l // 2].astype(BF16),
                           row(ln2_g[l]), row(ln2_b[l]), _pick(T, 1024), 512)
        else:
            h = _moe(h, moe_router[l // 2], moe_w13[l // 2].astype(BF16), moe_w2[l // 2].astype(BF16),
                     row(ln2_g[l]), row(ln2_b[l]), tm, _pick(T, 512), 512)
    return h.reshape(batch, seq, D_MODEL)
```

```python
import functools

import jax
import jax.numpy as jnp
import numpy as np
from jax import lax
from jax.experimental import pallas as pl
from jax.experimental.pallas import tpu as pltpu

D_MODEL = 1024
GRID_W = 64
HEAD_DIM = 64
N_BRANCH = 4
BRANCH_W = 256
MLA_HEADS = 4
MLA_NOPE = 64
MLA_ROPE = 32
MLA_V = 64
MLA_Q_LORA = 192
MLA_KV_LORA = 128
DIL_GROUPS = ((128, 1), (512, 4), (2048, 16))
DIL_HEADS = 4
GQA_Q_HEADS = 4
GQA_KV_HEADS = 2
DIFF_HEADS = 4
DIFF_D = 32
DIFF_V = 64
D_FF = 3584
N_EXPERTS = 8
TOP_K = 2
ROPE_THETA = 10000.0
LN_EPS = 1e-5
RMS_EPS = 1e-6
DIFF_NORM_EPS = 1e-5
DEPTH = 2
DEEPNORM_ALPHA = (2 * DEPTH) ** 0.25

LANES = 128
HALF = LANES // 2
VMEM_LIMIT = 56 * 1024 * 1024

LOG2E = 1.4426950408889634
NEG = -1e30
BF16 = jnp.bfloat16
F32 = jnp.float32

_MLA_COLS = 640
_GQA_COLS = 896
_DIFF_COLS = 768
_DIL_COLS = 2304
_OFF_GQA = _MLA_COLS
_OFF_DIFF = _OFF_GQA + _GQA_COLS
_OFF_DIL = _OFF_DIFF + _DIFF_COLS
_N_ATT = _OFF_DIL + _DIL_COLS


def _cparams(sem):
    return pltpu.CompilerParams(dimension_semantics=sem, vmem_limit_bytes=VMEM_LIMIT)


def _resident(shape, index_map):
    return pl.BlockSpec(shape, index_map, pipeline_mode=pl.Buffered(1))


def _layernorm(x, g, b):
    mu = jnp.mean(x, axis=-1, keepdims=True)
    xc = x - mu
    var = jnp.mean(xc * xc, axis=-1, keepdims=True)
    return xc * lax.rsqrt(var + LN_EPS) * g + b


def _lane(shape):
    return lax.broadcasted_iota(jnp.int32, shape, len(shape) - 1)


def _split_pair(x):
    lo_mask = _lane(x.shape) < HALF
    lo = jnp.where(lo_mask, x, 0.0)
    hi = jnp.where(lo_mask, pltpu.roll(x, HALF, axis=1), 0.0)
    return lo, hi


def _join_pair(lo, hi):
    return jnp.where(_lane(lo.shape) < HALF, lo, pltpu.roll(hi, HALF, axis=1))


def _with_ones(v):
    return jnp.where(_lane(v.shape) == HALF, 1.0, v)


def _diff_slope(h):
    return 2.0 ** (-8.0 * (h + 1) / DIFF_HEADS) * LOG2E


def _split3(val):
    f1 = val.astype(BF16).astype(F32)
    r1 = val - f1
    f2 = r1.astype(BF16).astype(F32)
    f3 = (r1 - f2).astype(BF16).astype(F32)
    return f1, f2, f3


_FEAT = 2 * DIFF_D


def _key_bias_features(val):
    f1, f2, f3 = _split3(val)
    lane = _lane(val.shape)
    ones = jnp.where((lane >= _FEAT + 3) & (lane < _FEAT + 6), 1.0, 0.0)
    return jnp.where(lane == _FEAT, f1, jnp.where(lane == _FEAT + 1, f2, jnp.where(lane == _FEAT + 2, f3, ones)))


def _proj_kernel(apply_ln, tiles_per_seq, *refs):
    (x_ref, lng_ref, lnb_ref, w1_ref, wuq_ref, wukv_ref, nq_ref, nkv_ref,
     mcq_ref, msq_ref, mck_ref, msk_ref, gaq_ref, gbq_ref, gak_ref, gbk_ref) = refs[:16]
    outs = refs[16:]
    if apply_ln:
        h_ref, outs = outs[0], outs[1:]
    (mq_ref, mk_ref, mv_ref, gq_ref, gk_ref, gv_ref, dq_ref, dk_ref, dv_ref) = outs[:9]
    dil_refs = outs[9:]

    x = x_ref[...]
    if apply_ln:
        x = _layernorm(x, lng_ref[...], lnb_ref[...])
        h_ref[...] = x
    hb = x.astype(BF16)

    def proj(off, n):
        return jnp.dot(hb, w1_ref[:, off:off + n], preferred_element_type=F32)

    p = proj(0, _MLA_COLS)
    cq, ckv, kr, krr = p[:, 0:256], p[:, 256:384], p[:, 384:512], p[:, 512:640]
    rq = lax.rsqrt(jnp.sum(cq * cq, axis=-1, keepdims=True) * (1.0 / MLA_Q_LORA) + RMS_EPS)
    cqn = (cq * rq * nq_ref[...]).astype(BF16)
    q2 = jnp.dot(cqn, wuq_ref[...], preferred_element_type=F32)
    rkv = lax.rsqrt(jnp.sum(ckv * ckv, axis=-1, keepdims=True) * (1.0 / MLA_KV_LORA) + RMS_EPS)
    ckvn = (ckv * rkv * nkv_ref[...]).astype(BF16)
    kv2 = jnp.dot(ckvn, wukv_ref[...], preferred_element_type=F32)
    k_rope = kr * mck_ref[...] + krr * msk_ref[...]
    cq_t, sq_t = mcq_ref[...], msq_ref[...]
    for h in range(MLA_HEADS):
        qh = q2[:, h * LANES:(h + 1) * LANES]
        qrh = q2[:, (MLA_HEADS + h) * LANES:(MLA_HEADS + h + 1) * LANES]
        mq_ref[h] = (qh * cq_t + qrh * sq_t).astype(BF16)
        mk_ref[h] = (kv2[:, h * LANES:(h + 1) * LANES] + k_rope).astype(BF16)
        vh = kv2[:, (MLA_HEADS + h) * LANES:(MLA_HEADS + h + 1) * LANES]
        mv_ref[h] = _with_ones(vh).astype(BF16)

    p = proj(_OFF_GQA, _GQA_COLS)

    def head_rms(xs):
        ss = xs * xs
        lo_mask = _lane(xs.shape) < HALF
        s_all = jnp.sum(ss, axis=-1, keepdims=True)
        s_lo = jnp.sum(jnp.where(lo_mask, ss, 0.0), axis=-1, keepdims=True)
        return lax.rsqrt(jnp.where(lo_mask, s_lo, s_all - s_lo) * (1.0 / HEAD_DIM) + RMS_EPS)

    aq, bq, ak, bk = gaq_ref[...], gbq_ref[...], gak_ref[...], gbk_ref[...]
    for s in range(2):
        qs = p[:, s * LANES:(s + 1) * LANES]
        qrs = p[:, 256 + s * LANES:256 + (s + 1) * LANES]
        qn = head_rms(qs) * (qs * aq + qrs * bq)
        lo, hi = _split_pair(qn)
        gq_ref[2 * s] = lo.astype(BF16)
        gq_ref[2 * s + 1] = hi.astype(BF16)
    ks, krs = p[:, 512:640], p[:, 640:768]
    kn = head_rms(ks) * (ks * ak + krs * bk)
    lo, hi = _split_pair(kn)
    gk_ref[0] = lo.astype(BF16)
    gk_ref[1] = hi.astype(BF16)
    lo, hi = _split_pair(p[:, 768:896])
    gv_ref[0] = _with_ones(lo).astype(BF16)
    gv_ref[1] = _with_ones(hi).astype(BF16)

    p = proj(_OFF_DIFF, _DIFF_COLS)
    dscale = DIFF_D ** -0.5 * LOG2E
    tm = x.shape[0]
    pos = (pl.program_id(0) % tiles_per_seq * tm
           + lax.broadcasted_iota(jnp.int32, (tm, LANES), 0)).astype(F32)
    for s in range(2):
        lo, hi = _split_pair(p[:, s * LANES:(s + 1) * LANES] * dscale)
        dq_ref[2 * s] = lo.astype(BF16)
        dq_ref[2 * s + 1] = hi.astype(BF16)
        lo, hi = _split_pair(p[:, 256 + s * LANES:256 + (s + 1) * LANES])
        dk_ref[2 * s] = (lo + _key_bias_features(_diff_slope(2 * s) * pos)).astype(BF16)
        dk_ref[2 * s + 1] = (hi + _key_bias_features(_diff_slope(2 * s + 1) * pos)).astype(BF16)
        lo, hi = _split_pair(p[:, 512 + s * LANES:512 + (s + 1) * LANES])
        dv_ref[2 * s] = _with_ones(lo).astype(BF16)
        dv_ref[2 * s + 1] = _with_ones(hi).astype(BF16)

    lscale = HEAD_DIM ** -0.5 * LOG2E
    for i in range(9):
        seg = proj(_OFF_DIL + i * BRANCH_W, BRANCH_W)
        if i < 3:
            seg = seg * lscale
        dil_refs[i][0] = seg[:, :LANES]
        dil_refs[i][1] = seg[:, LANES:]


def _project(x2d, batch, seq, apply_ln, lng, lnb, w1, wuq, wukv, nq, nkv, tables, tm):
    T = x2d.shape[0]
    nst = seq // tm
    grid = (T // tm,)
    row = lambda i: (i, 0)
    const = lambda i: (0, 0)
    tab = lambda i: (i % nst, 0)
    head = lambda i: (i // nst, 0, i % nst, 0)

    in_specs = [
        pl.BlockSpec((tm, D_MODEL), row),
        pl.BlockSpec((1, D_MODEL), const),
        pl.BlockSpec((1, D_MODEL), const),
        _resident((D_MODEL, _N_ATT), const),
        _resident((256, 8 * LANES), const),
        _resident((LANES, 8 * LANES), const),
        pl.BlockSpec((1, 256), const),
        pl.BlockSpec((1, LANES), const),
    ] + [pl.BlockSpec((tm, LANES), tab)] * 8

    def heads_out(n):
        return (jax.ShapeDtypeStruct((batch, n, seq, LANES), BF16),
                pl.BlockSpec((None, n, tm, LANES), head))

    outs = []
    if apply_ln:
        outs.append((jax.ShapeDtypeStruct((T, D_MODEL), F32), pl.BlockSpec((tm, D_MODEL), row)))
    outs += [heads_out(MLA_HEADS)] * 3
    outs += [heads_out(GQA_Q_HEADS), heads_out(GQA_KV_HEADS), heads_out(GQA_KV_HEADS)]
    outs += [heads_out(DIFF_HEADS)] * 3
    outs += [(jax.ShapeDtypeStruct((batch, 2, seq, LANES), F32), pl.BlockSpec((None, 2, tm, LANES), head))] * 9

    res = pl.pallas_call(
        functools.partial(_proj_kernel, apply_ln, nst),
        grid=grid,
        in_specs=in_specs,
        out_specs=[o[1] for o in outs],
        out_shape=[o[0] for o in outs],
        compiler_params=_cparams(("parallel",)),
        name="proj_prep",
    )(x2d, lng, lnb, w1, wuq, wukv, nq, nkv, *tables)
    return res


def _normalize(acc):
    return acc / acc[:, HALF:HALF + 1]


def _scores(q, k):
    return lax.dot_general(q, k, (((1,), (1,)), ((), ())), preferred_element_type=F32)


def _softmax_update(s, v, acc_ref, m_ref, idx):
    blocks = [s[:, j * LANES:(j + 1) * LANES] for j in range(s.shape[1] // LANES)]
    part = functools.reduce(jnp.maximum, blocks)
    m_prev = m_ref[idx]
    m_new = jnp.maximum(m_prev, jnp.max(part, axis=-1, keepdims=True))
    p = jnp.concatenate([jnp.exp2((b - m_new).astype(BF16)) for b in blocks], axis=1)
    acc_ref[idx] = (jnp.exp2(m_prev - m_new) * acc_ref[idx]
                    + jnp.dot(p, v, preferred_element_type=F32))
    m_ref[idx] = m_new


def _flash_kernel(n_rep, tk, q_ref, k_ref, v_ref, o_ref, acc_ref, m_ref):
    n_heads = q_ref.shape[0]
    seq = k_ref.shape[1]
    acc_ref[...] = jnp.zeros_like(acc_ref)
    m_ref[...] = jnp.full_like(m_ref, NEG)

    def body(c, carry):
        ks = pl.multiple_of(c * tk, tk)
        scores = [_scores(q_ref[h], k_ref[h // n_rep, pl.ds(ks, tk), :]) for h in range(n_heads)]
        for h in range(n_heads):
            _softmax_update(scores[h], v_ref[h // n_rep, pl.ds(ks, tk), :], acc_ref, m_ref, h)
        return carry

    lax.fori_loop(0, seq // tk, body, 0)
    outs = [_normalize(acc_ref[h]) for h in range(n_heads)]
    for s in range(n_heads // 2):
        o_ref[:, s * LANES:(s + 1) * LANES] = _join_pair(outs[2 * s], outs[2 * s + 1]).astype(o_ref.dtype)


def _flash(q, k, v, tq, tk):
    batch, n_heads, seq, _ = q.shape
    n_kv = k.shape[1]
    return pl.pallas_call(
        functools.partial(_flash_kernel, n_heads // n_kv, tk),
        grid=(batch, seq // tq),
        in_specs=[
            pl.BlockSpec((None, n_heads, tq, LANES), lambda b, i: (b, 0, i, 0)),
            _resident((None, n_kv, seq, LANES), lambda b, i: (b, 0, 0, 0)),
            _resident((None, n_kv, seq, LANES), lambda b, i: (b, 0, 0, 0)),
        ],
        out_specs=pl.BlockSpec((None, tq, n_heads * HALF), lambda b, i: (b, i, 0)),
        out_shape=jax.ShapeDtypeStruct((batch, seq, n_heads * HALF), BF16),
        scratch_shapes=[pltpu.VMEM((n_heads, tq, LANES), F32), pltpu.VMEM((n_heads, tq, LANES), F32)],
        compiler_params=_cparams(("parallel", "parallel")),
        name="flash_attn",
    )(q, k, v)


def _diff_kernel(lam_init, q_ref, k_ref, v_ref, lp_ref, sub_ref, o_ref, acc_ref, m_ref, qv_ref):
    n_heads, tq, _ = q_ref.shape
    seq = k_ref.shape[1]
    tile = pl.program_id(1)
    t0 = tile * tq
    lp = lp_ref[...]
    lam = (jnp.exp(jnp.sum(lp[0:1] * lp[1:2], keepdims=True))
           - jnp.exp(jnp.sum(lp[2:3] * lp[3:4], keepdims=True)) + lam_init)
    lane = _lane((tq, LANES))
    tpos = (t0 + lax.broadcasted_iota(jnp.int32, (tq, LANES), 0)).astype(F32)
    in_f = (lane >= _FEAT) & (lane < _FEAT + 3)
    for h in range(n_heads):
        q = q_ref[h].astype(F32)
        g1, g2, g3 = _split3(2.0 * _diff_slope(h) * tpos)
        left = jnp.where(in_f, 1.0, 0.0)
        right = jnp.where(in_f, -1.0, jnp.where(lane == _FEAT + 3, g1, jnp.where(
            lane == _FEAT + 4, g2, jnp.where(lane == _FEAT + 5, g3, 0.0))))
        for m in range(2):
            qm = jnp.where((lane >= m * DIFF_D) & (lane < (m + 1) * DIFF_D), q, 0.0)
            qv_ref[0, 2 * h + m] = (qm + left).astype(BF16)
            qv_ref[1, 2 * h + m] = (qm + right).astype(BF16)
            qv_ref[2, 2 * h + m] = qm.astype(BF16)
    acc_ref[...] = jnp.zeros_like(acc_ref)
    m_ref[...] = jnp.full_like(m_ref, NEG)

    def chunk(side, ks, bias):
        scores = [_scores(qv_ref[side, i], k_ref[i // 2, pl.ds(ks, tq), :]) for i in range(2 * n_heads)]
        for i in range(2 * n_heads):
            s = scores[i] if bias is None else scores[i] - bias[i // 2]
            _softmax_update(s, v_ref[i // 2, pl.ds(ks, tq), :], acc_ref, m_ref, i)

    def side_body(side):
        def body(c, carry):
            chunk(side, pl.multiple_of(c * tq, tq), None)
            return carry
        return body

    lax.fori_loop(0, tile, side_body(0), 0)
    dist = jnp.abs(lax.broadcasted_iota(jnp.int32, (tq, tq), 0)
                   - lax.broadcasted_iota(jnp.int32, (tq, tq), 1)).astype(F32)
    trow = (t0 + lax.broadcasted_iota(jnp.int32, (tq, tq), 0)).astype(F32)
    chunk(2, pl.multiple_of(t0, tq), [_diff_slope(h) * (dist - trow) for h in range(n_heads)])
    lax.fori_loop(tile + 1, seq // tq, side_body(1), 0)

    outs = []
    for h in range(n_heads):
        o = _normalize(acc_ref[2 * h]) - lam * _normalize(acc_ref[2 * h + 1])
        o = jnp.where(lane < DIFF_V, o, 0.0)
        ms = jnp.sum(o * o, axis=-1, keepdims=True) * (1.0 / DIFF_V)
        outs.append(o * lax.rsqrt(ms + DIFF_NORM_EPS) * sub_ref[...] * (1.0 - lam_init))
    for s in range(n_heads // 2):
        o_ref[:, s * LANES:(s + 1) * LANES] = _join_pair(outs[2 * s], outs[2 * s + 1]).astype(o_ref.dtype)


def _diff_attention(q, k, v, lam_params, subln, lam_init, tq):
    batch, n_heads, seq, _ = q.shape
    return pl.pallas_call(
        functools.partial(_diff_kernel, lam_init),
        grid=(batch, seq // tq),
        in_specs=[
            pl.BlockSpec((None, n_heads, tq, LANES), lambda b, i: (b, 0, i, 0)),
            _resident((None, n_heads, seq, LANES), lambda b, i: (b, 0, 0, 0)),
            _resident((None, n_heads, seq, LANES), lambda b, i: (b, 0, 0, 0)),
            pl.BlockSpec((4, DIFF_D), lambda b, i: (0, 0)),
            pl.BlockSpec((1, LANES), lambda b, i: (0, 0)),
        ],
        out_specs=pl.BlockSpec((None, tq, n_heads * HALF), lambda b, i: (b, i, 0)),
        out_shape=jax.ShapeDtypeStruct((batch, seq, n_heads * HALF), BF16),
        scratch_shapes=[pltpu.VMEM((2 * n_heads, tq, LANES), F32)] * 2
        + [pltpu.VMEM((3, 2 * n_heads, tq, LANES), BF16)],
        compiler_params=_cparams(("parallel", "parallel")),
        name="diff_attn",
    )(q, k, v, lam_params, subln)


_DIL_BLOCK = 2048
_DIL_Q = 128


def _dil_kernel(group, dil, seq, q_ref, kp_ref, kc_ref, kn_ref, vp_ref, vc_ref, vn_ref,
                o_ref, lse_ref, kw_ref, vw_ref):
    tb = q_ref.shape[1]
    halo = DIL_GROUPS[group][0] // 2
    assert halo == kp_ref.shape[1] and halo == (_DIL_Q // 2) * dil
    for w_ref, p_ref, c_ref, n_ref in ((kw_ref, kp_ref, kc_ref, kn_ref), (vw_ref, vp_ref, vc_ref, vn_ref)):
        w_ref[:, 0:halo] = p_ref[...]
        w_ref[:, halo:halo + tb] = c_ref[...]
        w_ref[:, halo + tb:halo + tb + halo] = n_ref[...]
    t0 = pl.program_id(1) * tb
    shape = (_DIL_Q, 2 * _DIL_Q)
    rel = lax.broadcasted_iota(jnp.int32, shape, 1) - lax.broadcasted_iota(jnp.int32, shape, 0) - _DIL_Q // 2
    absrel = jnp.abs(rel).astype(F32)
    band = jnp.where(absrel <= _DIL_Q // 2, 0.0, -NEG)
    n_slopes = len(DIL_GROUPS) * DIL_HEADS
    bias = [2.0 ** (-8.0 * (group * DIL_HEADS + h + 1) / n_slopes) * dil * LOG2E * absrel + band
            for h in range(DIL_HEADS)]
    col = lax.broadcasted_iota(jnp.int32, (1, 2 * _DIL_Q), 1)
    lo_mask = _lane((_DIL_Q, LANES)) < HALF
    for unit in range(tb // _DIL_Q):
        first = (unit // dil) * _DIL_Q * dil + unit % dil
        rows = pl.ds(first, _DIL_Q, stride=dil)
        win = pl.ds(first, 2 * _DIL_Q, stride=dil)
        key_token = t0 + (first - halo) + dil * col
        off_seq = jnp.where(key_token >= 0, jnp.where(key_token < seq, 0.0, NEG), NEG)
        for s in range(2):
            qs = q_ref[s, rows, :].astype(BF16)
            ksl = kw_ref[s, win, :].astype(BF16)
            vsl = vw_ref[s, win, :].astype(BF16)
            res = []
            for part in range(2):
                keep = lo_mask if part == 0 else jnp.logical_not(lo_mask)
                sc = _scores(jnp.where(keep, qs, jnp.zeros_like(qs)), ksl) - bias[2 * s + part] + off_seq
                m = jnp.max(sc, axis=-1, keepdims=True)
                p = jnp.exp2(sc - m)
                l = jnp.sum(p, axis=-1, keepdims=True)
                o = jnp.dot(p.astype(BF16), vsl, preferred_element_type=F32) / l
                res.append((o, m + jnp.log2(l)))
            o_ref[s, rows, :] = jnp.where(lo_mask, res[0][0], res[1][0])
            lse_ref[s, rows, :] = jnp.where(lo_mask, res[0][1], res[1][1])


def _dilated_group(q, k, v, group):
    batch, _, seq, _ = q.shape
    dil = DIL_GROUPS[group][1]
    halo = DIL_GROUPS[group][0] // 2
    tb = _DIL_BLOCK
    assert seq % tb == 0 and seq // dil >= 2 * _DIL_Q and tb % (_DIL_Q * dil) == 0
    per = tb // halo
    cur = pl.BlockSpec((None, 2, tb, LANES), lambda b, i: (b, 0, i, 0))
    prev = pl.BlockSpec((None, 2, halo, LANES), lambda b, i: (b, 0, jnp.maximum(i * per - 1, 0), 0))
    nxt = pl.BlockSpec((None, 2, halo, LANES),
                       lambda b, i: (b, 0, jnp.minimum((i + 1) * per, seq // halo - 1), 0))
    out_shape = jax.ShapeDtypeStruct((batch, 2, seq, LANES), F32)
    o, lse = pl.pallas_call(
        functools.partial(_dil_kernel, group, dil, seq),
        grid=(batch, seq // tb),
        in_specs=[cur, prev, cur, nxt, prev, cur, nxt],
        out_specs=[cur, cur],
        out_shape=[out_shape, out_shape],
        scratch_shapes=[pltpu.VMEM((2, tb + 2 * halo, LANES), F32)] * 2,
        compiler_params=_cparams(("parallel", "parallel")),
        name=f"dilated_g{group}",
    )(q, k, k, k, v, v, v)
    return o, lse


def _mix_kernel(h_ref, ya_ref, yc_ref, yd_ref, o0_ref, o1_ref, o2_ref, l0_ref, l1_ref, l2_ref,
                wg_ref, bg_ref, wb_ref, wo_ref, lng_ref, lnb_ref, out_ref):
    h = h_ref[...]
    hb = h.astype(BF16)
    slabs = []
    for s in range(2):
        l0, l1, l2 = l0_ref[s], l1_ref[s], l2_ref[s]
        mx = jnp.maximum(jnp.maximum(l0, l1), l2)
        e0, e1, e2 = jnp.exp2(l0 - mx), jnp.exp2(l1 - mx), jnp.exp2(l2 - mx)
        slabs.append(((e0 * o0_ref[s] + e1 * o1_ref[s] + e2 * o2_ref[s]) / (e0 + e1 + e2)).astype(BF16))
    ys = (ya_ref[...], jnp.concatenate(slabs, axis=1), yc_ref[...], yd_ref[...])
    acc = None
    for n in range(N_BRANCH):
        logit = jnp.dot(hb, wg_ref[:, n * D_MODEL:(n + 1) * D_MODEL],
                        preferred_element_type=F32) + bg_ref[n:n + 1, :]
        gate = 1.0 / (1.0 + jnp.exp(-logit))
        term = gate * jnp.dot(ys[n], wb_ref[n], preferred_element_type=F32)
        acc = term if acc is None else acc + term
    m = jnp.dot(acc.astype(BF16), wo_ref[...], preferred_element_type=F32)
    out_ref[...] = _layernorm(DEEPNORM_ALPHA * h + m, lng_ref[...], lnb_ref[...])


def _mix(h, ya, yc, yd, dil_o, dil_lse, wg, bg, wb, wo, lng, lnb, tm):
    T = h.shape[0]
    nst = dil_o[0].shape[2] // tm
    slab = pl.BlockSpec((None, 2, tm, LANES), lambda i: (i // nst, 0, i % nst, 0))
    row = lambda i: (i, 0)
    const = lambda i: (0, 0)
    wide = pl.BlockSpec((tm, D_MODEL), row)
    narrow = pl.BlockSpec((tm, BRANCH_W), row)
    return pl.pallas_call(
        _mix_kernel,
        grid=(T // tm,),
        in_specs=[wide] + [narrow] * 3 + [slab] * 6 + [
            _resident((D_MODEL, N_BRANCH * D_MODEL), const),
            pl.BlockSpec((N_BRANCH, D_MODEL), const),
            _resident((N_BRANCH, BRANCH_W, D_MODEL), lambda i: (0, 0, 0)),
            _resident((D_MODEL, D_MODEL), const),
            pl.BlockSpec((1, D_MODEL), const),
            pl.BlockSpec((1, D_MODEL), const),
        ],
        out_specs=wide,
        out_shape=jax.ShapeDtypeStruct((T, D_MODEL), F32),
        compiler_params=_cparams(("parallel",)),
        name="gate_mix",
    )(h, ya, yc, yd, *dil_o, *dil_lse, wg, bg, wb, wo, lng, lnb)


def _swiglu_step(x_ref, w1_ref, w3_ref, w2_ref, acc_ref):
    @pl.when(pl.program_id(1) == 0)
    def _():
        acc_ref[...] = jnp.zeros_like(acc_ref)

    xb = x_ref[...].astype(BF16)
    a = jnp.dot(xb, w1_ref[...].astype(BF16), preferred_element_type=F32)
    b = jnp.dot(xb, w3_ref[...].astype(BF16), preferred_element_type=F32)
    g = (a / (1.0 + jnp.exp(-a)) * b).astype(BF16)
    acc_ref[...] += jnp.dot(g, w2_ref[...].astype(BF16), preferred_element_type=F32)


def _ffn_dense_kernel(x_ref, w1_ref, w3_ref, w2_ref, lng_ref, lnb_ref, o_ref, acc_ref):
    _swiglu_step(x_ref, w1_ref, w3_ref, w2_ref, acc_ref)

    @pl.when(pl.program_id(1) == pl.num_programs(1) - 1)
    def _():
        o_ref[...] = _layernorm(DEEPNORM_ALPHA * x_ref[...] + acc_ref[...], lng_ref[...], lnb_ref[...])


def _ffn_dense(x, w13, w2, lng, lnb, tm, tf):
    T = x.shape[0]
    nf = D_FF // tf
    return pl.pallas_call(
        _ffn_dense_kernel,
        grid=(T // tm, nf),
        in_specs=[
            pl.BlockSpec((tm, D_MODEL), lambda i, j: (i, 0)),
            pl.BlockSpec((D_MODEL, tf), lambda i, j: (0, j)),
            pl.BlockSpec((D_MODEL, tf), lambda i, j: (0, nf + j)),
            pl.BlockSpec((tf, D_MODEL), lambda i, j: (j, 0)),
            pl.BlockSpec((1, D_MODEL), lambda i, j: (0, 0)),
            pl.BlockSpec((1, D_MODEL), lambda i, j: (0, 0)),
        ],
        out_specs=pl.BlockSpec((tm, D_MODEL), lambda i, j: (i, 0)),
        out_shape=jax.ShapeDtypeStruct((T, D_MODEL), F32),
        scratch_shapes=[pltpu.VMEM((tm, D_MODEL), F32)],
        compiler_params=_cparams(("parallel", "arbitrary")),
        name="ffn_dense",
    )(x, w13, w13, w2, lng, lnb)


def _ffn_expert_kernel(te_ref, used_ref, x_ref, w1_ref, w3_ref, w2_ref, o_ref, acc_ref):
    del te_ref
    live = pl.program_id(0) < used_ref[0]
    last = pl.program_id(1) == pl.num_programs(1) - 1

    @pl.when(live)
    def _():
        _swiglu_step(x_ref, w1_ref, w3_ref, w2_ref, acc_ref)

    @pl.when(live & last)
    def _():
        o_ref[...] = acc_ref[...]

    @pl.when(jnp.logical_not(live) & last)
    def _():
        o_ref[...] = jnp.zeros_like(o_ref)


def _ffn_experts(tile_expert, tiles_used, x_sorted, w13, w2, tm, tf):
    rows = x_sorted.shape[0]
    nf = D_FF // tf
    fblk = lambda i, j, nu: jnp.where(i < nu[0], j, nf - 1)
    grid_spec = pltpu.PrefetchScalarGridSpec(
        num_scalar_prefetch=2,
        grid=(rows // tm, nf),
        in_specs=[
            pl.BlockSpec((tm, D_MODEL), lambda i, j, te, nu: (i, 0)),
            pl.BlockSpec((None, D_MODEL, tf), lambda i, j, te, nu: (te[i], 0, fblk(i, j, nu))),
            pl.BlockSpec((None, D_MODEL, tf), lambda i, j, te, nu: (te[i], 0, nf + fblk(i, j, nu))),
            pl.BlockSpec((None, tf, D_MODEL), lambda i, j, te, nu: (te[i], fblk(i, j, nu), 0)),
        ],
        out_specs=pl.BlockSpec((tm, D_MODEL), lambda i, j, te, nu: (i, 0)),
        scratch_shapes=[pltpu.VMEM((tm, D_MODEL), F32)],
    )
    return pl.pallas_call(
        _ffn_expert_kernel,
        grid_spec=grid_spec,
        out_shape=jax.ShapeDtypeStruct((rows, D_MODEL), F32),
        compiler_params=_cparams(("parallel", "arbitrary")),
        name="ffn_experts",
    )(tile_expert, tiles_used, x_sorted, w13, w13, w2)


def _router_kernel(x_ref, wr_ref, idx_ref, wt_ref):
    logits = jnp.dot(x_ref[...], wr_ref[...], preferred_element_type=F32,
                     precision=lax.Precision.HIGHEST)
    lane = _lane(logits.shape)
    logits = jnp.where(lane < N_EXPERTS, logits, -jnp.inf)
    v1 = jnp.max(logits, axis=-1, keepdims=True)
    i1 = jnp.min(jnp.where(logits == v1, lane, LANES), axis=-1, keepdims=True)
    rest = jnp.where(lane == i1, -jnp.inf, logits)
    v2 = jnp.max(rest, axis=-1, keepdims=True)
    i2 = jnp.min(jnp.where(rest == v2, lane, LANES), axis=-1, keepdims=True)
    e2 = jnp.exp(v2 - v1)
    w1 = 1.0 / (1.0 + e2)
    idx_ref[...] = jnp.where(lane == 0, i1, jnp.where(lane == 1, i2, 0))
    wt_ref[...] = jnp.where(lane == 0, w1, jnp.where(lane == 1, e2 * w1, 0.0))


def _router(x, w_router, tm):
    T = x.shape[0]
    wr = jnp.pad(w_router, ((0, 0), (0, LANES - N_EXPERTS)))
    row = lambda i: (i, 0)
    return pl.pallas_call(
        _router_kernel,
        grid=(T // tm,),
        in_specs=[pl.BlockSpec((tm, D_MODEL), row), pl.BlockSpec((D_MODEL, LANES), lambda i: (0, 0))],
        out_specs=[pl.BlockSpec((tm, LANES), row), pl.BlockSpec((tm, LANES), row)],
        out_shape=[jax.ShapeDtypeStruct((T, LANES), jnp.int32), jax.ShapeDtypeStruct((T, LANES), F32)],
        compiler_params=_cparams(("parallel",)),
        name="router",
    )(x, wr)


def _moe_out_kernel(h_ref, y0_ref, y1_ref, wt_ref, lng_ref, lnb_ref, o_ref):
    wt = wt_ref[...]
    f = wt[:, 0:1] * y0_ref[...] + wt[:, 1:2] * y1_ref[...]
    o_ref[...] = _layernorm(DEEPNORM_ALPHA * h_ref[...] + f, lng_ref[...], lnb_ref[...])


def _moe_out(h, y0, y1, wt, lng, lnb, tm):
    T = h.shape[0]
    row = lambda i: (i, 0)
    const = lambda i: (0, 0)
    wide = pl.BlockSpec((tm, D_MODEL), row)
    return pl.pallas_call(
        _moe_out_kernel,
        grid=(T // tm,),
        in_specs=[wide, wide, wide, pl.BlockSpec((tm, LANES), row),
                  pl.BlockSpec((1, D_MODEL), const), pl.BlockSpec((1, D_MODEL), const)],
        out_specs=wide,
        out_shape=jax.ShapeDtypeStruct((T, D_MODEL), F32),
        compiler_params=_cparams(("parallel",)),
        name="moe_out",
    )(h, y0, y1, wt, lng, lnb)


def _moe(h, w_router, w13, w2, lng, lnb, tm_router, tm_e, tf):
    T = h.shape[0]
    idx, wt = _router(h, w_router, tm_router)
    e_flat = idx[:, :TOP_K].reshape(-1)
    onehot = (e_flat[:, None] == jnp.arange(N_EXPERTS, dtype=jnp.int32)[None, :]).astype(jnp.int32)
    counts = jnp.sum(onehot, axis=0)
    rank = jnp.sum((jnp.cumsum(onehot, axis=0) - onehot) * onehot, axis=1)
    padded = (counts + tm_e - 1) // tm_e * tm_e
    ends = jnp.cumsum(padded)
    pos = (ends - padded)[e_flat] + rank
    rows = TOP_K * T + N_EXPERTS * tm_e
    token = jnp.arange(TOP_K * T, dtype=jnp.int32) // TOP_K
    src = jnp.zeros((rows,), jnp.int32).at[pos].set(token)
    tile_start = jnp.arange(rows // tm_e, dtype=jnp.int32) * tm_e
    tile_expert = jnp.minimum(jnp.sum((tile_start[:, None] >= ends[None, :]).astype(jnp.int32), axis=1),
                              N_EXPERTS - 1)
    x_sorted = jnp.take(h.astype(BF16), src, axis=0)
    tiles_used = (ends[N_EXPERTS - 1:] // tm_e).astype(jnp.int32)
    y = _ffn_experts(tile_expert, tiles_used, x_sorted, w13, w2, tm_e, tf)
    pos2 = pos.reshape(T, TOP_K)
    y0 = jnp.take(y, pos2[:, 0], axis=0)
    y1 = jnp.take(y, pos2[:, 1], axis=0)
    return _moe_out(h, y0, y1, wt, lng, lnb, tm_router)


def _rot_cols(w, block):
    d, n = w.shape
    w4 = w.reshape(d, n // block, 2, block // 2)
    return jnp.concatenate([-w4[:, :, 1], w4[:, :, 0]], axis=2).reshape(d, n)


def _attention_weights(w_in):
    o = np.cumsum([0, MLA_Q_LORA, MLA_KV_LORA, MLA_ROPE, 2304, 256, 256, 512, 256, N_BRANCH * D_MODEL])
    cq, ckv, kr, dil, gq, gkv, dqk, dv, gate = (w_in[:, o[i]:o[i + 1]] for i in range(9))
    z = lambda n: jnp.zeros((D_MODEL, n), w_in.dtype)
    mla = [cq, z(64), ckv, z(64), kr, z(32), z(64), _rot_cols(kr, MLA_ROPE), z(32)]
    gqa = [gq, _rot_cols(gq, 32), gkv[:, :128], _rot_cols(gkv[:, :128], 32), gkv[:, 128:]]
    w1 = jnp.concatenate(mla + gqa + [dqk, dv, dil], axis=1)
    return w1.astype(BF16), gate.astype(BF16)


def _mla_up_weights(w_uq, w_ukv):
    zq = lambda n: jnp.zeros((MLA_Q_LORA, n), w_uq.dtype)
    per = MLA_NOPE + MLA_ROPE
    plain, rot = [], []
    for h in range(MLA_HEADS):
        nope = w_uq[:, h * per:h * per + MLA_NOPE]
        rope = w_uq[:, h * per + MLA_NOPE:(h + 1) * per]
        plain += [nope, rope, zq(32)]
        rot += [zq(64), _rot_cols(rope, MLA_ROPE), zq(32)]
    wuq = jnp.concatenate(plain + rot, axis=1)
    wuq = jnp.pad(wuq, ((0, 256 - MLA_Q_LORA), (0, 0)))
    zk = jnp.zeros((MLA_KV_LORA, 64), w_ukv.dtype)
    per = MLA_NOPE + MLA_V
    ks, vs = [], []
    for h in range(MLA_HEADS):
        ks += [w_ukv[:, h * per:h * per + MLA_NOPE], zk]
        vs += [w_ukv[:, h * per + MLA_NOPE:(h + 1) * per], zk]
    wukv = jnp.concatenate(ks + vs, axis=1)
    return wuq.astype(BF16), wukv.astype(BF16)


def _rope_angles(pos, dim):
    inv = ROPE_THETA ** (-(jnp.arange(0, dim, 2, dtype=F32) / dim))
    return pos[:, None] * inv[None, :]


def _position_tables(seq):
    rows = seq // GRID_W
    row_idx = jnp.repeat(jnp.arange(rows, dtype=F32), GRID_W)
    col_idx = jnp.tile(jnp.arange(GRID_W, dtype=F32), rows)
    a1 = _rope_angles(jnp.arange(seq, dtype=F32), MLA_ROPE)
    ar = _rope_angles(row_idx, HEAD_DIM // 2)
    ac = _rope_angles(col_idx, HEAD_DIM // 2)
    cos1 = jnp.concatenate([jnp.cos(a1)] * 2, axis=1)
    sin1 = jnp.concatenate([jnp.sin(a1)] * 2, axis=1)
    cos_ax = jnp.concatenate([jnp.cos(ar)] * 2 + [jnp.cos(ac)] * 2, axis=1)
    sin_ax = jnp.concatenate([jnp.sin(ar)] * 2 + [jnp.sin(ac)] * 2, axis=1)
    return cos1, sin1, cos_ax, sin_ax


def _layer_tables(pos_tabs, gqa_q_norm, gqa_k_norm):
    cos1, sin1, cos_ax, sin_ax = pos_tabs
    seq = cos1.shape[0]
    z = lambda n: jnp.zeros((seq, n), F32)
    sm = (MLA_NOPE + MLA_ROPE) ** -0.5 * LOG2E
    mcq = sm * jnp.concatenate([jnp.ones((seq, MLA_NOPE), F32), cos1, z(32)], axis=1)
    msq = sm * jnp.concatenate([z(MLA_NOPE), sin1, z(32)], axis=1)
    mck = jnp.concatenate([z(MLA_NOPE), cos1, z(32)], axis=1)
    msk = jnp.concatenate([z(MLA_NOPE), sin1, z(32)], axis=1)

    def swap(g):
        return jnp.concatenate([g[16:32], g[0:16], g[48:64], g[32:48]])

    sg = HEAD_DIM ** -0.5 * LOG2E
    two = lambda t: jnp.concatenate([t, t], axis=1)
    gaq = sg * two(cos_ax * gqa_q_norm[None, :])
    gbq = sg * two(sin_ax * swap(gqa_q_norm)[None, :])
    gak = two(cos_ax * gqa_k_norm[None, :])
    gbk = two(sin_ax * swap(gqa_k_norm)[None, :])
    return [mcq, msq, mck, msk, gaq, gbq, gak, gbk]


def _pick(n, pref):
    t = min(n, pref)
    assert n % t == 0, (n, pref)
    return t


def kernel(x, ln_emb_g, ln_emb_b, w_in, b_gate, mla_q_norm, mla_kv_norm, mla_w_uq, mla_w_ukv,
           gqa_q_norm, gqa_k_norm, diff_lambda, diff_subln, w_branch, w_out, ln1_g, ln1_b,
           ffn_w13, ffn_w2, moe_router, moe_w13, moe_w2, ln2_g, ln2_b):
    batch, seq, _ = x.shape
    T = batch * seq
    tm = _pick(seq, 512)
    tq = _pick(seq, 512)
    tk = _pick(seq, 1024)
    row = lambda v: v.reshape(1, -1)
    pos_tabs = _position_tables(seq)

    h = x.reshape(T, D_MODEL)
    for l in range(DEPTH):
        lam_init = 0.8 - 0.6 * float(np.exp(-0.3 * l))
        w1, wg = _attention_weights(w_in[l])
        wuq, wukv = _mla_up_weights(mla_w_uq[l], mla_w_ukv[l])
        tables = _layer_tables(pos_tabs, gqa_q_norm[l], gqa_k_norm[l])
        nq = jnp.pad(mla_q_norm[l], (0, 256 - MLA_Q_LORA)).reshape(1, 256)
        res = _project(h, batch, seq, l == 0, row(ln_emb_g), row(ln_emb_b), w1, wuq, wukv,
                       nq, row(mla_kv_norm[l]), tables, tm)
        if l == 0:
            h, res = res[0], res[1:]
        mq, mk, mv, gq, gk, gv, dq, dk, dv = res[:9]
        dil = res[9:]
        ya = _flash(mq, mk, mv, tq, tk).reshape(T, BRANCH_W)
        yc = _flash(gq, gk, gv, tq, tk).reshape(T, BRANCH_W)
        sub = jnp.pad(diff_subln[l], (0, LANES - DIFF_V)).reshape(1, LANES)
        yd = _diff_attention(dq, dk, dv, diff_lambda[l], sub, lam_init, tq).reshape(T, BRANCH_W)
        dil_o, dil_lse = [], []
        for g in range(len(DIL_GROUPS)):
            o, lse = _dilated_group(dil[g], dil[3 + g], dil[6 + g], g)
            dil_o.append(o)
            dil_lse.append(lse)
        h = _mix(h, ya, yc, yd, dil_o, dil_lse, wg, b_gate[l], w_branch[l].astype(BF16),
                 w_out[l].astype(BF16), row(ln1_g[l]), row(ln1_b[l]), tm)
        if l % 2 == 0:
            h = _ffn_dense(h, ffn_w13[l // 2], ffn_w2[l // 2],
                           row(ln2_g[l]), row(ln2_b[l]), _pick(T, 1024), 896)
        else:
            h = _moe(h, moe_router[l // 2], moe_w13[l // 2], moe_w2[l // 2],
                     row(ln2_g[l]), row(ln2_b[l]), tm, _pick(T, 1024), 896)
    return h.reshape(batch, seq, D_MODEL)
```

```python
import functools

import jax
import jax.numpy as jnp
import numpy as np
from jax import lax
from jax.experimental import pallas as pl
from jax.experimental.pallas import tpu as pltpu

D_MODEL = 1024
GRID_W = 64
HEAD_DIM = 64
N_BRANCH = 4
BRANCH_W = 256
MLA_HEADS = 4
MLA_NOPE = 64
MLA_ROPE = 32
MLA_V = 64
MLA_Q_LORA = 192
MLA_KV_LORA = 128
DIL_GROUPS = ((128, 1), (512, 4), (2048, 16))
DIL_HEADS = 4
GQA_Q_HEADS = 4
GQA_KV_HEADS = 2
DIFF_HEADS = 4
DIFF_D = 32
DIFF_V = 64
D_FF = 3584
N_EXPERTS = 8
TOP_K = 2
ROPE_THETA = 10000.0
LN_EPS = 1e-5
RMS_EPS = 1e-6
DIFF_NORM_EPS = 1e-5
DEPTH = 2
DEEPNORM_ALPHA = (2 * DEPTH) ** 0.25

LANES = 128
HALF = LANES // 2
VMEM_LIMIT = 56 * 1024 * 1024

LOG2E = 1.4426950408889634
NEG = -1e30
BF16 = jnp.bfloat16
F32 = jnp.float32

_MLA_COLS = 640
_GQA_COLS = 896
_DIFF_COLS = 768
_DIL_COLS = 2304
_SMALL_COLS = _MLA_COLS + _GQA_COLS


def _cparams(sem):
    return pltpu.CompilerParams(dimension_semantics=sem, vmem_limit_bytes=VMEM_LIMIT)


def _resident(shape, index_map):
    return pl.BlockSpec(shape, index_map, pipeline_mode=pl.Buffered(1))


def _layernorm(x, g, b):
    mu = jnp.mean(x, axis=-1, keepdims=True)
    xc = x - mu
    var = jnp.mean(xc * xc, axis=-1, keepdims=True)
    return xc * lax.rsqrt(var + LN_EPS) * g + b


def _lane(shape):
    return lax.broadcasted_iota(jnp.int32, shape, len(shape) - 1)


def _split_pair(x):
    lo_mask = _lane(x.shape) < HALF
    lo = jnp.where(lo_mask, x, 0.0)
    hi = jnp.where(lo_mask, pltpu.roll(x, HALF, axis=1), 0.0)
    return lo, hi


def _join_pair(lo, hi):
    return jnp.where(_lane(lo.shape) < HALF, lo, pltpu.roll(hi, HALF, axis=1))


def _with_ones(v):
    return jnp.where(_lane(v.shape) == HALF, 1.0, v)


def _diff_slope(h):
    return 2.0 ** (-8.0 * (h + 1) / DIFF_HEADS) * LOG2E


def _split3(val):
    f1 = val.astype(BF16).astype(F32)
    r1 = val - f1
    f2 = r1.astype(BF16).astype(F32)
    f3 = (r1 - f2).astype(BF16).astype(F32)
    return f1, f2, f3


_FEAT = 2 * DIFF_D


def _key_bias_features(val):
    f1, f2, f3 = _split3(val)
    lane = _lane(val.shape)
    ones = jnp.where((lane >= _FEAT + 3) & (lane < _FEAT + 6), 1.0, 0.0)
    return jnp.where(lane == _FEAT, f1, jnp.where(lane == _FEAT + 1, f2, jnp.where(lane == _FEAT + 2, f3, ones)))


def _proj_kernel(apply_ln, tiles_per_seq, *refs):
    (x_ref, lng_ref, lnb_ref, ws_ref, wdiff_ref, wdil_ref, wuq_ref, wukv_ref, nq_ref, nkv_ref,
     gains_ref, c1_ref, s1_ref, ca_ref, sa_ref) = refs[:15]
    outs = refs[15:]
    if apply_ln:
        h_ref, outs = outs[0], outs[1:]
    (mq_ref, mk_ref, mv_ref, gq_ref, gk_ref, gv_ref, dq_ref, dk_ref, dv_ref) = outs[:9]
    dil_refs = outs[9:]

    x = x_ref[...]
    if apply_ln:
        x = _layernorm(x, lng_ref[...], lnb_ref[...])
        h_ref[...] = x
    hb = x.astype(BF16)

    def proj(w_ref, off, n):
        return jnp.dot(hb, w_ref[:, off:off + n], preferred_element_type=F32)

    p = proj(ws_ref, 0, _MLA_COLS)
    cq, ckv, kr, krr = p[:, 0:256], p[:, 256:384], p[:, 384:512], p[:, 512:640]
    rq = lax.rsqrt(jnp.sum(cq * cq, axis=-1, keepdims=True) * (1.0 / MLA_Q_LORA) + RMS_EPS)
    cqn = (cq * rq * nq_ref[...]).astype(BF16)
    q2 = jnp.dot(cqn, wuq_ref[...], preferred_element_type=F32)
    rkv = lax.rsqrt(jnp.sum(ckv * ckv, axis=-1, keepdims=True) * (1.0 / MLA_KV_LORA) + RMS_EPS)
    ckvn = (ckv * rkv * nkv_ref[...]).astype(BF16)
    kv2 = jnp.dot(ckvn, wukv_ref[...], preferred_element_type=F32)
    c1, s1 = c1_ref[...], s1_ref[...]
    k_rope = kr * c1 + krr * s1
    mla_scale = (MLA_NOPE + MLA_ROPE) ** -0.5 * LOG2E
    cq_t, sq_t = c1 * mla_scale, s1 * mla_scale
    for h in range(MLA_HEADS):
        qh = q2[:, h * LANES:(h + 1) * LANES]
        qrh = q2[:, (MLA_HEADS + h) * LANES:(MLA_HEADS + h + 1) * LANES]
        mq_ref[h] = (qh * cq_t + qrh * sq_t).astype(BF16)
        mk_ref[h] = (kv2[:, h * LANES:(h + 1) * LANES] + k_rope).astype(BF16)
        vh = kv2[:, (MLA_HEADS + h) * LANES:(MLA_HEADS + h + 1) * LANES]
        mv_ref[h] = _with_ones(vh).astype(BF16)

    p = proj(ws_ref, _MLA_COLS, _GQA_COLS)

    def head_rms(xs):
        ss = xs * xs
        lo_mask = _lane(xs.shape) < HALF
        s_all = jnp.sum(ss, axis=-1, keepdims=True)
        s_lo = jnp.sum(jnp.where(lo_mask, ss, 0.0), axis=-1, keepdims=True)
        return lax.rsqrt(jnp.where(lo_mask, s_lo, s_all - s_lo) * (1.0 / HEAD_DIM) + RMS_EPS)

    ca, sa = ca_ref[...], sa_ref[...]
    aq, bq = ca * gains_ref[0:1, :], sa * gains_ref[1:2, :]
    ak, bk = ca * gains_ref[2:3, :], sa * gains_ref[3:4, :]
    for s in range(2):
        qs = p[:, s * LANES:(s + 1) * LANES]
        qrs = p[:, 256 + s * LANES:256 + (s + 1) * LANES]
        qn = head_rms(qs) * (qs * aq + qrs * bq)
        lo, hi = _split_pair(qn)
        gq_ref[2 * s] = lo.astype(BF16)
        gq_ref[2 * s + 1] = hi.astype(BF16)
    ks, krs = p[:, 512:640], p[:, 640:768]
    kn = head_rms(ks) * (ks * ak + krs * bk)
    lo, hi = _split_pair(kn)
    gk_ref[0] = lo.astype(BF16)
    gk_ref[1] = hi.astype(BF16)
    lo, hi = _split_pair(p[:, 768:896])
    gv_ref[0] = _with_ones(lo).astype(BF16)
    gv_ref[1] = _with_ones(hi).astype(BF16)

    p = proj(wdiff_ref, 0, _DIFF_COLS)
    dscale = DIFF_D ** -0.5 * LOG2E
    tm = x.shape[0]
    pos = (pl.program_id(0) % tiles_per_seq * tm
           + lax.broadcasted_iota(jnp.int32, (tm, LANES), 0)).astype(F32)
    for s in range(2):
        lo, hi = _split_pair(p[:, s * LANES:(s + 1) * LANES] * dscale)
        dq_ref[2 * s] = lo.astype(BF16)
        dq_ref[2 * s + 1] = hi.astype(BF16)
        lo, hi = _split_pair(p[:, 256 + s * LANES:256 + (s + 1) * LANES])
        dk_ref[2 * s] = (lo + _key_bias_features(_diff_slope(2 * s) * pos)).astype(BF16)
        dk_ref[2 * s + 1] = (hi + _key_bias_features(_diff_slope(2 * s + 1) * pos)).astype(BF16)
        lo, hi = _split_pair(p[:, 512 + s * LANES:512 + (s + 1) * LANES])
        dv_ref[2 * s] = _with_ones(lo).astype(BF16)
        dv_ref[2 * s + 1] = _with_ones(hi).astype(BF16)

    lscale = HEAD_DIM ** -0.5 * LOG2E
    for i in range(9):
        seg = proj(wdil_ref, i * BRANCH_W, BRANCH_W)
        if i < 3:
            seg = seg * lscale
        dil_refs[i][0] = seg[:, :LANES]
        dil_refs[i][1] = seg[:, LANES:]


def _project(x2d, batch, seq, apply_ln, lng, lnb, weights, wuq, wukv, nq, nkv, gains, tables, tm):
    T = x2d.shape[0]
    nst = seq // tm
    grid = (T // tm,)
    row = lambda i: (i, 0)
    const = lambda i: (0, 0)
    tab = lambda i: (i % nst, 0)
    head = lambda i: (i // nst, 0, i % nst, 0)

    in_specs = [
        pl.BlockSpec((tm, D_MODEL), row),
        pl.BlockSpec((1, D_MODEL), const),
        pl.BlockSpec((1, D_MODEL), const),
        _resident((D_MODEL, _SMALL_COLS), const),
        _resident((D_MODEL, _DIFF_COLS), const),
        _resident((D_MODEL, _DIL_COLS), const),
        _resident((256, 8 * LANES), const),
        _resident((LANES, 8 * LANES), const),
        pl.BlockSpec((1, 256), const),
        pl.BlockSpec((1, LANES), const),
        pl.BlockSpec((4, LANES), const),
    ] + [pl.BlockSpec((tm, LANES), tab)] * 4

    def heads_out(n):
        return (jax.ShapeDtypeStruct((batch, n, seq, LANES), BF16),
                pl.BlockSpec((None, n, tm, LANES), head))

    outs = []
    if apply_ln:
        outs.append((jax.ShapeDtypeStruct((T, D_MODEL), F32), pl.BlockSpec((tm, D_MODEL), row)))
    outs += [heads_out(MLA_HEADS)] * 3
    outs += [heads_out(GQA_Q_HEADS), heads_out(GQA_KV_HEADS), heads_out(GQA_KV_HEADS)]
    outs += [heads_out(DIFF_HEADS)] * 3
    outs += [(jax.ShapeDtypeStruct((batch, 2, seq, LANES), F32), pl.BlockSpec((None, 2, tm, LANES), head))] * 9

    res = pl.pallas_call(
        functools.partial(_proj_kernel, apply_ln, nst),
        grid=grid,
        in_specs=in_specs,
        out_specs=[o[1] for o in outs],
        out_shape=[o[0] for o in outs],
        compiler_params=_cparams(("parallel",)),
        name="proj_prep",
    )(x2d, lng, lnb, *weights, wuq, wukv, nq, nkv, gains, *tables)
    return res


def _normalize(acc):
    return acc / acc[:, HALF:HALF + 1]


def _scores(q, k):
    return lax.dot_general(q, k, (((1,), (1,)), ((), ())), preferred_element_type=F32)


def _softmax_update(s, v, acc_ref, m_ref, idx):
    blocks = [s[:, j * LANES:(j + 1) * LANES] for j in range(s.shape[1] // LANES)]
    part = functools.reduce(jnp.maximum, blocks)
    m_prev = m_ref[idx]
    m_new = jnp.maximum(m_prev, jnp.max(part, axis=-1, keepdims=True))
    p = jnp.concatenate([jnp.exp2((b - m_new).astype(BF16)) for b in blocks], axis=1)
    acc_ref[idx] = (jnp.exp2(m_prev - m_new) * acc_ref[idx]
                    + jnp.dot(p, v, preferred_element_type=F32))
    m_ref[idx] = m_new


def _flash_kernel(n_rep, tk, q_ref, k_ref, v_ref, o_ref, acc_ref, m_ref):
    n_heads = q_ref.shape[0]
    seq = k_ref.shape[1]
    acc_ref[...] = jnp.zeros_like(acc_ref)
    m_ref[...] = jnp.full_like(m_ref, NEG)

    def body(c, carry):
        ks = pl.multiple_of(c * tk, tk)
        scores = [_scores(q_ref[h], k_ref[h // n_rep, pl.ds(ks, tk), :]) for h in range(n_heads)]
        for h in range(n_heads):
            _softmax_update(scores[h], v_ref[h // n_rep, pl.ds(ks, tk), :], acc_ref, m_ref, h)
        return carry

    lax.fori_loop(0, seq // tk, body, 0)
    outs = [_normalize(acc_ref[h]) for h in range(n_heads)]
    for s in range(n_heads // 2):
        o_ref[:, s * LANES:(s + 1) * LANES] = _join_pair(outs[2 * s], outs[2 * s + 1]).astype(o_ref.dtype)


def _flash(q, k, v, tq, tk):
    batch, n_heads, seq, _ = q.shape
    n_kv = k.shape[1]
    return pl.pallas_call(
        functools.partial(_flash_kernel, n_heads // n_kv, tk),
        grid=(batch, seq // tq),
        in_specs=[
            pl.BlockSpec((None, n_heads, tq, LANES), lambda b, i: (b, 0, i, 0)),
            _resident((None, n_kv, seq, LANES), lambda b, i: (b, 0, 0, 0)),
            _resident((None, n_kv, seq, LANES), lambda b, i: (b, 0, 0, 0)),
        ],
        out_specs=pl.BlockSpec((None, tq, n_heads * HALF), lambda b, i: (b, i, 0)),
        out_shape=jax.ShapeDtypeStruct((batch, seq, n_heads * HALF), BF16),
        scratch_shapes=[pltpu.VMEM((n_heads, tq, LANES), F32), pltpu.VMEM((n_heads, tq, LANES), F32)],
        compiler_params=_cparams(("parallel", "parallel")),
        name="flash_attn",
    )(q, k, v)


def _diff_kernel(lam_init, q_ref, k_ref, v_ref, lp_ref, sub_ref, o_ref, acc_ref, m_ref, qv_ref):
    n_heads, tq, _ = q_ref.shape
    seq = k_ref.shape[1]
    tile = pl.program_id(1)
    t0 = tile * tq
    lp = lp_ref[...]
    lam = (jnp.exp(jnp.sum(lp[0:1] * lp[1:2], keepdims=True))
           - jnp.exp(jnp.sum(lp[2:3] * lp[3:4], keepdims=True)) + lam_init)
    lane = _lane((tq, LANES))
    tpos = (t0 + lax.broadcasted_iota(jnp.int32, (tq, LANES), 0)).astype(F32)
    in_f = (lane >= _FEAT) & (lane < _FEAT + 3)
    for h in range(n_heads):
        q = q_ref[h].astype(F32)
        g1, g2, g3 = _split3(2.0 * _diff_slope(h) * tpos)
        left = jnp.where(in_f, 1.0, 0.0)
        right = jnp.where(in_f, -1.0, jnp.where(lane == _FEAT + 3, g1, jnp.where(
            lane == _FEAT + 4, g2, jnp.where(lane == _FEAT + 5, g3, 0.0))))
        for m in range(2):
            qm = jnp.where((lane >= m * DIFF_D) & (lane < (m + 1) * DIFF_D), q, 0.0)
            qv_ref[0, 2 * h + m] = (qm + left).astype(BF16)
            qv_ref[1, 2 * h + m] = (qm + right).astype(BF16)
            qv_ref[2, 2 * h + m] = qm.astype(BF16)
    acc_ref[...] = jnp.zeros_like(acc_ref)
    m_ref[...] = jnp.full_like(m_ref, NEG)

    def chunk(side, ks, bias):
        scores = [_scores(qv_ref[side, i], k_ref[i // 2, pl.ds(ks, tq), :]) for i in range(2 * n_heads)]
        for i in range(2 * n_heads):
            s = scores[i] if bias is None else scores[i] - bias[i // 2]
            _softmax_update(s, v_ref[i // 2, pl.ds(ks, tq), :], acc_ref, m_ref, i)

    def side_body(side):
        def body(c, carry):
            chunk(side, pl.multiple_of(c * tq, tq), None)
            return carry
        return body

    lax.fori_loop(0, tile, side_body(0), 0)
    dist = jnp.abs(lax.broadcasted_iota(jnp.int32, (tq, tq), 0)
                   - lax.broadcasted_iota(jnp.int32, (tq, tq), 1)).astype(F32)
    trow = (t0 + lax.broadcasted_iota(jnp.int32, (tq, tq), 0)).astype(F32)
    chunk(2, pl.multiple_of(t0, tq), [_diff_slope(h) * (dist - trow) for h in range(n_heads)])
    lax.fori_loop(tile + 1, seq // tq, side_body(1), 0)

    outs = []
    for h in range(n_heads):
        o = _normalize(acc_ref[2 * h]) - lam * _normalize(acc_ref[2 * h + 1])
        o = jnp.where(lane < DIFF_V, o, 0.0)
        ms = jnp.sum(o * o, axis=-1, keepdims=True) * (1.0 / DIFF_V)
        outs.append(o * lax.rsqrt(ms + DIFF_NORM_EPS) * sub_ref[...] * (1.0 - lam_init))
    for s in range(n_heads // 2):
        o_ref[:, s * LANES:(s + 1) * LANES] = _join_pair(outs[2 * s], outs[2 * s + 1]).astype(o_ref.dtype)


def _diff_attention(q, k, v, lam_params, subln, lam_init, tq):
    batch, n_heads, seq, _ = q.shape
    return pl.pallas_call(
        functools.partial(_diff_kernel, lam_init),
        grid=(batch, seq // tq),
        in_specs=[
            pl.BlockSpec((None, n_heads, tq, LANES), lambda b, i: (b, 0, i, 0)),
            _resident((None, n_heads, seq, LANES), lambda b, i: (b, 0, 0, 0)),
            _resident((None, n_heads, seq, LANES), lambda b, i: (b, 0, 0, 0)),
            pl.BlockSpec((4, DIFF_D), lambda b, i: (0, 0)),
            pl.BlockSpec((1, LANES), lambda b, i: (0, 0)),
        ],
        out_specs=pl.BlockSpec((None, tq, n_heads * HALF), lambda b, i: (b, i, 0)),
        out_shape=jax.ShapeDtypeStruct((batch, seq, n_heads * HALF), BF16),
        scratch_shapes=[pltpu.VMEM((2 * n_heads, tq, LANES), F32)] * 2
        + [pltpu.VMEM((3, 2 * n_heads, tq, LANES), BF16)],
        compiler_params=_cparams(("parallel", "parallel")),
        name="diff_attn",
    )(q, k, v, lam_params, subln)


_DIL_BLOCK = 2048
_DIL_Q = 128


def _dil_kernel(group, dil, seq, q_ref, kp_ref, kc_ref, kn_ref, vp_ref, vc_ref, vn_ref,
                o_ref, lse_ref, kw_ref, vw_ref):
    tb = q_ref.shape[1]
    halo = DIL_GROUPS[group][0] // 2
    assert halo == kp_ref.shape[1] and halo == (_DIL_Q // 2) * dil
    for w_ref, p_ref, c_ref, n_ref in ((kw_ref, kp_ref, kc_ref, kn_ref), (vw_ref, vp_ref, vc_ref, vn_ref)):
        w_ref[:, 0:halo] = p_ref[...]
        w_ref[:, halo:halo + tb] = c_ref[...]
        w_ref[:, halo + tb:halo + tb + halo] = n_ref[...]
    t0 = pl.program_id(1) * tb
    shape = (_DIL_Q, 2 * _DIL_Q)
    rel = lax.broadcasted_iota(jnp.int32, shape, 1) - lax.broadcasted_iota(jnp.int32, shape, 0) - _DIL_Q // 2
    absrel = jnp.abs(rel).astype(F32)
    band = jnp.where(absrel <= _DIL_Q // 2, 0.0, -NEG)
    n_slopes = len(DIL_GROUPS) * DIL_HEADS
    bias = [2.0 ** (-8.0 * (group * DIL_HEADS + h + 1) / n_slopes) * dil * LOG2E * absrel + band
            for h in range(DIL_HEADS)]
    col = lax.broadcasted_iota(jnp.int32, (1, 2 * _DIL_Q), 1)
    lo_mask = _lane((_DIL_Q, LANES)) < HALF
    for unit in range(tb // _DIL_Q):
        first = (unit // dil) * _DIL_Q * dil + unit % dil
        rows = pl.ds(first, _DIL_Q, stride=dil)
        win = pl.ds(first, 2 * _DIL_Q, stride=dil)
        key_token = t0 + (first - halo) + dil * col
        off_seq = jnp.where(key_token >= 0, jnp.where(key_token < seq, 0.0, NEG), NEG)
        for s in range(2):
            qs = q_ref[s, rows, :].astype(BF16)
            ksl = kw_ref[s, win, :].astype(BF16)
            vsl = vw_ref[s, win, :].astype(BF16)
            res = []
            for part in range(2):
                keep = lo_mask if part == 0 else jnp.logical_not(lo_mask)
                sc = _scores(jnp.where(keep, qs, jnp.zeros_like(qs)), ksl) - bias[2 * s + part] + off_seq
                m = jnp.max(sc, axis=-1, keepdims=True)
                p = jnp.exp2(sc - m)
                l = jnp.sum(p, axis=-1, keepdims=True)
                o = jnp.dot(p.astype(BF16), vsl, preferred_element_type=F32) / l
                res.append((o, m + jnp.log2(l)))
            o_ref[s, rows, :] = jnp.where(lo_mask, res[0][0], res[1][0])
            lse_ref[s, rows, :] = jnp.where(lo_mask, res[0][1], res[1][1])


def _dilated_group(q, k, v, group):
    batch, _, seq, _ = q.shape
    dil = DIL_GROUPS[group][1]
    halo = DIL_GROUPS[group][0] // 2
    tb = _DIL_BLOCK
    assert seq % tb == 0 and seq // dil >= 2 * _DIL_Q and tb % (_DIL_Q * dil) == 0
    per = tb // halo
    cur = pl.BlockSpec((None, 2, tb, LANES), lambda b, i: (b, 0, i, 0))
    prev = pl.BlockSpec((None, 2, halo, LANES), lambda b, i: (b, 0, jnp.maximum(i * per - 1, 0), 0))
    nxt = pl.BlockSpec((None, 2, halo, LANES),
                       lambda b, i: (b, 0, jnp.minimum((i + 1) * per, seq // halo - 1), 0))
    out_shape = jax.ShapeDtypeStruct((batch, 2, seq, LANES), F32)
    o, lse = pl.pallas_call(
        functools.partial(_dil_kernel, group, dil, seq),
        grid=(batch, seq // tb),
        in_specs=[cur, prev, cur, nxt, prev, cur, nxt],
        out_specs=[cur, cur],
        out_shape=[out_shape, out_shape],
        scratch_shapes=[pltpu.VMEM((2, tb + 2 * halo, LANES), F32)] * 2,
        compiler_params=_cparams(("parallel", "parallel")),
        name=f"dilated_g{group}",
    )(q, k, k, k, v, v, v)
    return o, lse


def _mix_kernel(h_ref, ya_ref, yc_ref, yd_ref, o0_ref, o1_ref, o2_ref, l0_ref, l1_ref, l2_ref,
                wg_ref, bg_ref, wb_ref, wo_ref, lng_ref, lnb_ref, out_ref):
    h = h_ref[...]
    hb = h.astype(BF16)
    slabs = []
    for s in range(2):
        l0, l1, l2 = l0_ref[s], l1_ref[s], l2_ref[s]
        mx = jnp.maximum(jnp.maximum(l0, l1), l2)
        e0, e1, e2 = jnp.exp2(l0 - mx), jnp.exp2(l1 - mx), jnp.exp2(l2 - mx)
        slabs.append(((e0 * o0_ref[s] + e1 * o1_ref[s] + e2 * o2_ref[s]) / (e0 + e1 + e2)).astype(BF16))
    ys = (ya_ref[...], jnp.concatenate(slabs, axis=1), yc_ref[...], yd_ref[...])
    acc = None
    for n in range(N_BRANCH):
        logit = jnp.dot(hb, wg_ref[:, n * D_MODEL:(n + 1) * D_MODEL],
                        preferred_element_type=F32) + bg_ref[n:n + 1, :]
        gate = 1.0 / (1.0 + jnp.exp(-logit))
        term = gate * jnp.dot(ys[n], wb_ref[n], preferred_element_type=F32)
        acc = term if acc is None else acc + term
    m = jnp.dot(acc.astype(BF16), wo_ref[...], preferred_element_type=F32)
    out_ref[...] = _layernorm(DEEPNORM_ALPHA * h + m, lng_ref[...], lnb_ref[...])


def _mix(h, ya, yc, yd, dil_o, dil_lse, wg, bg, wb, wo, lng, lnb, tm):
    T = h.shape[0]
    nst = dil_o[0].shape[2] // tm
    slab = pl.BlockSpec((None, 2, tm, LANES), lambda i: (i // nst, 0, i % nst, 0))
    row = lambda i: (i, 0)
    const = lambda i: (0, 0)
    wide = pl.BlockSpec((tm, D_MODEL), row)
    narrow = pl.BlockSpec((tm, BRANCH_W), row)
    return pl.pallas_call(
        _mix_kernel,
        grid=(T // tm,),
        in_specs=[wide] + [narrow] * 3 + [slab] * 6 + [
            _resident((D_MODEL, N_BRANCH * D_MODEL), const),
            pl.BlockSpec((N_BRANCH, D_MODEL), const),
            _resident((N_BRANCH, BRANCH_W, D_MODEL), lambda i: (0, 0, 0)),
            _resident((D_MODEL, D_MODEL), const),
            pl.BlockSpec((1, D_MODEL), const),
            pl.BlockSpec((1, D_MODEL), const),
        ],
        out_specs=wide,
        out_shape=jax.ShapeDtypeStruct((T, D_MODEL), F32),
        compiler_params=_cparams(("parallel",)),
        name="gate_mix",
    )(h, ya, yc, yd, *dil_o, *dil_lse, wg, bg, wb, wo, lng, lnb)


def _swiglu_step(x_ref, w1_ref, w3_ref, w2_ref, acc_ref):
    @pl.when(pl.program_id(1) == 0)
    def _():
        acc_ref[...] = jnp.zeros_like(acc_ref)

    xb = x_ref[...].astype(BF16)
    a = jnp.dot(xb, w1_ref[...].astype(BF16), preferred_element_type=F32)
    b = jnp.dot(xb, w3_ref[...].astype(BF16), preferred_element_type=F32)
    g = (a / (1.0 + jnp.exp(-a)) * b).astype(BF16)
    acc_ref[...] += jnp.dot(g, w2_ref[...].astype(BF16), preferred_element_type=F32)


def _ffn_dense_kernel(x_ref, w1_ref, w3_ref, w2_ref, lng_ref, lnb_ref, o_ref, acc_ref):
    _swiglu_step(x_ref, w1_ref, w3_ref, w2_ref, acc_ref)

    @pl.when(pl.program_id(1) == pl.num_programs(1) - 1)
    def _():
        o_ref[...] = _layernorm(DEEPNORM_ALPHA * x_ref[...] + acc_ref[...], lng_ref[...], lnb_ref[...])


def _ffn_dense(x, w13, w2, lng, lnb, tm, tf):
    T = x.shape[0]
    nf = D_FF // tf
    return pl.pallas_call(
        _ffn_dense_kernel,
        grid=(T // tm, nf),
        in_specs=[
            pl.BlockSpec((tm, D_MODEL), lambda i, j: (i, 0)),
            pl.BlockSpec((D_MODEL, tf), lambda i, j: (0, j)),
            pl.BlockSpec((D_MODEL, tf), lambda i, j: (0, nf + j)),
            pl.BlockSpec((tf, D_MODEL), lambda i, j: (j, 0)),
            pl.BlockSpec((1, D_MODEL), lambda i, j: (0, 0)),
            pl.BlockSpec((1, D_MODEL), lambda i, j: (0, 0)),
        ],
        out_specs=pl.BlockSpec((tm, D_MODEL), lambda i, j: (i, 0)),
        out_shape=jax.ShapeDtypeStruct((T, D_MODEL), F32),
        scratch_shapes=[pltpu.VMEM((tm, D_MODEL), F32)],
        compiler_params=_cparams(("parallel", "arbitrary")),
        name="ffn_dense",
    )(x, w13, w13, w2, lng, lnb)


def _ffn_expert_kernel(te_ref, used_ref, x_ref, w1_ref, w3_ref, w2_ref, o_ref, acc_ref):
    del te_ref
    live = pl.program_id(0) < used_ref[0]
    last = pl.program_id(1) == pl.num_programs(1) - 1

    @pl.when(live)
    def _():
        _swiglu_step(x_ref, w1_ref, w3_ref, w2_ref, acc_ref)

    @pl.when(live & last)
    def _():
        o_ref[...] = acc_ref[...].astype(o_ref.dtype)

    @pl.when(jnp.logical_not(live) & last)
    def _():
        o_ref[...] = jnp.zeros_like(o_ref)


def _ffn_experts(tile_expert, tiles_used, x_sorted, w13, w2, tm, tf):
    rows = x_sorted.shape[0]
    nf = D_FF // tf
    fblk = lambda i, j, nu: jnp.where(i < nu[0], j, nf - 1)
    grid_spec = pltpu.PrefetchScalarGridSpec(
        num_scalar_prefetch=2,
        grid=(rows // tm, nf),
        in_specs=[
            pl.BlockSpec((tm, D_MODEL), lambda i, j, te, nu: (i, 0)),
            pl.BlockSpec((None, D_MODEL, tf), lambda i, j, te, nu: (te[i], 0, fblk(i, j, nu))),
            pl.BlockSpec((None, D_MODEL, tf), lambda i, j, te, nu: (te[i], 0, nf + fblk(i, j, nu))),
            pl.BlockSpec((None, tf, D_MODEL), lambda i, j, te, nu: (te[i], fblk(i, j, nu), 0)),
        ],
        out_specs=pl.BlockSpec((tm, D_MODEL), lambda i, j, te, nu: (i, 0)),
        scratch_shapes=[pltpu.VMEM((tm, D_MODEL), F32)],
    )
    return pl.pallas_call(
        _ffn_expert_kernel,
        grid_spec=grid_spec,
        out_shape=jax.ShapeDtypeStruct((rows, D_MODEL), BF16),
        compiler_params=_cparams(("parallel", "arbitrary")),
        name="ffn_experts",
    )(tile_expert, tiles_used, x_sorted, w13, w13, w2)


def _router_kernel(x_ref, wr_ref, idx_ref, wt_ref):
    logits = jnp.dot(x_ref[...], wr_ref[...], preferred_element_type=F32,
                     precision=lax.Precision.HIGHEST)
    lane = _lane(logits.shape)
    logits = jnp.where(lane < N_EXPERTS, logits, -jnp.inf)
    v1 = jnp.max(logits, axis=-1, keepdims=True)
    i1 = jnp.min(jnp.where(logits == v1, lane, LANES), axis=-1, keepdims=True)
    rest = jnp.where(lane == i1, -jnp.inf, logits)
    v2 = jnp.max(rest, axis=-1, keepdims=True)
    i2 = jnp.min(jnp.where(rest == v2, lane, LANES), axis=-1, keepdims=True)
    e2 = jnp.exp(v2 - v1)
    w1 = 1.0 / (1.0 + e2)
    idx_ref[...] = jnp.where(lane == 0, i1, jnp.where(lane == 1, i2, 0))
    wt_ref[...] = jnp.where(lane == 0, w1, jnp.where(lane == 1, e2 * w1, 0.0))


def _router(x, w_router, tm):
    T = x.shape[0]
    wr = jnp.pad(w_router, ((0, 0), (0, LANES - N_EXPERTS)))
    row = lambda i: (i, 0)
    return pl.pallas_call(
        _router_kernel,
        grid=(T // tm,),
        in_specs=[pl.BlockSpec((tm, D_MODEL), row), pl.BlockSpec((D_MODEL, LANES), lambda i: (0, 0))],
        out_specs=[pl.BlockSpec((tm, LANES), row), pl.BlockSpec((tm, LANES), row)],
        out_shape=[jax.ShapeDtypeStruct((T, LANES), jnp.int32), jax.ShapeDtypeStruct((T, LANES), F32)],
        compiler_params=_cparams(("parallel",)),
        name="router",
    )(x, wr)


def _moe_out_kernel(h_ref, y0_ref, y1_ref, wt_ref, lng_ref, lnb_ref, o_ref):
    wt = wt_ref[...]
    f = wt[:, 0:1] * y0_ref[...] + wt[:, 1:2] * y1_ref[...]
    o_ref[...] = _layernorm(DEEPNORM_ALPHA * h_ref[...] + f, lng_ref[...], lnb_ref[...])


def _moe_out(h, y0, y1, wt, lng, lnb, tm):
    T = h.shape[0]
    row = lambda i: (i, 0)
    const = lambda i: (0, 0)
    wide = pl.BlockSpec((tm, D_MODEL), row)
    return pl.pallas_call(
        _moe_out_kernel,
        grid=(T // tm,),
        in_specs=[wide, wide, wide, pl.BlockSpec((tm, LANES), row),
                  pl.BlockSpec((1, D_MODEL), const), pl.BlockSpec((1, D_MODEL), const)],
        out_specs=wide,
        out_shape=jax.ShapeDtypeStruct((T, D_MODEL), F32),
        compiler_params=_cparams(("parallel",)),
        name="moe_out",
    )(h, y0, y1, wt, lng, lnb)


def _moe(h, w_router, w13, w2, lng, lnb, tm_router, tm_e, tf):
    T = h.shape[0]
    idx, wt = _router(h, w_router, tm_router)
    e_flat = idx[:, :TOP_K].reshape(-1)
    onehot = (e_flat[None, :] == jnp.arange(N_EXPERTS, dtype=jnp.int32)[:, None]).astype(jnp.int32)
    counts = jnp.sum(onehot, axis=1)
    rank = jnp.sum((jnp.cumsum(onehot, axis=1) - onehot) * onehot, axis=0)
    padded = (counts + tm_e - 1) // tm_e * tm_e
    ends = jnp.cumsum(padded)
    pos = jnp.sum(onehot * (ends - padded)[:, None], axis=0) + rank
    rows = TOP_K * T + N_EXPERTS * tm_e
    token = jnp.arange(TOP_K * T, dtype=jnp.int32) // TOP_K
    src = jnp.zeros((rows,), jnp.int32).at[pos].set(token)
    tile_start = jnp.arange(rows // tm_e, dtype=jnp.int32) * tm_e
    tile_expert = jnp.minimum(jnp.sum((tile_start[:, None] >= ends[None, :]).astype(jnp.int32), axis=1),
                              N_EXPERTS - 1)
    x_sorted = jnp.take(h.astype(BF16), src, axis=0)
    tiles_used = (ends[N_EXPERTS - 1:] // tm_e).astype(jnp.int32)
    y = _ffn_experts(tile_expert, tiles_used, x_sorted, w13, w2, tm_e, tf)
    pos2 = pos.reshape(T, TOP_K)
    y0 = jnp.take(y, pos2[:, 0], axis=0)
    y1 = jnp.take(y, pos2[:, 1], axis=0)
    return _moe_out(h, y0, y1, wt, lng, lnb, tm_router)


def _rot_cols(w, block):
    d, n = w.shape
    w4 = w.reshape(d, n // block, 2, block // 2)
    return jnp.concatenate([-w4[:, :, 1], w4[:, :, 0]], axis=2).reshape(d, n)


def _attention_weights(w_in):
    o = np.cumsum([0, MLA_Q_LORA, MLA_KV_LORA, MLA_ROPE, 2304, 256, 256, 512 + 256, N_BRANCH * D_MODEL])
    cq, ckv, kr, dil, gq, gkv, diff, gate = (w_in[:, o[i]:o[i + 1]] for i in range(8))
    z = lambda n: jnp.zeros((D_MODEL, n), w_in.dtype)
    mla = [cq, z(64), ckv, z(64), kr, z(32), z(64), _rot_cols(kr, MLA_ROPE), z(32)]
    gqa = [gq, _rot_cols(gq, 32), gkv[:, :128], _rot_cols(gkv[:, :128], 32), gkv[:, 128:]]
    small = jnp.concatenate(mla + gqa, axis=1)
    return [w.astype(BF16) for w in (small, diff, dil)], gate.astype(BF16)


def _mla_up_weights(w_uq, w_ukv):
    zq = lambda n: jnp.zeros((MLA_Q_LORA, n), w_uq.dtype)
    per = MLA_NOPE + MLA_ROPE
    plain, rot = [], []
    for h in range(MLA_HEADS):
        nope = w_uq[:, h * per:h * per + MLA_NOPE]
        rope = w_uq[:, h * per + MLA_NOPE:(h + 1) * per]
        plain += [nope, rope, zq(32)]
        rot += [zq(64), _rot_cols(rope, MLA_ROPE), zq(32)]
    wuq = jnp.concatenate(plain + rot, axis=1)
    wuq = jnp.pad(wuq, ((0, 256 - MLA_Q_LORA), (0, 0)))
    zk = jnp.zeros((MLA_KV_LORA, 64), w_ukv.dtype)
    per = MLA_NOPE + MLA_V
    ks, vs = [], []
    for h in range(MLA_HEADS):
        ks += [w_ukv[:, h * per:h * per + MLA_NOPE], zk]
        vs += [w_ukv[:, h * per + MLA_NOPE:(h + 1) * per], zk]
    wukv = jnp.concatenate(ks + vs, axis=1)
    return wuq.astype(BF16), wukv.astype(BF16)


def _rope_angles(pos, dim):
    inv = ROPE_THETA ** (-(jnp.arange(0, dim, 2, dtype=F32) / dim))
    return pos[:, None] * inv[None, :]


def _position_tables(seq):
    rows = seq // GRID_W
    row_idx = jnp.repeat(jnp.arange(rows, dtype=F32), GRID_W)
    col_idx = jnp.tile(jnp.arange(GRID_W, dtype=F32), rows)
    a1 = _rope_angles(jnp.arange(seq, dtype=F32), MLA_ROPE)
    ar = _rope_angles(row_idx, HEAD_DIM // 2)
    ac = _rope_angles(col_idx, HEAD_DIM // 2)
    z = lambda n: jnp.zeros((seq, n), F32)
    c1 = jnp.concatenate([jnp.ones((seq, MLA_NOPE), F32)] + [jnp.cos(a1)] * 2 + [z(32)], axis=1)
    s1 = jnp.concatenate([z(MLA_NOPE)] + [jnp.sin(a1)] * 2 + [z(32)], axis=1)
    ca = jnp.concatenate(([jnp.cos(ar)] * 2 + [jnp.cos(ac)] * 2) * 2, axis=1)
    sa = jnp.concatenate(([jnp.sin(ar)] * 2 + [jnp.sin(ac)] * 2) * 2, axis=1)
    return [c1, s1, ca, sa]


def _gqa_gain_rows(q_norm, k_norm):
    def swap(g):
        return jnp.concatenate([g[16:32], g[0:16], g[48:64], g[32:48]])

    sg = HEAD_DIM ** -0.5 * LOG2E
    rows = [sg * q_norm, sg * swap(q_norm), k_norm, swap(k_norm)]
    return jnp.stack([jnp.concatenate([r, r]) for r in rows])


def _pick(n, pref):
    t = min(n, pref)
    assert n % t == 0, (n, pref)
    return t


def kernel(x, ln_emb_g, ln_emb_b, w_in, b_gate, mla_q_norm, mla_kv_norm, mla_w_uq, mla_w_ukv,
           gqa_q_norm, gqa_k_norm, diff_lambda, diff_subln, w_branch, w_out, ln1_g, ln1_b,
           ffn_w13, ffn_w2, moe_router, moe_w13, moe_w2, ln2_g, ln2_b):
    batch, seq, _ = x.shape
    T = batch * seq
    tm = _pick(seq, 512)
    tq = _pick(seq, 512)
    tk = _pick(seq, 1024)
    row = lambda v: v.reshape(1, -1)
    pos_tabs = _position_tables(seq)

    h = x.reshape(T, D_MODEL)
    for l in range(DEPTH):
        lam_init = 0.8 - 0.6 * float(np.exp(-0.3 * l))
        weights, wg = _attention_weights(w_in[l])
        wuq, wukv = _mla_up_weights(mla_w_uq[l], mla_w_ukv[l])
        gains = _gqa_gain_rows(gqa_q_norm[l], gqa_k_norm[l])
        nq = jnp.pad(mla_q_norm[l], (0, 256 - MLA_Q_LORA)).reshape(1, 256)
        res = _project(h, batch, seq, l == 0, row(ln_emb_g), row(ln_emb_b), weights, wuq, wukv,
                       nq, row(mla_kv_norm[l]), gains, pos_tabs, tm)
        if l == 0:
            h, res = res[0], res[1:]
        mq, mk, mv, gq, gk, gv, dq, dk, dv = res[:9]
        dil = res[9:]
        ya = _flash(mq, mk, mv, tq, tk).reshape(T, BRANCH_W)
        yc = _flash(gq, gk, gv, tq, tk).reshape(T, BRANCH_W)
        sub = jnp.pad(diff_subln[l], (0, LANES - DIFF_V)).reshape(1, LANES)
        yd = _diff_attention(dq, dk, dv, diff_lambda[l], sub, lam_init, tq).reshape(T, BRANCH_W)
        dil_o, dil_lse = [], []
        for g in range(len(DIL_GROUPS)):
            o, lse = _dilated_group(dil[g], dil[3 + g], dil[6 + g], g)
            dil_o.append(o)
            dil_lse.append(lse)
        h = _mix(h, ya, yc, yd, dil_o, dil_lse, wg, b_gate[l], w_branch[l].astype(BF16),
                 w_out[l].astype(BF16), row(ln1_g[l]), row(ln1_b[l]), tm)
        if l % 2 == 0:
            h = _ffn_dense(h, ffn_w13[l // 2], ffn_w2[l // 2],
                           row(ln2_g[l]), row(ln2_b[l]), _pick(T, 1024), 896)
        else:
            h = _moe(h, moe_router[l // 2], moe_w13[l // 2], moe_w2[l // 2],
                     row(ln2_g[l]), row(ln2_b[l]), tm, _pick(T, 1024), 896)
    return h.reshape(batch, seq, D_MODEL)
```

```python
import functools

import jax
import jax.numpy as jnp
import numpy as np
from jax import lax
from jax.experimental import pallas as pl
from jax.experimental.pallas import tpu as pltpu

D_MODEL = 1024
GRID_W = 64
HEAD_DIM = 64
N_BRANCH = 4
BRANCH_W = 256
MLA_HEADS = 4
MLA_NOPE = 64
MLA_ROPE = 32
MLA_V = 64
MLA_Q_LORA = 192
MLA_KV_LORA = 128
DIL_GROUPS = ((128, 1), (512, 4), (2048, 16))
DIL_HEADS = 4
GQA_Q_HEADS = 4
GQA_KV_HEADS = 2
DIFF_HEADS = 4
DIFF_D = 32
DIFF_V = 64
D_FF = 3584
N_EXPERTS = 8
TOP_K = 2
ROPE_THETA = 10000.0
LN_EPS = 1e-5
RMS_EPS = 1e-6
DIFF_NORM_EPS = 1e-5
DEPTH = 2
DEEPNORM_ALPHA = (2 * DEPTH) ** 0.25

LANES = 128
HALF = LANES // 2
VMEM_LIMIT = 56 * 1024 * 1024

LOG2E = 1.4426950408889634
NEG = -1e30
BF16 = jnp.bfloat16
F32 = jnp.float32

_MLA_COLS = 640
_GQA_COLS = 896
_DIFF_COLS = 768
_DIL_COLS = 2304
_SMALL_COLS = _MLA_COLS + _GQA_COLS


def _cparams(sem):
    return pltpu.CompilerParams(dimension_semantics=sem, vmem_limit_bytes=VMEM_LIMIT)


def _resident(shape, index_map):
    return pl.BlockSpec(shape, index_map, pipeline_mode=pl.Buffered(1))


def _layernorm(x, g, b):
    mu = jnp.mean(x, axis=-1, keepdims=True)
    xc = x - mu
    var = jnp.mean(xc * xc, axis=-1, keepdims=True)
    return xc * lax.rsqrt(var + LN_EPS) * g + b


def _lane(shape):
    return lax.broadcasted_iota(jnp.int32, shape, len(shape) - 1)


def _split_pair(x):
    lo_mask = _lane(x.shape) < HALF
    lo = jnp.where(lo_mask, x, 0.0)
    hi = jnp.where(lo_mask, pltpu.roll(x, HALF, axis=1), 0.0)
    return lo, hi


def _join_pair(lo, hi):
    return jnp.where(_lane(lo.shape) < HALF, lo, pltpu.roll(hi, HALF, axis=1))


def _with_ones(v):
    return jnp.where(_lane(v.shape) == HALF, 1.0, v)


def _diff_slope(h):
    return 2.0 ** (-8.0 * (h + 1) / DIFF_HEADS) * LOG2E


def _split3(val):
    f1 = val.astype(BF16).astype(F32)
    r1 = val - f1
    f2 = r1.astype(BF16).astype(F32)
    f3 = (r1 - f2).astype(BF16).astype(F32)
    return f1, f2, f3


_FEAT = 2 * DIFF_D


def _key_bias_features(val):
    f1, f2, f3 = _split3(val)
    lane = _lane(val.shape)
    ones = jnp.where((lane >= _FEAT + 3) & (lane < _FEAT + 6), 1.0, 0.0)
    return jnp.where(lane == _FEAT, f1, jnp.where(lane == _FEAT + 1, f2, jnp.where(lane == _FEAT + 2, f3, ones)))


def _proj_kernel(apply_ln, tiles_per_seq, *refs):
    (x_ref, lng_ref, lnb_ref, ws_ref, wdiff_ref, wdil_ref, wuq_ref, wukv_ref, nq_ref, nkv_ref,
     gains_ref, c1_ref, s1_ref, ca_ref, sa_ref) = refs[:15]
    outs = refs[15:]
    if apply_ln:
        h_ref, outs = outs[0], outs[1:]
    (mq_ref, mk_ref, mv_ref, gq_ref, gk_ref, gv_ref, dq_ref, dk_ref, dv_ref) = outs[:9]
    dil_refs = outs[9:]

    x = x_ref[...]
    if apply_ln:
        x = _layernorm(x, lng_ref[...], lnb_ref[...])
        h_ref[...] = x
    hb = x.astype(BF16)

    def proj(w_ref, off, n):
        return jnp.dot(hb, w_ref[:, off:off + n], preferred_element_type=F32)

    p = proj(ws_ref, 0, _MLA_COLS)
    cq, ckv, kr, krr = p[:, 0:256], p[:, 256:384], p[:, 384:512], p[:, 512:640]
    rq = lax.rsqrt(jnp.sum(cq * cq, axis=-1, keepdims=True) * (1.0 / MLA_Q_LORA) + RMS_EPS)
    cqn = (cq * rq * nq_ref[...]).astype(BF16)
    q2 = jnp.dot(cqn, wuq_ref[...], preferred_element_type=F32)
    rkv = lax.rsqrt(jnp.sum(ckv * ckv, axis=-1, keepdims=True) * (1.0 / MLA_KV_LORA) + RMS_EPS)
    ckvn = (ckv * rkv * nkv_ref[...]).astype(BF16)
    kv2 = jnp.dot(ckvn, wukv_ref[...], preferred_element_type=F32)
    c1, s1 = c1_ref[...], s1_ref[...]
    k_rope = kr * c1 + krr * s1
    mla_scale = (MLA_NOPE + MLA_ROPE) ** -0.5 * LOG2E
    cq_t, sq_t = c1 * mla_scale, s1 * mla_scale
    for h in range(MLA_HEADS):
        qh = q2[:, h * LANES:(h + 1) * LANES]
        qrh = q2[:, (MLA_HEADS + h) * LANES:(MLA_HEADS + h + 1) * LANES]
        mq_ref[h] = (qh * cq_t + qrh * sq_t).astype(BF16)
        mk_ref[h] = (kv2[:, h * LANES:(h + 1) * LANES] + k_rope).astype(BF16)
        vh = kv2[:, (MLA_HEADS + h) * LANES:(MLA_HEADS + h + 1) * LANES]
        mv_ref[h] = _with_ones(vh).astype(BF16)

    p = proj(ws_ref, _MLA_COLS, _GQA_COLS)

    def head_rms(xs):
        ss = xs * xs
        lo_mask = _lane(xs.shape) < HALF
        s_all = jnp.sum(ss, axis=-1, keepdims=True)
        s_lo = jnp.sum(jnp.where(lo_mask, ss, 0.0), axis=-1, keepdims=True)
        return lax.rsqrt(jnp.where(lo_mask, s_lo, s_all - s_lo) * (1.0 / HEAD_DIM) + RMS_EPS)

    ca, sa = ca_ref[...], sa_ref[...]
    aq, bq = ca * gains_ref[0:1, :], sa * gains_ref[1:2, :]
    ak, bk = ca * gains_ref[2:3, :], sa * gains_ref[3:4, :]
    for s in range(2):
        qs = p[:, s * LANES:(s + 1) * LANES]
        qrs = p[:, 256 + s * LANES:256 + (s + 1) * LANES]
        qn = head_rms(qs) * (qs * aq + qrs * bq)
        lo, hi = _split_pair(qn)
        gq_ref[2 * s] = lo.astype(BF16)
        gq_ref[2 * s + 1] = hi.astype(BF16)
    ks, krs = p[:, 512:640], p[:, 640:768]
    kn = head_rms(ks) * (ks * ak + krs * bk)
    lo, hi = _split_pair(kn)
    gk_ref[0] = lo.astype(BF16)
    gk_ref[1] = hi.astype(BF16)
    lo, hi = _split_pair(p[:, 768:896])
    gv_ref[0] = _with_ones(lo).astype(BF16)
    gv_ref[1] = _with_ones(hi).astype(BF16)

    p = proj(wdiff_ref, 0, _DIFF_COLS)
    dscale = DIFF_D ** -0.5 * LOG2E
    tm = x.shape[0]
    pos = (pl.program_id(0) % tiles_per_seq * tm
           + lax.broadcasted_iota(jnp.int32, (tm, LANES), 0)).astype(F32)
    for s in range(2):
        lo, hi = _split_pair(p[:, s * LANES:(s + 1) * LANES] * dscale)
        dq_ref[2 * s] = lo.astype(BF16)
        dq_ref[2 * s + 1] = hi.astype(BF16)
        lo, hi = _split_pair(p[:, 256 + s * LANES:256 + (s + 1) * LANES])
        dk_ref[2 * s] = (lo + _key_bias_features(_diff_slope(2 * s) * pos)).astype(BF16)
        dk_ref[2 * s + 1] = (hi + _key_bias_features(_diff_slope(2 * s + 1) * pos)).astype(BF16)
        lo, hi = _split_pair(p[:, 512 + s * LANES:512 + (s + 1) * LANES])
        dv_ref[2 * s] = _with_ones(lo).astype(BF16)
        dv_ref[2 * s + 1] = _with_ones(hi).astype(BF16)

    lscale = HEAD_DIM ** -0.5 * LOG2E
    for i in range(9):
        seg = proj(wdil_ref, i * BRANCH_W, BRANCH_W)
        if i < 3:
            seg = seg * lscale
        dil_refs[i][0] = seg[:, :LANES]
        dil_refs[i][1] = seg[:, LANES:]


def _project(x2d, batch, seq, apply_ln, lng, lnb, weights, wuq, wukv, nq, nkv, gains, tables, tm):
    T = x2d.shape[0]
    nst = seq // tm
    grid = (T // tm,)
    row = lambda i: (i, 0)
    const = lambda i: (0, 0)
    tab = lambda i: (i % nst, 0)
    head = lambda i: (i // nst, 0, i % nst, 0)

    in_specs = [
        pl.BlockSpec((tm, D_MODEL), row),
        pl.BlockSpec((1, D_MODEL), const),
        pl.BlockSpec((1, D_MODEL), const),
        _resident((D_MODEL, _SMALL_COLS), const),
        _resident((D_MODEL, _DIFF_COLS), const),
        _resident((D_MODEL, _DIL_COLS), const),
        _resident((256, 8 * LANES), const),
        _resident((LANES, 8 * LANES), const),
        pl.BlockSpec((1, 256), const),
        pl.BlockSpec((1, LANES), const),
        pl.BlockSpec((4, LANES), const),
    ] + [pl.BlockSpec((tm, LANES), tab)] * 4

    def heads_out(n):
        return (jax.ShapeDtypeStruct((batch, n, seq, LANES), BF16),
                pl.BlockSpec((None, n, tm, LANES), head))

    outs = []
    if apply_ln:
        outs.append((jax.ShapeDtypeStruct((T, D_MODEL), F32), pl.BlockSpec((tm, D_MODEL), row)))
    outs += [heads_out(MLA_HEADS)] * 3
    outs += [heads_out(GQA_Q_HEADS), heads_out(GQA_KV_HEADS), heads_out(GQA_KV_HEADS)]
    outs += [heads_out(DIFF_HEADS)] * 3
    outs += [(jax.ShapeDtypeStruct((batch, 2, seq, LANES), F32), pl.BlockSpec((None, 2, tm, LANES), head))] * 9

    res = pl.pallas_call(
        functools.partial(_proj_kernel, apply_ln, nst),
        grid=grid,
        in_specs=in_specs,
        out_specs=[o[1] for o in outs],
        out_shape=[o[0] for o in outs],
        compiler_params=_cparams(("parallel",)),
        name="proj_prep",
    )(x2d, lng, lnb, *weights, wuq, wukv, nq, nkv, gains, *tables)
    return res


def _normalize(acc):
    return acc / acc[:, HALF:HALF + 1]


def _scores(q, k):
    return lax.dot_general(q, k, (((1,), (1,)), ((), ())), preferred_element_type=F32)


def _softmax_update(s, v, acc_ref, m_ref, idx):
    blocks = [s[:, j * LANES:(j + 1) * LANES] for j in range(s.shape[1] // LANES)]
    part = functools.reduce(jnp.maximum, blocks)
    m_prev = m_ref[idx]
    m_new = jnp.maximum(m_prev, jnp.max(part, axis=-1, keepdims=True))
    p = jnp.concatenate([jnp.exp2((b - m_new).astype(BF16)) for b in blocks], axis=1)
    acc_ref[idx] = (jnp.exp2(m_prev - m_new) * acc_ref[idx]
                    + jnp.dot(p, v, preferred_element_type=F32))
    m_ref[idx] = m_new


def _flash_kernel(n_rep, tk, q_ref, k_ref, v_ref, o_ref, acc_ref, m_ref):
    n_heads = q_ref.shape[0]
    seq = k_ref.shape[1]
    acc_ref[...] = jnp.zeros_like(acc_ref)
    m_ref[...] = jnp.full_like(m_ref, NEG)

    def body(c, carry):
        ks = pl.multiple_of(c * tk, tk)
        scores = [_scores(q_ref[h], k_ref[h // n_rep, pl.ds(ks, tk), :]) for h in range(n_heads)]
        for h in range(n_heads):
            _softmax_update(scores[h], v_ref[h // n_rep, pl.ds(ks, tk), :], acc_ref, m_ref, h)
        return carry

    lax.fori_loop(0, seq // tk, body, 0)
    outs = [_normalize(acc_ref[h]) for h in range(n_heads)]
    for s in range(n_heads // 2):
        o_ref[:, s * LANES:(s + 1) * LANES] = _join_pair(outs[2 * s], outs[2 * s + 1]).astype(o_ref.dtype)


def _flash(q, k, v, tq, tk):
    batch, n_heads, seq, _ = q.shape
    n_kv = k.shape[1]
    return pl.pallas_call(
        functools.partial(_flash_kernel, n_heads // n_kv, tk),
        grid=(batch, seq // tq),
        in_specs=[
            pl.BlockSpec((None, n_heads, tq, LANES), lambda b, i: (b, 0, i, 0)),
            _resident((None, n_kv, seq, LANES), lambda b, i: (b, 0, 0, 0)),
            _resident((None, n_kv, seq, LANES), lambda b, i: (b, 0, 0, 0)),
        ],
        out_specs=pl.BlockSpec((None, tq, n_heads * HALF), lambda b, i: (b, i, 0)),
        out_shape=jax.ShapeDtypeStruct((batch, seq, n_heads * HALF), BF16),
        scratch_shapes=[pltpu.VMEM((n_heads, tq, LANES), F32), pltpu.VMEM((n_heads, tq, LANES), F32)],
        compiler_params=_cparams(("parallel", "parallel")),
        name="flash_attn",
    )(q, k, v)


def _diff_kernel(lam_init, q_ref, k_ref, v_ref, lp_ref, sub_ref, o_ref, acc_ref, m_ref, qv_ref):
    n_heads, tq, _ = q_ref.shape
    seq = k_ref.shape[1]
    tile = pl.program_id(1)
    t0 = tile * tq
    lp = lp_ref[...]
    lam = (jnp.exp(jnp.sum(lp[0:1] * lp[1:2], keepdims=True))
           - jnp.exp(jnp.sum(lp[2:3] * lp[3:4], keepdims=True)) + lam_init)
    lane = _lane((tq, LANES))
    tpos = (t0 + lax.broadcasted_iota(jnp.int32, (tq, LANES), 0)).astype(F32)
    in_f = (lane >= _FEAT) & (lane < _FEAT + 3)
    for h in range(n_heads):
        q = q_ref[h].astype(F32)
        g1, g2, g3 = _split3(2.0 * _diff_slope(h) * tpos)
        left = jnp.where(in_f, 1.0, 0.0)
        right = jnp.where(in_f, -1.0, jnp.where(lane == _FEAT + 3, g1, jnp.where(
            lane == _FEAT + 4, g2, jnp.where(lane == _FEAT + 5, g3, 0.0))))
        for m in range(2):
            qm = jnp.where((lane >= m * DIFF_D) & (lane < (m + 1) * DIFF_D), q, 0.0)
            qv_ref[0, 2 * h + m] = (qm + left).astype(BF16)
            qv_ref[1, 2 * h + m] = (qm + right).astype(BF16)
            qv_ref[2, 2 * h + m] = qm.astype(BF16)
    acc_ref[...] = jnp.zeros_like(acc_ref)
    m_ref[...] = jnp.full_like(m_ref, NEG)

    def chunk(side, ks, bias):
        scores = [_scores(qv_ref[side, i], k_ref[i // 2, pl.ds(ks, tq), :]) for i in range(2 * n_heads)]
        for i in range(2 * n_heads):
            s = scores[i] if bias is None else scores[i] - bias[i // 2]
            _softmax_update(s, v_ref[i // 2, pl.ds(ks, tq), :], acc_ref, m_ref, i)

    def side_body(side):
        def body(c, carry):
            chunk(side, pl.multiple_of(c * tq, tq), None)
            return carry
        return body

    lax.fori_loop(0, tile, side_body(0), 0)
    dist = jnp.abs(lax.broadcasted_iota(jnp.int32, (tq, tq), 0)
                   - lax.broadcasted_iota(jnp.int32, (tq, tq), 1)).astype(F32)
    trow = (t0 + lax.broadcasted_iota(jnp.int32, (tq, tq), 0)).astype(F32)
    chunk(2, pl.multiple_of(t0, tq), [_diff_slope(h) * (dist - trow) for h in range(n_heads)])
    lax.fori_loop(tile + 1, seq // tq, side_body(1), 0)

    outs = []
    for h in range(n_heads):
        o = _normalize(acc_ref[2 * h]) - lam * _normalize(acc_ref[2 * h + 1])
        o = jnp.where(lane < DIFF_V, o, 0.0)
        ms = jnp.sum(o * o, axis=-1, keepdims=True) * (1.0 / DIFF_V)
        outs.append(o * lax.rsqrt(ms + DIFF_NORM_EPS) * sub_ref[...] * (1.0 - lam_init))
    for s in range(n_heads // 2):
        o_ref[:, s * LANES:(s + 1) * LANES] = _join_pair(outs[2 * s], outs[2 * s + 1]).astype(o_ref.dtype)


def _diff_attention(q, k, v, lam_params, subln, lam_init, tq):
    batch, n_heads, seq, _ = q.shape
    return pl.pallas_call(
        functools.partial(_diff_kernel, lam_init),
        grid=(batch, seq // tq),
        in_specs=[
            pl.BlockSpec((None, n_heads, tq, LANES), lambda b, i: (b, 0, i, 0)),
            _resident((None, n_heads, seq, LANES), lambda b, i: (b, 0, 0, 0)),
            _resident((None, n_heads, seq, LANES), lambda b, i: (b, 0, 0, 0)),
            pl.BlockSpec((4, DIFF_D), lambda b, i: (0, 0)),
            pl.BlockSpec((1, LANES), lambda b, i: (0, 0)),
        ],
        out_specs=pl.BlockSpec((None, tq, n_heads * HALF), lambda b, i: (b, i, 0)),
        out_shape=jax.ShapeDtypeStruct((batch, seq, n_heads * HALF), BF16),
        scratch_shapes=[pltpu.VMEM((2 * n_heads, tq, LANES), F32)] * 2
        + [pltpu.VMEM((3, 2 * n_heads, tq, LANES), BF16)],
        compiler_params=_cparams(("parallel", "parallel")),
        name="diff_attn",
    )(q, k, v, lam_params, subln)


_DIL_BLOCK = 2048
_DIL_Q = 128


def _dil_kernel(group, dil, seq, q_ref, kp_ref, kc_ref, kn_ref, vp_ref, vc_ref, vn_ref,
                o_ref, lse_ref, kw_ref, vw_ref):
    tb = q_ref.shape[1]
    halo = DIL_GROUPS[group][0] // 2
    assert halo == kp_ref.shape[1] and halo == (_DIL_Q // 2) * dil
    for w_ref, p_ref, c_ref, n_ref in ((kw_ref, kp_ref, kc_ref, kn_ref), (vw_ref, vp_ref, vc_ref, vn_ref)):
        w_ref[:, 0:halo] = p_ref[...]
        w_ref[:, halo:halo + tb] = c_ref[...]
        w_ref[:, halo + tb:halo + tb + halo] = n_ref[...]
    t0 = pl.program_id(1) * tb
    shape = (_DIL_Q, 2 * _DIL_Q)
    rel = lax.broadcasted_iota(jnp.int32, shape, 1) - lax.broadcasted_iota(jnp.int32, shape, 0) - _DIL_Q // 2
    absrel = jnp.abs(rel).astype(F32)
    band = jnp.where(absrel <= _DIL_Q // 2, 0.0, -NEG)
    n_slopes = len(DIL_GROUPS) * DIL_HEADS
    bias = [2.0 ** (-8.0 * (group * DIL_HEADS + h + 1) / n_slopes) * dil * LOG2E * absrel + band
            for h in range(DIL_HEADS)]
    col = lax.broadcasted_iota(jnp.int32, (1, 2 * _DIL_Q), 1)
    lo_mask = _lane((_DIL_Q, LANES)) < HALF
    for unit in range(tb // _DIL_Q):
        first = (unit // dil) * _DIL_Q * dil + unit % dil
        rows = pl.ds(first, _DIL_Q, stride=dil)
        win = pl.ds(first, 2 * _DIL_Q, stride=dil)
        key_token = t0 + (first - halo) + dil * col
        off_seq = jnp.where(key_token >= 0, jnp.where(key_token < seq, 0.0, NEG), NEG)
        for s in range(2):
            qs = q_ref[s, rows, :].astype(BF16)
            ksl = kw_ref[s, win, :].astype(BF16)
            vsl = vw_ref[s, win, :].astype(BF16)
            res = []
            for part in range(2):
                keep = lo_mask if part == 0 else jnp.logical_not(lo_mask)
                sc = _scores(jnp.where(keep, qs, jnp.zeros_like(qs)), ksl) - bias[2 * s + part] + off_seq
                m = jnp.max(sc, axis=-1, keepdims=True)
                p = jnp.exp2(sc - m)
                l = jnp.sum(p, axis=-1, keepdims=True)
                o = jnp.dot(p.astype(BF16), vsl, preferred_element_type=F32) / l
                res.append((o, m + jnp.log2(l)))
            o_ref[s, rows, :] = jnp.where(lo_mask, res[0][0], res[1][0])
            lse_ref[s, rows, :] = jnp.where(lo_mask, res[0][1], res[1][1])


def _dilated_group(q, k, v, group):
    batch, _, seq, _ = q.shape
    dil = DIL_GROUPS[group][1]
    halo = DIL_GROUPS[group][0] // 2
    tb = _DIL_BLOCK
    assert seq % tb == 0 and seq // dil >= 2 * _DIL_Q and tb % (_DIL_Q * dil) == 0
    per = tb // halo
    cur = pl.BlockSpec((None, 2, tb, LANES), lambda b, i: (b, 0, i, 0))
    prev = pl.BlockSpec((None, 2, halo, LANES), lambda b, i: (b, 0, jnp.maximum(i * per - 1, 0), 0))
    nxt = pl.BlockSpec((None, 2, halo, LANES),
                       lambda b, i: (b, 0, jnp.minimum((i + 1) * per, seq // halo - 1), 0))
    out_shape = jax.ShapeDtypeStruct((batch, 2, seq, LANES), F32)
    o, lse = pl.pallas_call(
        functools.partial(_dil_kernel, group, dil, seq),
        grid=(batch, seq // tb),
        in_specs=[cur, prev, cur, nxt, prev, cur, nxt],
        out_specs=[cur, cur],
        out_shape=[out_shape, out_shape],
        scratch_shapes=[pltpu.VMEM((2, tb + 2 * halo, LANES), F32)] * 2,
        compiler_params=_cparams(("parallel", "parallel")),
        name=f"dilated_g{group}",
    )(q, k, k, k, v, v, v)
    return o, lse


def _mix_kernel(h_ref, ya_ref, yc_ref, yd_ref, o0_ref, o1_ref, o2_ref, l0_ref, l1_ref, l2_ref,
                wg_ref, bg_ref, wb_ref, wo_ref, lng_ref, lnb_ref, out_ref):
    h = h_ref[...]
    hb = h.astype(BF16)
    slabs = []
    for s in range(2):
        l0, l1, l2 = l0_ref[s], l1_ref[s], l2_ref[s]
        mx = jnp.maximum(jnp.maximum(l0, l1), l2)
        e0, e1, e2 = jnp.exp2(l0 - mx), jnp.exp2(l1 - mx), jnp.exp2(l2 - mx)
        slabs.append(((e0 * o0_ref[s] + e1 * o1_ref[s] + e2 * o2_ref[s]) / (e0 + e1 + e2)).astype(BF16))
    ys = (ya_ref[...], jnp.concatenate(slabs, axis=1), yc_ref[...], yd_ref[...])
    acc = None
    for n in range(N_BRANCH):
        logit = jnp.dot(hb, wg_ref[:, n * D_MODEL:(n + 1) * D_MODEL],
                        preferred_element_type=F32) + bg_ref[n:n + 1, :]
        gate = 1.0 / (1.0 + jnp.exp(-logit))
        term = gate * jnp.dot(ys[n], wb_ref[n], preferred_element_type=F32)
        acc = term if acc is None else acc + term
    m = jnp.dot(acc.astype(BF16), wo_ref[...], preferred_element_type=F32)
    out_ref[...] = _layernorm(DEEPNORM_ALPHA * h + m, lng_ref[...], lnb_ref[...])


def _mix(h, ya, yc, yd, dil_o, dil_lse, wg, bg, wb, wo, lng, lnb, tm):
    T = h.shape[0]
    nst = dil_o[0].shape[2] // tm
    slab = pl.BlockSpec((None, 2, tm, LANES), lambda i: (i // nst, 0, i % nst, 0))
    row = lambda i: (i, 0)
    const = lambda i: (0, 0)
    wide = pl.BlockSpec((tm, D_MODEL), row)
    narrow = pl.BlockSpec((tm, BRANCH_W), row)
    return pl.pallas_call(
        _mix_kernel,
        grid=(T // tm,),
        in_specs=[wide] + [narrow] * 3 + [slab] * 6 + [
            _resident((D_MODEL, N_BRANCH * D_MODEL), const),
            pl.BlockSpec((N_BRANCH, D_MODEL), const),
            _resident((N_BRANCH, BRANCH_W, D_MODEL), lambda i: (0, 0, 0)),
            _resident((D_MODEL, D_MODEL), const),
            pl.BlockSpec((1, D_MODEL), const),
            pl.BlockSpec((1, D_MODEL), const),
        ],
        out_specs=wide,
        out_shape=jax.ShapeDtypeStruct((T, D_MODEL), F32),
        compiler_params=_cparams(("parallel",)),
        name="gate_mix",
    )(h, ya, yc, yd, *dil_o, *dil_lse, wg, bg, wb, wo, lng, lnb)


def _swiglu_step(x_ref, w1_ref, w3_ref, w2_ref, acc_ref):
    @pl.when(pl.program_id(1) == 0)
    def _():
        acc_ref[...] = jnp.zeros_like(acc_ref)

    xb = x_ref[...].astype(BF16)
    a = jnp.dot(xb, w1_ref[...].astype(BF16), preferred_element_type=F32)
    b = jnp.dot(xb, w3_ref[...].astype(BF16), preferred_element_type=F32)
    g = (a / (1.0 + jnp.exp(-a)) * b).astype(BF16)
    acc_ref[...] += jnp.dot(g, w2_ref[...].astype(BF16), preferred_element_type=F32)


def _ffn_dense_kernel(x_ref, w1_ref, w3_ref, w2_ref, lng_ref, lnb_ref, o_ref, acc_ref):
    _swiglu_step(x_ref, w1_ref, w3_ref, w2_ref, acc_ref)

    @pl.when(pl.program_id(1) == pl.num_programs(1) - 1)
    def _():
        o_ref[...] = _layernorm(DEEPNORM_ALPHA * x_ref[...] + acc_ref[...], lng_ref[...], lnb_ref[...])


def _ffn_dense(x, w13, w2, lng, lnb, tm, tf):
    T = x.shape[0]
    nf = D_FF // tf
    return pl.pallas_call(
        _ffn_dense_kernel,
        grid=(T // tm, nf),
        in_specs=[
            pl.BlockSpec((tm, D_MODEL), lambda i, j: (i, 0)),
            pl.BlockSpec((D_MODEL, tf), lambda i, j: (0, j)),
            pl.BlockSpec((D_MODEL, tf), lambda i, j: (0, nf + j)),
            pl.BlockSpec((tf, D_MODEL), lambda i, j: (j, 0)),
            pl.BlockSpec((1, D_MODEL), lambda i, j: (0, 0)),
            pl.BlockSpec((1, D_MODEL), lambda i, j: (0, 0)),
        ],
        out_specs=pl.BlockSpec((tm, D_MODEL), lambda i, j: (i, 0)),
        out_shape=jax.ShapeDtypeStruct((T, D_MODEL), F32),
        scratch_shapes=[pltpu.VMEM((tm, D_MODEL), F32)],
        compiler_params=_cparams(("parallel", "arbitrary")),
        name="ffn_dense",
    )(x, w13, w13, w2, lng, lnb)


def _ffn_expert_kernel(te_ref, used_ref, x_ref, w1_ref, w3_ref, w2_ref, o_ref, acc_ref):
    del te_ref
    live = pl.program_id(0) < used_ref[0]
    last = pl.program_id(1) == pl.num_programs(1) - 1

    @pl.when(live)
    def _():
        _swiglu_step(x_ref, w1_ref, w3_ref, w2_ref, acc_ref)

    @pl.when(live & last)
    def _():
        o_ref[...] = acc_ref[...].astype(o_ref.dtype)

    @pl.when(jnp.logical_not(live) & last)
    def _():
        o_ref[...] = jnp.zeros_like(o_ref)


def _ffn_experts(tile_expert, tiles_used, x_sorted, w13, w2, tm, tf):
    rows = x_sorted.shape[0]
    nf = D_FF // tf
    fblk = lambda i, j, nu: jnp.where(i < nu[0], j, nf - 1)
    grid_spec = pltpu.PrefetchScalarGridSpec(
        num_scalar_prefetch=2,
        grid=(rows // tm, nf),
        in_specs=[
            pl.BlockSpec((tm, D_MODEL), lambda i, j, te, nu: (i, 0)),
            pl.BlockSpec((None, D_MODEL, tf), lambda i, j, te, nu: (te[i], 0, fblk(i, j, nu))),
            pl.BlockSpec((None, D_MODEL, tf), lambda i, j, te, nu: (te[i], 0, nf + fblk(i, j, nu))),
            pl.BlockSpec((None, tf, D_MODEL), lambda i, j, te, nu: (te[i], fblk(i, j, nu), 0)),
        ],
        out_specs=pl.BlockSpec((tm, D_MODEL), lambda i, j, te, nu: (i, 0)),
        scratch_shapes=[pltpu.VMEM((tm, D_MODEL), F32)],
    )
    return pl.pallas_call(
        _ffn_expert_kernel,
        grid_spec=grid_spec,
        out_shape=jax.ShapeDtypeStruct((rows, D_MODEL), BF16),
        compiler_params=_cparams(("parallel", "arbitrary")),
        name="ffn_experts",
    )(tile_expert, tiles_used, x_sorted, w13, w13, w2)


def _router_kernel(x_ref, wr_ref, idx_ref, wt_ref):
    logits = jnp.dot(x_ref[...], wr_ref[...], preferred_element_type=F32,
                     precision=lax.Precision.HIGHEST)
    lane = _lane(logits.shape)
    logits = jnp.where(lane < N_EXPERTS, logits, -jnp.inf)
    v1 = jnp.max(logits, axis=-1, keepdims=True)
    i1 = jnp.min(jnp.where(logits == v1, lane, LANES), axis=-1, keepdims=True)
    rest = jnp.where(lane == i1, -jnp.inf, logits)
    v2 = jnp.max(rest, axis=-1, keepdims=True)
    i2 = jnp.min(jnp.where(rest == v2, lane, LANES), axis=-1, keepdims=True)
    e2 = jnp.exp(v2 - v1)
    w1 = 1.0 / (1.0 + e2)
    idx_ref[...] = jnp.where(lane == 0, i1, jnp.where(lane == 1, i2, 0))
    wt_ref[...] = jnp.where(lane == 0, w1, jnp.where(lane == 1, e2 * w1, 0.0))


def _router(x, w_router, tm):
    T = x.shape[0]
    wr = jnp.pad(w_router, ((0, 0), (0, LANES - N_EXPERTS)))
    row = lambda i: (i, 0)
    return pl.pallas_call(
        _router_kernel,
        grid=(T // tm,),
        in_specs=[pl.BlockSpec((tm, D_MODEL), row), pl.BlockSpec((D_MODEL, LANES), lambda i: (0, 0))],
        out_specs=[pl.BlockSpec((tm, LANES), row), pl.BlockSpec((tm, LANES), row)],
        out_shape=[jax.ShapeDtypeStruct((T, LANES), jnp.int32), jax.ShapeDtypeStruct((T, LANES), F32)],
        compiler_params=_cparams(("parallel",)),
        name="router",
    )(x, wr)


def _moe_out_kernel(h_ref, y0_ref, y1_ref, wt_ref, lng_ref, lnb_ref, o_ref):
    wt = wt_ref[...]
    f = wt[:, 0:1] * y0_ref[...] + wt[:, 1:2] * y1_ref[...]
    o_ref[...] = _layernorm(DEEPNORM_ALPHA * h_ref[...] + f, lng_ref[...], lnb_ref[...])


def _moe_out(h, y0, y1, wt, lng, lnb, tm):
    T = h.shape[0]
    row = lambda i: (i, 0)
    const = lambda i: (0, 0)
    wide = pl.BlockSpec((tm, D_MODEL), row)
    return pl.pallas_call(
        _moe_out_kernel,
        grid=(T // tm,),
        in_specs=[wide, wide, wide, pl.BlockSpec((tm, LANES), row),
                  pl.BlockSpec((1, D_MODEL), const), pl.BlockSpec((1, D_MODEL), const)],
        out_specs=wide,
        out_shape=jax.ShapeDtypeStruct((T, D_MODEL), F32),
        compiler_params=_cparams(("parallel",)),
        name="moe_out",
    )(h, y0, y1, wt, lng, lnb)


def _moe(h, w_router, w13, w2, lng, lnb, tm_router, tm_e, tf):
    T = h.shape[0]
    idx, wt = _router(h, w_router, tm_router)
    e_flat = jnp.concatenate([idx[:, s] for s in range(TOP_K)])
    onehot = (e_flat[None, :] == jnp.arange(N_EXPERTS, dtype=jnp.int32)[:, None]).astype(jnp.int32)
    counts = jnp.sum(onehot, axis=1)
    rank = jnp.sum((jnp.cumsum(onehot, axis=1) - onehot) * onehot, axis=0)
    padded = (counts + tm_e - 1) // tm_e * tm_e
    ends = jnp.cumsum(padded)
    pos = jnp.sum(onehot * (ends - padded)[:, None], axis=0) + rank
    rows = TOP_K * T + N_EXPERTS * tm_e
    token = jnp.arange(TOP_K * T, dtype=jnp.int32) % T
    src = (jnp.arange(rows, dtype=jnp.int32) % T).at[pos].set(
        token, unique_indices=True, mode="promise_in_bounds")
    tile_start = jnp.arange(rows // tm_e, dtype=jnp.int32) * tm_e
    tile_expert = jnp.minimum(jnp.sum((tile_start[:, None] >= ends[None, :]).astype(jnp.int32), axis=1),
                              N_EXPERTS - 1)
    x_sorted = h.astype(BF16).at[src].get(mode="promise_in_bounds")
    tiles_used = (ends[N_EXPERTS - 1:] // tm_e).astype(jnp.int32)
    y = _ffn_experts(tile_expert, tiles_used, x_sorted, w13, w2, tm_e, tf)
    y0 = y.at[pos[:T]].get(mode="promise_in_bounds")
    y1 = y.at[pos[T:]].get(mode="promise_in_bounds")
    return _moe_out(h, y0, y1, wt, lng, lnb, tm_router)


def _rot_cols(w, block):
    d, n = w.shape
    w4 = w.reshape(d, n // block, 2, block // 2)
    return jnp.concatenate([-w4[:, :, 1], w4[:, :, 0]], axis=2).reshape(d, n)


def _attention_weights(w_in):
    o = np.cumsum([0, MLA_Q_LORA, MLA_KV_LORA, MLA_ROPE, 2304, 256, 256, 512 + 256, N_BRANCH * D_MODEL])
    cq, ckv, kr, dil, gq, gkv, diff, gate = (w_in[:, o[i]:o[i + 1]] for i in range(8))
    z = lambda n: jnp.zeros((D_MODEL, n), w_in.dtype)
    mla = [cq, z(64), ckv, z(64), kr, z(32), z(64), _rot_cols(kr, MLA_ROPE), z(32)]
    gqa = [gq, _rot_cols(gq, 32), gkv[:, :128], _rot_cols(gkv[:, :128], 32), gkv[:, 128:]]
    small = jnp.concatenate(mla + gqa, axis=1)
    return [w.astype(BF16) for w in (small, diff, dil)], gate.astype(BF16)


def _mla_up_weights(w_uq, w_ukv):
    zq = lambda n: jnp.zeros((MLA_Q_LORA, n), w_uq.dtype)
    per = MLA_NOPE + MLA_ROPE
    plain, rot = [], []
    for h in range(MLA_HEADS):
        nope = w_uq[:, h * per:h * per + MLA_NOPE]
        rope = w_uq[:, h * per + MLA_NOPE:(h + 1) * per]
        plain += [nope, rope, zq(32)]
        rot += [zq(64), _rot_cols(rope, MLA_ROPE), zq(32)]
    wuq = jnp.concatenate(plain + rot, axis=1)
    wuq = jnp.pad(wuq, ((0, 256 - MLA_Q_LORA), (0, 0)))
    zk = jnp.zeros((MLA_KV_LORA, 64), w_ukv.dtype)
    per = MLA_NOPE + MLA_V
    ks, vs = [], []
    for h in range(MLA_HEADS):
        ks += [w_ukv[:, h * per:h * per + MLA_NOPE], zk]
        vs += [w_ukv[:, h * per + MLA_NOPE:(h + 1) * per], zk]
    wukv = jnp.concatenate(ks + vs, axis=1)
    return wuq.astype(BF16), wukv.astype(BF16)


def _rope_angles(pos, dim):
    inv = ROPE_THETA ** (-(jnp.arange(0, dim, 2, dtype=F32) / dim))
    return pos[:, None] * inv[None, :]


def _position_tables(seq):
    rows = seq // GRID_W
    row_idx = jnp.repeat(jnp.arange(rows, dtype=F32), GRID_W)
    col_idx = jnp.tile(jnp.arange(GRID_W, dtype=F32), rows)
    a1 = _rope_angles(jnp.arange(seq, dtype=F32), MLA_ROPE)
    ar = _rope_angles(row_idx, HEAD_DIM // 2)
    ac = _rope_angles(col_idx, HEAD_DIM // 2)
    z = lambda n: jnp.zeros((seq, n), F32)
    c1 = jnp.concatenate([jnp.ones((seq, MLA_NOPE), F32)] + [jnp.cos(a1)] * 2 + [z(32)], axis=1)
    s1 = jnp.concatenate([z(MLA_NOPE)] + [jnp.sin(a1)] * 2 + [z(32)], axis=1)
    ca = jnp.concatenate(([jnp.cos(ar)] * 2 + [jnp.cos(ac)] * 2) * 2, axis=1)
    sa = jnp.concatenate(([jnp.sin(ar)] * 2 + [jnp.sin(ac)] * 2) * 2, axis=1)
    return [c1, s1, ca, sa]


def _gqa_gain_rows(q_norm, k_norm):
    def swap(g):
        return jnp.concatenate([g[16:32], g[0:16], g[48:64], g[32:48]])

    sg = HEAD_DIM ** -0.5 * LOG2E
    rows = [sg * q_norm, sg * swap(q_norm), k_norm, swap(k_norm)]
    return jnp.stack([jnp.concatenate([r, r]) for r in rows])


def _pick(n, pref):
    t = min(n, pref)
    assert n % t == 0, (n, pref)
    return t


def kernel(x, ln_emb_g, ln_emb_b, w_in, b_gate, mla_q_norm, mla_kv_norm, mla_w_uq, mla_w_ukv,
           gqa_q_norm, gqa_k_norm, diff_lambda, diff_subln, w_branch, w_out, ln1_g, ln1_b,
           ffn_w13, ffn_w2, moe_router, moe_w13, moe_w2, ln2_g, ln2_b):
    batch, seq, _ = x.shape
    T = batch * seq
    tm = _pick(seq, 512)
    tq = _pick(seq, 512)
    tk = _pick(seq, 1024)
    row = lambda v: v.reshape(1, -1)
    pos_tabs = _position_tables(seq)

    h = x.reshape(T, D_MODEL)
    for l in range(DEPTH):
        lam_init = 0.8 - 0.6 * float(np.exp(-0.3 * l))
        weights, wg = _attention_weights(w_in[l])
        wuq, wukv = _mla_up_weights(mla_w_uq[l], mla_w_ukv[l])
        gains = _gqa_gain_rows(gqa_q_norm[l], gqa_k_norm[l])
        nq = jnp.pad(mla_q_norm[l], (0, 256 - MLA_Q_LORA)).reshape(1, 256)
        res = _project(h, batch, seq, l == 0, row(ln_emb_g), row(ln_emb_b), weights, wuq, wukv,
                       nq, row(mla_kv_norm[l]), gains, pos_tabs, tm)
        if l == 0:
            h, res = res[0], res[1:]
        mq, mk, mv, gq, gk, gv, dq, dk, dv = res[:9]
        dil = res[9:]
        ya = _flash(mq, mk, mv, tq, tk).reshape(T, BRANCH_W)
        yc = _flash(gq, gk, gv, tq, tk).reshape(T, BRANCH_W)
        sub = jnp.pad(diff_subln[l], (0, LANES - DIFF_V)).reshape(1, LANES)
        yd = _diff_attention(dq, dk, dv, diff_lambda[l], sub, lam_init, tq).reshape(T, BRANCH_W)
        dil_o, dil_lse = [], []
        for g in range(len(DIL_GROUPS)):
            o, lse = _dilated_group(dil[g], dil[3 + g], dil[6 + g], g)
            dil_o.append(o)
            dil_lse.append(lse)
        h = _mix(h, ya, yc, yd, dil_o, dil_lse, wg, b_gate[l], w_branch[l].astype(BF16),
                 w_out[l].astype(BF16), row(ln1_g[l]), row(ln1_b[l]), tm)
        if l % 2 == 0:
            h = _ffn_dense(h, ffn_w13[l // 2], ffn_w2[l // 2],
                           row(ln2_g[l]), row(ln2_b[l]), _pick(T, 1024), 896)
        else:
            h = _moe(h, moe_router[l // 2], moe_w13[l // 2], moe_w2[l // 2],
                     row(ln2_g[l]), row(ln2_b[l]), tm, _pick(T, 1024), 896)
    return h.reshape(batch, seq, D_MODEL)
```

```python
import functools

import jax
import jax.numpy as jnp
import numpy as np
from jax import lax
from jax.experimental import pallas as pl
from jax.experimental.pallas import tpu as pltpu

D_MODEL = 1024
GRID_W = 64
HEAD_DIM = 64
N_BRANCH = 4
BRANCH_W = 256
MLA_HEADS = 4
MLA_NOPE = 64
MLA_ROPE = 32
MLA_V = 64
MLA_Q_LORA = 192
MLA_KV_LORA = 128
DIL_GROUPS = ((128, 1), (512, 4), (2048, 16))
DIL_HEADS = 4
GQA_Q_HEADS = 4
GQA_KV_HEADS = 2
DIFF_HEADS = 4
DIFF_D = 32
DIFF_V = 64
D_FF = 3584
N_EXPERTS = 8
TOP_K = 2
ROPE_THETA = 10000.0
LN_EPS = 1e-5
RMS_EPS = 1e-6
DIFF_NORM_EPS = 1e-5
DEPTH = 2
DEEPNORM_ALPHA = (2 * DEPTH) ** 0.25

LANES = 128
HALF = LANES // 2
VMEM_LIMIT = 56 * 1024 * 1024

LOG2E = 1.4426950408889634
NEG = -1e30
BF16 = jnp.bfloat16
F32 = jnp.float32

_MLA_COLS = 640
_GQA_COLS = 896
_DIFF_COLS = 768
_DIL_COLS = 2304
_SMALL_COLS = _MLA_COLS + _GQA_COLS


def _cparams(sem):
    return pltpu.CompilerParams(dimension_semantics=sem, vmem_limit_bytes=VMEM_LIMIT)


def _resident(shape, index_map):
    return pl.BlockSpec(shape, index_map, pipeline_mode=pl.Buffered(1))


def _layernorm(x, g, b):
    mu = jnp.mean(x, axis=-1, keepdims=True)
    xc = x - mu
    var = jnp.mean(xc * xc, axis=-1, keepdims=True)
    return xc * lax.rsqrt(var + LN_EPS) * g + b


def _lane(shape):
    return lax.broadcasted_iota(jnp.int32, shape, len(shape) - 1)


def _split_pair(x):
    lo_mask = _lane(x.shape) < HALF
    lo = jnp.where(lo_mask, x, 0.0)
    hi = jnp.where(lo_mask, pltpu.roll(x, HALF, axis=1), 0.0)
    return lo, hi


def _join_pair(lo, hi):
    return jnp.where(_lane(lo.shape) < HALF, lo, pltpu.roll(hi, HALF, axis=1))


def _with_ones(v):
    return jnp.where(_lane(v.shape) == HALF, 1.0, v)


def _diff_slope(h):
    return 2.0 ** (-8.0 * (h + 1) / DIFF_HEADS) * LOG2E


def _split3(val):
    f1 = val.astype(BF16).astype(F32)
    r1 = val - f1
    f2 = r1.astype(BF16).astype(F32)
    f3 = (r1 - f2).astype(BF16).astype(F32)
    return f1, f2, f3


_FEAT = 2 * DIFF_D


def _key_bias_features(val):
    f1, f2, f3 = _split3(val)
    lane = _lane(val.shape)
    ones = jnp.where((lane >= _FEAT + 3) & (lane < _FEAT + 6), 1.0, 0.0)
    return jnp.where(lane == _FEAT, f1, jnp.where(lane == _FEAT + 1, f2, jnp.where(lane == _FEAT + 2, f3, ones)))


def _proj_kernel(apply_ln, tiles_per_seq, *refs):
    (x_ref, lng_ref, lnb_ref, ws_ref, wdiff_ref, wdil_ref, wuq_ref, wukv_ref, nq_ref, nkv_ref,
     gains_ref, c1_ref, s1_ref, ca_ref, sa_ref) = refs[:15]
    outs = refs[15:]
    if apply_ln:
        h_ref, outs = outs[0], outs[1:]
    (mq_ref, mk_ref, mv_ref, gq_ref, gk_ref, gv_ref, dq_ref, dk_ref, dv_ref) = outs[:9]
    dil_refs = outs[9:]

    x = x_ref[...]
    if apply_ln:
        x = _layernorm(x, lng_ref[...], lnb_ref[...])
        h_ref[...] = x
    hb = x.astype(BF16)

    def proj(w_ref, off, n):
        return jnp.dot(hb, w_ref[:, off:off + n], preferred_element_type=F32)

    p = proj(ws_ref, 0, _MLA_COLS)
    cq, ckv, kr, krr = p[:, 0:256], p[:, 256:384], p[:, 384:512], p[:, 512:640]
    rq = lax.rsqrt(jnp.sum(cq * cq, axis=-1, keepdims=True) * (1.0 / MLA_Q_LORA) + RMS_EPS)
    cqn = (cq * rq * nq_ref[...]).astype(BF16)
    q2 = jnp.dot(cqn, wuq_ref[...], preferred_element_type=F32)
    rkv = lax.rsqrt(jnp.sum(ckv * ckv, axis=-1, keepdims=True) * (1.0 / MLA_KV_LORA) + RMS_EPS)
    ckvn = (ckv * rkv * nkv_ref[...]).astype(BF16)
    kv2 = jnp.dot(ckvn, wukv_ref[...], preferred_element_type=F32)
    c1, s1 = c1_ref[...], s1_ref[...]
    k_rope = kr * c1 + krr * s1
    mla_scale = (MLA_NOPE + MLA_ROPE) ** -0.5 * LOG2E
    cq_t, sq_t = c1 * mla_scale, s1 * mla_scale
    for h in range(MLA_HEADS):
        qh = q2[:, h * LANES:(h + 1) * LANES]
        qrh = q2[:, (MLA_HEADS + h) * LANES:(MLA_HEADS + h + 1) * LANES]
        mq_ref[h] = (qh * cq_t + qrh * sq_t).astype(BF16)
        mk_ref[h] = (kv2[:, h * LANES:(h + 1) * LANES] + k_rope).astype(BF16)
        vh = kv2[:, (MLA_HEADS + h) * LANES:(MLA_HEADS + h + 1) * LANES]
        mv_ref[h] = _with_ones(vh).astype(BF16)

    p = proj(ws_ref, _MLA_COLS, _GQA_COLS)

    def head_rms(xs):
        ss = xs * xs
        lo_mask = _lane(xs.shape) < HALF
        s_all = jnp.sum(ss, axis=-1, keepdims=True)
        s_lo = jnp.sum(jnp.where(lo_mask, ss, 0.0), axis=-1, keepdims=True)
        return lax.rsqrt(jnp.where(lo_mask, s_lo, s_all - s_lo) * (1.0 / HEAD_DIM) + RMS_EPS)

    ca, sa = ca_ref[...], sa_ref[...]
    aq, bq = ca * gains_ref[0:1, :], sa * gains_ref[1:2, :]
    ak, bk = ca * gains_ref[2:3, :], sa * gains_ref[3:4, :]
    for s in range(2):
        qs = p[:, s * LANES:(s + 1) * LANES]
        qrs = p[:, 256 + s * LANES:256 + (s + 1) * LANES]
        qn = head_rms(qs) * (qs * aq + qrs * bq)
        lo, hi = _split_pair(qn)
        gq_ref[2 * s] = lo.astype(BF16)
        gq_ref[2 * s + 1] = hi.astype(BF16)
    ks, krs = p[:, 512:640], p[:, 640:768]
    kn = head_rms(ks) * (ks * ak + krs * bk)
    lo, hi = _split_pair(kn)
    gk_ref[0] = lo.astype(BF16)
    gk_ref[1] = hi.astype(BF16)
    lo, hi = _split_pair(p[:, 768:896])
    gv_ref[0] = _with_ones(lo).astype(BF16)
    gv_ref[1] = _with_ones(hi).astype(BF16)

    p = proj(wdiff_ref, 0, _DIFF_COLS)
    dscale = DIFF_D ** -0.5 * LOG2E
    tm = x.shape[0]
    pos = (pl.program_id(0) % tiles_per_seq * tm
           + lax.broadcasted_iota(jnp.int32, (tm, LANES), 0)).astype(F32)
    for s in range(2):
        lo, hi = _split_pair(p[:, s * LANES:(s + 1) * LANES] * dscale)
        dq_ref[2 * s] = lo.astype(BF16)
        dq_ref[2 * s + 1] = hi.astype(BF16)
        lo, hi = _split_pair(p[:, 256 + s * LANES:256 + (s + 1) * LANES])
        dk_ref[2 * s] = (lo + _key_bias_features(_diff_slope(2 * s) * pos)).astype(BF16)
        dk_ref[2 * s + 1] = (hi + _key_bias_features(_diff_slope(2 * s + 1) * pos)).astype(BF16)
        lo, hi = _split_pair(p[:, 512 + s * LANES:512 + (s + 1) * LANES])
        dv_ref[2 * s] = _with_ones(lo).astype(BF16)
        dv_ref[2 * s + 1] = _with_ones(hi).astype(BF16)

    lscale = HEAD_DIM ** -0.5 * LOG2E
    for i in range(9):
        seg = proj(wdil_ref, i * BRANCH_W, BRANCH_W)
        if i < 3:
            seg = seg * lscale
        dil_refs[i][0] = seg[:, :LANES]
        dil_refs[i][1] = seg[:, LANES:]


def _project(x2d, batch, seq, apply_ln, lng, lnb, weights, wuq, wukv, nq, nkv, gains, tables, tm):
    T = x2d.shape[0]
    nst = seq // tm
    grid = (T // tm,)
    row = lambda i: (i, 0)
    const = lambda i: (0, 0)
    tab = lambda i: (i % nst, 0)
    head = lambda i: (i // nst, 0, i % nst, 0)

    in_specs = [
        pl.BlockSpec((tm, D_MODEL), row),
        pl.BlockSpec((1, D_MODEL), const),
        pl.BlockSpec((1, D_MODEL), const),
        _resident((D_MODEL, _SMALL_COLS), const),
        _resident((D_MODEL, _DIFF_COLS), const),
        _resident((D_MODEL, _DIL_COLS), const),
        _resident((256, 8 * LANES), const),
        _resident((LANES, 8 * LANES), const),
        pl.BlockSpec((1, 256), const),
        pl.BlockSpec((1, LANES), const),
        pl.BlockSpec((4, LANES), const),
    ] + [pl.BlockSpec((tm, LANES), tab)] * 4

    def heads_out(n):
        return (jax.ShapeDtypeStruct((batch, n, seq, LANES), BF16),
                pl.BlockSpec((None, n, tm, LANES), head))

    outs = []
    if apply_ln:
        outs.append((jax.ShapeDtypeStruct((T, D_MODEL), F32), pl.BlockSpec((tm, D_MODEL), row)))
    outs += [heads_out(MLA_HEADS)] * 3
    outs += [heads_out(GQA_Q_HEADS), heads_out(GQA_KV_HEADS), heads_out(GQA_KV_HEADS)]
    outs += [heads_out(DIFF_HEADS)] * 3
    outs += [(jax.ShapeDtypeStruct((batch, 2, seq, LANES), F32), pl.BlockSpec((None, 2, tm, LANES), head))] * 9

    res = pl.pallas_call(
        functools.partial(_proj_kernel, apply_ln, nst),
        grid=grid,
        in_specs=in_specs,
        out_specs=[o[1] for o in outs],
        out_shape=[o[0] for o in outs],
        compiler_params=_cparams(("parallel",)),
        name="proj_prep",
    )(x2d, lng, lnb, *weights, wuq, wukv, nq, nkv, gains, *tables)
    return res


def _normalize(acc):
    return acc / acc[:, HALF:HALF + 1]


def _scores(q, k):
    return lax.dot_general(q, k, (((1,), (1,)), ((), ())), preferred_element_type=F32)


def _softmax_update(s, v, acc_ref, m_ref, idx):
    blocks = [s[:, j * LANES:(j + 1) * LANES] for j in range(s.shape[1] // LANES)]
    part = functools.reduce(jnp.maximum, blocks)
    m_prev = m_ref[idx]
    m_new = jnp.maximum(m_prev, jnp.max(part, axis=-1, keepdims=True))
    p = jnp.concatenate([jnp.exp2((b - m_new).astype(BF16)) for b in blocks], axis=1)
    acc_ref[idx] = (jnp.exp2(m_prev - m_new) * acc_ref[idx]
                    + jnp.dot(p, v, preferred_element_type=F32))
    m_ref[idx] = m_new


def _flash_kernel(n_rep, tk, q_ref, k_ref, v_ref, o_ref, acc_ref, m_ref):
    n_heads = q_ref.shape[0]
    seq = k_ref.shape[1]
    acc_ref[...] = jnp.zeros_like(acc_ref)
    m_ref[...] = jnp.full_like(m_ref, NEG)

    def body(c, carry):
        ks = pl.multiple_of(c * tk, tk)
        scores = [_scores(q_ref[h], k_ref[h // n_rep, pl.ds(ks, tk), :]) for h in range(n_heads)]
        for h in range(n_heads):
            _softmax_update(scores[h], v_ref[h // n_rep, pl.ds(ks, tk), :], acc_ref, m_ref, h)
        return carry

    lax.fori_loop(0, seq // tk, body, 0)
    outs = [_normalize(acc_ref[h]) for h in range(n_heads)]
    for s in range(n_heads // 2):
        o_ref[:, s * LANES:(s + 1) * LANES] = _join_pair(outs[2 * s], outs[2 * s + 1]).astype(o_ref.dtype)


def _flash(q, k, v, tq, tk):
    batch, n_heads, seq, _ = q.shape
    n_kv = k.shape[1]
    return pl.pallas_call(
        functools.partial(_flash_kernel, n_heads // n_kv, tk),
        grid=(batch, seq // tq),
        in_specs=[
            pl.BlockSpec((None, n_heads, tq, LANES), lambda b, i: (b, 0, i, 0)),
            _resident((None, n_kv, seq, LANES), lambda b, i: (b, 0, 0, 0)),
            _resident((None, n_kv, seq, LANES), lambda b, i: (b, 0, 0, 0)),
        ],
        out_specs=pl.BlockSpec((None, tq, n_heads * HALF), lambda b, i: (b, i, 0)),
        out_shape=jax.ShapeDtypeStruct((batch, seq, n_heads * HALF), BF16),
        scratch_shapes=[pltpu.VMEM((n_heads, tq, LANES), F32), pltpu.VMEM((n_heads, tq, LANES), F32)],
        compiler_params=_cparams(("parallel", "parallel")),
        name="flash_attn",
    )(q, k, v)


def _diff_kernel(lam_init, q_ref, k_ref, v_ref, lp_ref, sub_ref, o_ref, acc_ref, m_ref, qv_ref):
    n_heads, tq, _ = q_ref.shape
    seq = k_ref.shape[1]
    tile = pl.program_id(1)
    t0 = tile * tq
    lp = lp_ref[...]
    lam = (jnp.exp(jnp.sum(lp[0:1] * lp[1:2], keepdims=True))
           - jnp.exp(jnp.sum(lp[2:3] * lp[3:4], keepdims=True)) + lam_init)
    lane = _lane((tq, LANES))
    tpos = (t0 + lax.broadcasted_iota(jnp.int32, (tq, LANES), 0)).astype(F32)
    in_f = (lane >= _FEAT) & (lane < _FEAT + 3)
    for h in range(n_heads):
        q = q_ref[h].astype(F32)
        g1, g2, g3 = _split3(2.0 * _diff_slope(h) * tpos)
        left = jnp.where(in_f, 1.0, 0.0)
        right = jnp.where(in_f, -1.0, jnp.where(lane == _FEAT + 3, g1, jnp.where(
            lane == _FEAT + 4, g2, jnp.where(lane == _FEAT + 5, g3, 0.0))))
        for m in range(2):
            qm = jnp.where((lane >= m * DIFF_D) & (lane < (m + 1) * DIFF_D), q, 0.0)
            qv_ref[0, 2 * h + m] = (qm + left).astype(BF16)
            qv_ref[1, 2 * h + m] = (qm + right).astype(BF16)
            qv_ref[2, 2 * h + m] = qm.astype(BF16)
    acc_ref[...] = jnp.zeros_like(acc_ref)
    m_ref[...] = jnp.full_like(m_ref, NEG)

    def chunk(side, ks, bias):
        scores = [_scores(qv_ref[side, i], k_ref[i // 2, pl.ds(ks, tq), :]) for i in range(2 * n_heads)]
        for i in range(2 * n_heads):
            s = scores[i] if bias is None else scores[i] - bias[i // 2]
            _softmax_update(s, v_ref[i // 2, pl.ds(ks, tq), :], acc_ref, m_ref, i)

    def side_body(side):
        def body(c, carry):
            chunk(side, pl.multiple_of(c * tq, tq), None)
            return carry
        return body

    lax.fori_loop(0, tile, side_body(0), 0)
    dist = jnp.abs(lax.broadcasted_iota(jnp.int32, (tq, tq), 0)
                   - lax.broadcasted_iota(jnp.int32, (tq, tq), 1)).astype(F32)
    trow = (t0 + lax.broadcasted_iota(jnp.int32, (tq, tq), 0)).astype(F32)
    chunk(2, pl.multiple_of(t0, tq), [_diff_slope(h) * (dist - trow) for h in range(n_heads)])
    lax.fori_loop(tile + 1, seq // tq, side_body(1), 0)

    outs = []
    for h in range(n_heads):
        o = _normalize(acc_ref[2 * h]) - lam * _normalize(acc_ref[2 * h + 1])
        o = jnp.where(lane < DIFF_V, o, 0.0)
        ms = jnp.sum(o * o, axis=-1, keepdims=True) * (1.0 / DIFF_V)
        outs.append(o * lax.rsqrt(ms + DIFF_NORM_EPS) * sub_ref[...] * (1.0 - lam_init))
    for s in range(n_heads // 2):
        o_ref[:, s * LANES:(s + 1) * LANES] = _join_pair(outs[2 * s], outs[2 * s + 1]).astype(o_ref.dtype)


def _diff_attention(q, k, v, lam_params, subln, lam_init, tq):
    batch, n_heads, seq, _ = q.shape
    return pl.pallas_call(
        functools.partial(_diff_kernel, lam_init),
        grid=(batch, seq // tq),
        in_specs=[
            pl.BlockSpec((None, n_heads, tq, LANES), lambda b, i: (b, 0, i, 0)),
            _resident((None, n_heads, seq, LANES), lambda b, i: (b, 0, 0, 0)),
            _resident((None, n_heads, seq, LANES), lambda b, i: (b, 0, 0, 0)),
            pl.BlockSpec((4, DIFF_D), lambda b, i: (0, 0)),
            pl.BlockSpec((1, LANES), lambda b, i: (0, 0)),
        ],
        out_specs=pl.BlockSpec((None, tq, n_heads * HALF), lambda b, i: (b, i, 0)),
        out_shape=jax.ShapeDtypeStruct((batch, seq, n_heads * HALF), BF16),
        scratch_shapes=[pltpu.VMEM((2 * n_heads, tq, LANES), F32)] * 2
        + [pltpu.VMEM((3, 2 * n_heads, tq, LANES), BF16)],
        compiler_params=_cparams(("parallel", "parallel")),
        name="diff_attn",
    )(q, k, v, lam_params, subln)


_DIL_BLOCK = 2048
_DIL_Q = 128


def _dil_kernel(group, dil, seq, q_ref, kp_ref, kc_ref, kn_ref, vp_ref, vc_ref, vn_ref,
                o_ref, lse_ref, kw_ref, vw_ref):
    tb = q_ref.shape[1]
    halo = DIL_GROUPS[group][0] // 2
    assert halo == kp_ref.shape[1] and halo == (_DIL_Q // 2) * dil
    for w_ref, p_ref, c_ref, n_ref in ((kw_ref, kp_ref, kc_ref, kn_ref), (vw_ref, vp_ref, vc_ref, vn_ref)):
        w_ref[:, 0:halo] = p_ref[...]
        w_ref[:, halo:halo + tb] = c_ref[...]
        w_ref[:, halo + tb:halo + tb + halo] = n_ref[...]
    t0 = pl.program_id(1) * tb
    shape = (_DIL_Q, 2 * _DIL_Q)
    rel = lax.broadcasted_iota(jnp.int32, shape, 1) - lax.broadcasted_iota(jnp.int32, shape, 0) - _DIL_Q // 2
    absrel = jnp.abs(rel).astype(F32)
    band = jnp.where(absrel <= _DIL_Q // 2, 0.0, -NEG)
    n_slopes = len(DIL_GROUPS) * DIL_HEADS
    bias = [2.0 ** (-8.0 * (group * DIL_HEADS + h + 1) / n_slopes) * dil * LOG2E * absrel + band
            for h in range(DIL_HEADS)]
    col = lax.broadcasted_iota(jnp.int32, (1, 2 * _DIL_Q), 1)
    lo_mask = _lane((_DIL_Q, LANES)) < HALF
    for unit in range(tb // _DIL_Q):
        first = (unit // dil) * _DIL_Q * dil + unit % dil
        rows = pl.ds(first, _DIL_Q, stride=dil)
        win = pl.ds(first, 2 * _DIL_Q, stride=dil)
        key_token = t0 + (first - halo) + dil * col
        off_seq = jnp.where(key_token >= 0, jnp.where(key_token < seq, 0.0, NEG), NEG)
        for s in range(2):
            qs = q_ref[s, rows, :].astype(BF16)
            ksl = kw_ref[s, win, :].astype(BF16)
            vsl = vw_ref[s, win, :].astype(BF16)
            res = []
            for part in range(2):
                keep = lo_mask if part == 0 else jnp.logical_not(lo_mask)
                sc = _scores(jnp.where(keep, qs, jnp.zeros_like(qs)), ksl) - bias[2 * s + part] + off_seq
                m = jnp.max(sc, axis=-1, keepdims=True)
                p = jnp.exp2(sc - m)
                l = jnp.sum(p, axis=-1, keepdims=True)
                o = jnp.dot(p.astype(BF16), vsl, preferred_element_type=F32) / l
                res.append((o, m + jnp.log2(l)))
            o_ref[s, rows, :] = jnp.where(lo_mask, res[0][0], res[1][0])
            lse_ref[s, rows, :] = jnp.where(lo_mask, res[0][1], res[1][1])


def _dilated_group(q, k, v, group):
    batch, _, seq, _ = q.shape
    dil = DIL_GROUPS[group][1]
    halo = DIL_GROUPS[group][0] // 2
    tb = _DIL_BLOCK
    assert seq % tb == 0 and seq // dil >= 2 * _DIL_Q and tb % (_DIL_Q * dil) == 0
    per = tb // halo
    cur = pl.BlockSpec((None, 2, tb, LANES), lambda b, i: (b, 0, i, 0))
    prev = pl.BlockSpec((None, 2, halo, LANES), lambda b, i: (b, 0, jnp.maximum(i * per - 1, 0), 0))
    nxt = pl.BlockSpec((None, 2, halo, LANES),
                       lambda b, i: (b, 0, jnp.minimum((i + 1) * per, seq // halo - 1), 0))
    out_shape = jax.ShapeDtypeStruct((batch, 2, seq, LANES), F32)
    o, lse = pl.pallas_call(
        functools.partial(_dil_kernel, group, dil, seq),
        grid=(batch, seq // tb),
        in_specs=[cur, prev, cur, nxt, prev, cur, nxt],
        out_specs=[cur, cur],
        out_shape=[out_shape, out_shape],
        scratch_shapes=[pltpu.VMEM((2, tb + 2 * halo, LANES), F32)] * 2,
        compiler_params=_cparams(("parallel", "parallel")),
        name=f"dilated_g{group}",
    )(q, k, k, k, v, v, v)
    return o, lse


def _mix_kernel(h_ref, ya_ref, yc_ref, yd_ref, o0_ref, o1_ref, o2_ref, l0_ref, l1_ref, l2_ref,
                wg_ref, bg_ref, wb_ref, wo_ref, lng_ref, lnb_ref, out_ref, *maybe_bf16_ref):
    h = h_ref[...]
    hb = h.astype(BF16)
    slabs = []
    for s in range(2):
        l0, l1, l2 = l0_ref[s], l1_ref[s], l2_ref[s]
        mx = jnp.maximum(jnp.maximum(l0, l1), l2)
        e0, e1, e2 = jnp.exp2(l0 - mx), jnp.exp2(l1 - mx), jnp.exp2(l2 - mx)
        slabs.append(((e0 * o0_ref[s] + e1 * o1_ref[s] + e2 * o2_ref[s]) / (e0 + e1 + e2)).astype(BF16))
    ys = (ya_ref[...], jnp.concatenate(slabs, axis=1), yc_ref[...], yd_ref[...])
    acc = None
    for n in range(N_BRANCH):
        logit = jnp.dot(hb, wg_ref[:, n * D_MODEL:(n + 1) * D_MODEL],
                        preferred_element_type=F32) + bg_ref[n:n + 1, :]
        gate = 1.0 / (1.0 + jnp.exp(-logit))
        term = gate * jnp.dot(ys[n], wb_ref[n], preferred_element_type=F32)
        acc = term if acc is None else acc + term
    m = jnp.dot(acc.astype(BF16), wo_ref[...], preferred_element_type=F32)
    out = _layernorm(DEEPNORM_ALPHA * h + m, lng_ref[...], lnb_ref[...])
    out_ref[...] = out
    for r in maybe_bf16_ref:
        r[...] = out.astype(BF16)


def _mix(h, ya, yc, yd, dil_o, dil_lse, wg, bg, wb, wo, lng, lnb, tm, with_bf16_copy):
    T = h.shape[0]
    nst = dil_o[0].shape[2] // tm
    slab = pl.BlockSpec((None, 2, tm, LANES), lambda i: (i // nst, 0, i % nst, 0))
    row = lambda i: (i, 0)
    const = lambda i: (0, 0)
    wide = pl.BlockSpec((tm, D_MODEL), row)
    narrow = pl.BlockSpec((tm, BRANCH_W), row)
    return pl.pallas_call(
        _mix_kernel,
        grid=(T // tm,),
        in_specs=[wide] + [narrow] * 3 + [slab] * 6 + [
            _resident((D_MODEL, N_BRANCH * D_MODEL), const),
            pl.BlockSpec((N_BRANCH, D_MODEL), const),
            _resident((N_BRANCH, BRANCH_W, D_MODEL), lambda i: (0, 0, 0)),
            _resident((D_MODEL, D_MODEL), const),
            pl.BlockSpec((1, D_MODEL), const),
            pl.BlockSpec((1, D_MODEL), const),
        ],
        out_specs=[wide] * (2 if with_bf16_copy else 1),
        out_shape=[jax.ShapeDtypeStruct((T, D_MODEL), F32)]
        + ([jax.ShapeDtypeStruct((T, D_MODEL), BF16)] if with_bf16_copy else []),
        compiler_params=_cparams(("parallel",)),
        name="gate_mix",
    )(h, ya, yc, yd, *dil_o, *dil_lse, wg, bg, wb, wo, lng, lnb)


def _swiglu_step(xb_ref, w1_ref, w3_ref, w2_ref, acc_ref):
    @pl.when(pl.program_id(1) == 0)
    def _():
        acc_ref[...] = jnp.zeros_like(acc_ref)

    xb = xb_ref[...]
    tf = w1_ref.shape[1]
    mid = (tf // LANES + 1) // 2 * LANES
    total = None
    for lo, hi in ((0, mid), (mid, tf)):
        a = jnp.dot(xb, w1_ref[:, lo:hi].astype(BF16), preferred_element_type=F32)
        b = jnp.dot(xb, w3_ref[:, lo:hi].astype(BF16), preferred_element_type=F32)
        g = (a / (1.0 + jnp.exp(-a)) * b).astype(BF16)
        part = jnp.dot(g, w2_ref[lo:hi, :].astype(BF16), preferred_element_type=F32)
        total = part if total is None else total + part
    acc_ref[...] += total


def _ffn_dense_kernel(x_ref, w1_ref, w3_ref, w2_ref, lng_ref, lnb_ref, o_ref, acc_ref, xb_ref):
    @pl.when(pl.program_id(1) == 0)
    def _():
        xb_ref[...] = x_ref[...].astype(BF16)

    _swiglu_step(xb_ref, w1_ref, w3_ref, w2_ref, acc_ref)

    @pl.when(pl.program_id(1) == pl.num_programs(1) - 1)
    def _():
        o_ref[...] = _layernorm(DEEPNORM_ALPHA * x_ref[...] + acc_ref[...], lng_ref[...], lnb_ref[...])


def _ffn_dense(x, w13, w2, lng, lnb, tm, tf):
    T = x.shape[0]
    nf = D_FF // tf
    return pl.pallas_call(
        _ffn_dense_kernel,
        grid=(T // tm, nf),
        in_specs=[
            pl.BlockSpec((tm, D_MODEL), lambda i, j: (i, 0)),
            pl.BlockSpec((D_MODEL, tf), lambda i, j: (0, j)),
            pl.BlockSpec((D_MODEL, tf), lambda i, j: (0, nf + j)),
            pl.BlockSpec((tf, D_MODEL), lambda i, j: (j, 0)),
            pl.BlockSpec((1, D_MODEL), lambda i, j: (0, 0)),
            pl.BlockSpec((1, D_MODEL), lambda i, j: (0, 0)),
        ],
        out_specs=pl.BlockSpec((tm, D_MODEL), lambda i, j: (i, 0)),
        out_shape=jax.ShapeDtypeStruct((T, D_MODEL), F32),
        scratch_shapes=[pltpu.VMEM((tm, D_MODEL), F32), pltpu.VMEM((tm, D_MODEL), BF16)],
        compiler_params=_cparams(("parallel", "arbitrary")),
        name="ffn_dense",
    )(x, w13, w13, w2, lng, lnb)


def _ffn_expert_kernel(te_ref, used_ref, x_ref, w1_ref, w3_ref, w2_ref, o_ref, acc_ref):
    del te_ref
    live = pl.program_id(0) < used_ref[0]
    last = pl.program_id(1) == pl.num_programs(1) - 1

    @pl.when(live)
    def _():
        _swiglu_step(x_ref, w1_ref, w3_ref, w2_ref, acc_ref)

    @pl.when(live & last)
    def _():
        o_ref[...] = acc_ref[...].astype(o_ref.dtype)

    @pl.when(jnp.logical_not(live) & last)
    def _():
        o_ref[...] = jnp.zeros_like(o_ref)


def _ffn_experts(tile_expert, tiles_used, x_sorted, w13, w2, tm, tf):
    rows = x_sorted.shape[0]
    nf = D_FF // tf
    fblk = lambda i, j, nu: jnp.where(i < nu[0], j, nf - 1)
    grid_spec = pltpu.PrefetchScalarGridSpec(
        num_scalar_prefetch=2,
        grid=(rows // tm, nf),
        in_specs=[
            pl.BlockSpec((tm, D_MODEL), lambda i, j, te, nu: (i, 0)),
            pl.BlockSpec((None, D_MODEL, tf), lambda i, j, te, nu: (te[i], 0, fblk(i, j, nu))),
            pl.BlockSpec((None, D_MODEL, tf), lambda i, j, te, nu: (te[i], 0, nf + fblk(i, j, nu))),
            pl.BlockSpec((None, tf, D_MODEL), lambda i, j, te, nu: (te[i], fblk(i, j, nu), 0)),
        ],
        out_specs=pl.BlockSpec((tm, D_MODEL), lambda i, j, te, nu: (i, 0)),
        scratch_shapes=[pltpu.VMEM((tm, D_MODEL), F32)],
    )
    return pl.pallas_call(
        _ffn_expert_kernel,
        grid_spec=grid_spec,
        out_shape=jax.ShapeDtypeStruct((rows, D_MODEL), BF16),
        compiler_params=_cparams(("parallel", "arbitrary")),
        name="ffn_experts",
    )(tile_expert, tiles_used, x_sorted, w13, w13, w2)


def _router_kernel(x_ref, wr_ref, idx_ref, wt_ref):
    logits = jnp.dot(x_ref[...], wr_ref[...], preferred_element_type=F32,
                     precision=lax.Precision.HIGHEST)
    lane = _lane(logits.shape)
    logits = jnp.where(lane < N_EXPERTS, logits, -jnp.inf)
    v1 = jnp.max(logits, axis=-1, keepdims=True)
    i1 = jnp.min(jnp.where(logits == v1, lane, LANES), axis=-1, keepdims=True)
    rest = jnp.where(lane == i1, -jnp.inf, logits)
    v2 = jnp.max(rest, axis=-1, keepdims=True)
    i2 = jnp.min(jnp.where(rest == v2, lane, LANES), axis=-1, keepdims=True)
    e2 = jnp.exp(v2 - v1)
    w1 = 1.0 / (1.0 + e2)
    idx_ref[...] = jnp.where(lane == 0, i1, jnp.where(lane == 1, i2, 0))
    wt_ref[...] = jnp.where(lane == 0, w1, jnp.where(lane == 1, e2 * w1, 0.0))


def _router(x, w_router, tm):
    T = x.shape[0]
    wr = jnp.pad(w_router, ((0, 0), (0, LANES - N_EXPERTS)))
    row = lambda i: (i, 0)
    return pl.pallas_call(
        _router_kernel,
        grid=(T // tm,),
        in_specs=[pl.BlockSpec((tm, D_MODEL), row), pl.BlockSpec((D_MODEL, LANES), lambda i: (0, 0))],
        out_specs=[pl.BlockSpec((tm, LANES), row), pl.BlockSpec((tm, LANES), row)],
        out_shape=[jax.ShapeDtypeStruct((T, LANES), jnp.int32), jax.ShapeDtypeStruct((T, LANES), F32)],
        compiler_params=_cparams(("parallel",)),
        name="router",
    )(x, wr)


def _moe_out_kernel(h_ref, y0_ref, y1_ref, wt_ref, lng_ref, lnb_ref, o_ref):
    wt = wt_ref[...]
    f = wt[:, 0:1] * y0_ref[...] + wt[:, 1:2] * y1_ref[...]
    o_ref[...] = _layernorm(DEEPNORM_ALPHA * h_ref[...] + f, lng_ref[...], lnb_ref[...])


def _moe_out(h, y0, y1, wt, lng, lnb, tm):
    T = h.shape[0]
    row = lambda i: (i, 0)
    const = lambda i: (0, 0)
    wide = pl.BlockSpec((tm, D_MODEL), row)
    return pl.pallas_call(
        _moe_out_kernel,
        grid=(T // tm,),
        in_specs=[wide, wide, wide, pl.BlockSpec((tm, LANES), row),
                  pl.BlockSpec((1, D_MODEL), const), pl.BlockSpec((1, D_MODEL), const)],
        out_specs=wide,
        out_shape=jax.ShapeDtypeStruct((T, D_MODEL), F32),
        compiler_params=_cparams(("parallel",)),
        name="moe_out",
    )(h, y0, y1, wt, lng, lnb)


def _moe(h, h_bf16, w_router, w13, w2, lng, lnb, tm_router, tm_e, tf):
    T = h.shape[0]
    idx, wt = _router(h, w_router, tm_router)
    e_flat = jnp.concatenate([idx[:, s] for s in range(TOP_K)])
    onehot = (e_flat[None, :] == jnp.arange(N_EXPERTS, dtype=jnp.int32)[:, None]).astype(jnp.int32)
    counts = jnp.sum(onehot, axis=1)
    rank = jnp.sum((jnp.cumsum(onehot, axis=1) - onehot) * onehot, axis=0)
    padded = (counts + tm_e - 1) // tm_e * tm_e
    ends = jnp.cumsum(padded)
    pos = jnp.sum(onehot * (ends - padded)[:, None], axis=0) + rank
    rows = TOP_K * T + N_EXPERTS * tm_e
    token = jnp.arange(TOP_K * T, dtype=jnp.int32) % T
    src = (jnp.arange(rows, dtype=jnp.int32) % T).at[pos].set(
        token, unique_indices=True, mode="promise_in_bounds")
    tile_start = jnp.arange(rows // tm_e, dtype=jnp.int32) * tm_e
    tile_expert = jnp.minimum(jnp.sum((tile_start[:, None] >= ends[None, :]).astype(jnp.int32), axis=1),
                              N_EXPERTS - 1)
    x_sorted = h_bf16.at[src].get(mode="promise_in_bounds")
    tiles_used = (ends[N_EXPERTS - 1:] // tm_e).astype(jnp.int32)
    y = _ffn_experts(tile_expert, tiles_used, x_sorted, w13, w2, tm_e, tf)
    y0 = y.at[pos[:T]].get(mode="promise_in_bounds")
    y1 = y.at[pos[T:]].get(mode="promise_in_bounds")
    return _moe_out(h, y0, y1, wt, lng, lnb, tm_router)


def _rot_cols(w, block):
    d, n = w.shape
    w4 = w.reshape(d, n // block, 2, block // 2)
    return jnp.concatenate([-w4[:, :, 1], w4[:, :, 0]], axis=2).reshape(d, n)


def _attention_weights(w_in):
    o = np.cumsum([0, MLA_Q_LORA, MLA_KV_LORA, MLA_ROPE, 2304, 256, 256, 512 + 256, N_BRANCH * D_MODEL])
    cq, ckv, kr, dil, gq, gkv, diff, gate = (w_in[:, o[i]:o[i + 1]] for i in range(8))
    z = lambda n: jnp.zeros((D_MODEL, n), w_in.dtype)
    mla = [cq, z(64), ckv, z(64), kr, z(32), z(64), _rot_cols(kr, MLA_ROPE), z(32)]
    gqa = [gq, _rot_cols(gq, 32), gkv[:, :128], _rot_cols(gkv[:, :128], 32), gkv[:, 128:]]
    small = jnp.concatenate(mla + gqa, axis=1)
    return [w.astype(BF16) for w in (small, diff, dil)], gate.astype(BF16)


def _mla_up_weights(w_uq, w_ukv):
    zq = lambda n: jnp.zeros((MLA_Q_LORA, n), w_uq.dtype)
    per = MLA_NOPE + MLA_ROPE
    plain, rot = [], []
    for h in range(MLA_HEADS):
        nope = w_uq[:, h * per:h * per + MLA_NOPE]
        rope = w_uq[:, h * per + MLA_NOPE:(h + 1) * per]
        plain += [nope, rope, zq(32)]
        rot += [zq(64), _rot_cols(rope, MLA_ROPE), zq(32)]
    wuq = jnp.concatenate(plain + rot, axis=1)
    wuq = jnp.pad(wuq, ((0, 256 - MLA_Q_LORA), (0, 0)))
    zk = jnp.zeros((MLA_KV_LORA, 64), w_ukv.dtype)
    per = MLA_NOPE + MLA_V
    ks, vs = [], []
    for h in range(MLA_HEADS):
        ks += [w_ukv[:, h * per:h * per + MLA_NOPE], zk]
        vs += [w_ukv[:, h * per + MLA_NOPE:(h + 1) * per], zk]
    wukv = jnp.concatenate(ks + vs, axis=1)
    return wuq.astype(BF16), wukv.astype(BF16)


def _rope_angles(pos, dim):
    inv = ROPE_THETA ** (-(jnp.arange(0, dim, 2, dtype=F32) / dim))
    return pos[:, None] * inv[None, :]


def _position_tables(seq):
    rows = seq // GRID_W
    row_idx = jnp.repeat(jnp.arange(rows, dtype=F32), GRID_W)
    col_idx = jnp.tile(jnp.arange(GRID_W, dtype=F32), rows)
    a1 = _rope_angles(jnp.arange(seq, dtype=F32), MLA_ROPE)
    ar = _rope_angles(row_idx, HEAD_DIM // 2)
    ac = _rope_angles(col_idx, HEAD_DIM // 2)
    z = lambda n: jnp.zeros((seq, n), F32)
    c1 = jnp.concatenate([jnp.ones((seq, MLA_NOPE), F32)] + [jnp.cos(a1)] * 2 + [z(32)], axis=1)
    s1 = jnp.concatenate([z(MLA_NOPE)] + [jnp.sin(a1)] * 2 + [z(32)], axis=1)
    ca = jnp.concatenate(([jnp.cos(ar)] * 2 + [jnp.cos(ac)] * 2) * 2, axis=1)
    sa = jnp.concatenate(([jnp.sin(ar)] * 2 + [jnp.sin(ac)] * 2) * 2, axis=1)
    return [c1, s1, ca, sa]


def _gqa_gain_rows(q_norm, k_norm):
    def swap(g):
        return jnp.concatenate([g[16:32], g[0:16], g[48:64], g[32:48]])

    sg = HEAD_DIM ** -0.5 * LOG2E
    rows = [sg * q_norm, sg * swap(q_norm), k_norm, swap(k_norm)]
    return jnp.stack([jnp.concatenate([r, r]) for r in rows])


def _pick(n, pref):
    t = min(n, pref)
    assert n % t == 0, (n, pref)
    return t


def kernel(x, ln_emb_g, ln_emb_b, w_in, b_gate, mla_q_norm, mla_kv_norm, mla_w_uq, mla_w_ukv,
           gqa_q_norm, gqa_k_norm, diff_lambda, diff_subln, w_branch, w_out, ln1_g, ln1_b,
           ffn_w13, ffn_w2, moe_router, moe_w13, moe_w2, ln2_g, ln2_b):
    batch, seq, _ = x.shape
    T = batch * seq
    tm = _pick(seq, 512)
    tq = _pick(seq, 512)
    tk = _pick(seq, 1024)
    row = lambda v: v.reshape(1, -1)
    pos_tabs = _position_tables(seq)

    h = x.reshape(T, D_MODEL)
    for l in range(DEPTH):
        lam_init = 0.8 - 0.6 * float(np.exp(-0.3 * l))
        weights, wg = _attention_weights(w_in[l])
        wuq, wukv = _mla_up_weights(mla_w_uq[l], mla_w_ukv[l])
        gains = _gqa_gain_rows(gqa_q_norm[l], gqa_k_norm[l])
        nq = jnp.pad(mla_q_norm[l], (0, 256 - MLA_Q_LORA)).reshape(1, 256)
        res = _project(h, batch, seq, l == 0, row(ln_emb_g), row(ln_emb_b), weights, wuq, wukv,
                       nq, row(mla_kv_norm[l]), gains, pos_tabs, tm)
        if l == 0:
            h, res = res[0], res[1:]
        mq, mk, mv, gq, gk, gv, dq, dk, dv = res[:9]
        dil = res[9:]
        ya = _flash(mq, mk, mv, tq, tk).reshape(T, BRANCH_W)
        yc = _flash(gq, gk, gv, tq, _pick(seq, 2 * tk)).reshape(T, BRANCH_W)
        sub = jnp.pad(diff_subln[l], (0, LANES - DIFF_V)).reshape(1, LANES)
        yd = _diff_attention(dq, dk, dv, diff_lambda[l], sub, lam_init, tq).reshape(T, BRANCH_W)
        dil_o, dil_lse = [], []
        for g in range(len(DIL_GROUPS)):
            o, lse = _dilated_group(dil[g], dil[3 + g], dil[6 + g], g)
            dil_o.append(o)
            dil_lse.append(lse)
        dense = l % 2 == 0
        mixed = _mix(h, ya, yc, yd, dil_o, dil_lse, wg, b_gate[l], w_branch[l].astype(BF16),
                     w_out[l].astype(BF16), row(ln1_g[l]), row(ln1_b[l]), tm, not dense)
        if dense:
            h = _ffn_dense(mixed[0], ffn_w13[l // 2], ffn_w2[l // 2],
                           row(ln2_g[l]), row(ln2_b[l]), _pick(T, 1024), 896)
        else:
            h = _moe(mixed[0], mixed[1], moe_router[l // 2], moe_w13[l // 2], moe_w2[l // 2],
                     row(ln2_g[l]), row(ln2_b[l]), tm, _pick(T, 1024), 896)
    return h.reshape(batch, seq, D_MODEL)
```

```python
import functools

import jax
import jax.numpy as jnp
import numpy as np
from jax import lax
from jax.experimental import pallas as pl
from jax.experimental.pallas import tpu as pltpu

D_MODEL = 1024
GRID_W = 64
HEAD_DIM = 64
N_BRANCH = 4
BRANCH_W = 256
MLA_HEADS = 4
MLA_NOPE = 64
MLA_ROPE = 32
MLA_V = 64
MLA_Q_LORA = 192
MLA_KV_LORA = 128
DIL_GROUPS = ((128, 1), (512, 4), (2048, 16))
DIL_HEADS = 4
GQA_Q_HEADS = 4
GQA_KV_HEADS = 2
DIFF_HEADS = 4
DIFF_D = 32
DIFF_V = 64
D_FF = 3584
N_EXPERTS = 8
TOP_K = 2
ROPE_THETA = 10000.0
LN_EPS = 1e-5
RMS_EPS = 1e-6
DIFF_NORM_EPS = 1e-5
DEPTH = 2
DEEPNORM_ALPHA = (2 * DEPTH) ** 0.25

LANES = 128
HALF = LANES // 2
VMEM_LIMIT = 56 * 1024 * 1024

LOG2E = 1.4426950408889634
NEG = -1e30
BF16 = jnp.bfloat16
F32 = jnp.float32

_MLA_COLS = 640
_GQA_COLS = 896
_DIFF_COLS = 768
_DIL_COLS = 2304
_SMALL_COLS = _MLA_COLS + _GQA_COLS


def _cparams(sem):
    return pltpu.CompilerParams(dimension_semantics=sem, vmem_limit_bytes=VMEM_LIMIT)


def _resident(shape, index_map):
    return pl.BlockSpec(shape, index_map, pipeline_mode=pl.Buffered(1))


def _layernorm(x, g, b):
    mu = jnp.mean(x, axis=-1, keepdims=True)
    xc = x - mu
    var = jnp.mean(xc * xc, axis=-1, keepdims=True)
    return xc * lax.rsqrt(var + LN_EPS) * g + b


def _lane(shape):
    return lax.broadcasted_iota(jnp.int32, shape, len(shape) - 1)


def _split_pair(x):
    lo_mask = _lane(x.shape) < HALF
    lo = jnp.where(lo_mask, x, 0.0)
    hi = jnp.where(lo_mask, pltpu.roll(x, HALF, axis=1), 0.0)
    return lo, hi


def _join_pair(lo, hi):
    return jnp.where(_lane(lo.shape) < HALF, lo, pltpu.roll(hi, HALF, axis=1))


def _with_ones(v):
    return jnp.where(_lane(v.shape) == HALF, 1.0, v)


def _diff_slope(h):
    return 2.0 ** (-8.0 * (h + 1) / DIFF_HEADS) * LOG2E


def _split3(val):
    f1 = val.astype(BF16).astype(F32)
    r1 = val - f1
    f2 = r1.astype(BF16).astype(F32)
    f3 = (r1 - f2).astype(BF16).astype(F32)
    return f1, f2, f3


_FEAT = 2 * DIFF_D


def _key_bias_features(val):
    f1, f2, f3 = _split3(val)
    lane = _lane(val.shape)
    ones = jnp.where((lane >= _FEAT + 3) & (lane < _FEAT + 6), 1.0, 0.0)
    return jnp.where(lane == _FEAT, f1, jnp.where(lane == _FEAT + 1, f2, jnp.where(lane == _FEAT + 2, f3, ones)))


def _proj_kernel(apply_ln, tiles_per_seq, *refs):
    (x_ref, lng_ref, lnb_ref, ws_ref, wdiff_ref, wdil_ref, wuq_ref, wukv_ref, nq_ref, nkv_ref,
     gains_ref, c1_ref, s1_ref, ca_ref, sa_ref) = refs[:15]
    outs = refs[15:]
    if apply_ln:
        h_ref, outs = outs[0], outs[1:]
    (mq_ref, mk_ref, mv_ref, gq_ref, gk_ref, gv_ref, dq_ref, dk_ref, dv_ref) = outs[:9]
    dil_refs = outs[9:]

    x = x_ref[...]
    if apply_ln:
        x = _layernorm(x, lng_ref[...], lnb_ref[...])
        h_ref[...] = x
    hb = x.astype(BF16)

    def proj(w_ref, off, n):
        return jnp.dot(hb, w_ref[:, off:off + n], preferred_element_type=F32)

    p = proj(ws_ref, 0, _MLA_COLS)
    cq, ckv, kr, krr = p[:, 0:256], p[:, 256:384], p[:, 384:512], p[:, 512:640]
    rq = lax.rsqrt(jnp.sum(cq * cq, axis=-1, keepdims=True) * (1.0 / MLA_Q_LORA) + RMS_EPS)
    cqn = (cq * rq * nq_ref[...]).astype(BF16)
    q2 = jnp.dot(cqn, wuq_ref[...], preferred_element_type=F32)
    rkv = lax.rsqrt(jnp.sum(ckv * ckv, axis=-1, keepdims=True) * (1.0 / MLA_KV_LORA) + RMS_EPS)
    ckvn = (ckv * rkv * nkv_ref[...]).astype(BF16)
    kv2 = jnp.dot(ckvn, wukv_ref[...], preferred_element_type=F32)
    c1, s1 = c1_ref[...], s1_ref[...]
    k_rope = kr * c1 + krr * s1
    mla_scale = (MLA_NOPE + MLA_ROPE) ** -0.5 * LOG2E
    cq_t, sq_t = c1 * mla_scale, s1 * mla_scale
    for h in range(MLA_HEADS):
        qh = q2[:, h * LANES:(h + 1) * LANES]
        qrh = q2[:, (MLA_HEADS + h) * LANES:(MLA_HEADS + h + 1) * LANES]
        mq_ref[h] = (qh * cq_t + qrh * sq_t).astype(BF16)
        mk_ref[h] = (kv2[:, h * LANES:(h + 1) * LANES] + k_rope).astype(BF16)
        vh = kv2[:, (MLA_HEADS + h) * LANES:(MLA_HEADS + h + 1) * LANES]
        mv_ref[h] = _with_ones(vh).astype(BF16)

    p = proj(ws_ref, _MLA_COLS, _GQA_COLS)

    def head_rms(xs):
        ss = xs * xs
        lo_mask = _lane(xs.shape) < HALF
        s_all = jnp.sum(ss, axis=-1, keepdims=True)
        s_lo = jnp.sum(jnp.where(lo_mask, ss, 0.0), axis=-1, keepdims=True)
        return lax.rsqrt(jnp.where(lo_mask, s_lo, s_all - s_lo) * (1.0 / HEAD_DIM) + RMS_EPS)

    ca, sa = ca_ref[...], sa_ref[...]
    aq, bq = ca * gains_ref[0:1, :], sa * gains_ref[1:2, :]
    ak, bk = ca * gains_ref[2:3, :], sa * gains_ref[3:4, :]
    for s in range(2):
        qs = p[:, s * LANES:(s + 1) * LANES]
        qrs = p[:, 256 + s * LANES:256 + (s + 1) * LANES]
        qn = head_rms(qs) * (qs * aq + qrs * bq)
        lo, hi = _split_pair(qn)
        gq_ref[2 * s] = lo.astype(BF16)
        gq_ref[2 * s + 1] = hi.astype(BF16)
    ks, krs = p[:, 512:640], p[:, 640:768]
    kn = head_rms(ks) * (ks * ak + krs * bk)
    lo, hi = _split_pair(kn)
    gk_ref[0] = lo.astype(BF16)
    gk_ref[1] = hi.astype(BF16)
    lo, hi = _split_pair(p[:, 768:896])
    gv_ref[0] = _with_ones(lo).astype(BF16)
    gv_ref[1] = _with_ones(hi).astype(BF16)

    p = proj(wdiff_ref, 0, _DIFF_COLS)
    dscale = DIFF_D ** -0.5 * LOG2E
    tm = x.shape[0]
    pos = (pl.program_id(0) % tiles_per_seq * tm
           + lax.broadcasted_iota(jnp.int32, (tm, LANES), 0)).astype(F32)
    for s in range(2):
        lo, hi = _split_pair(p[:, s * LANES:(s + 1) * LANES] * dscale)
        dq_ref[2 * s] = lo.astype(BF16)
        dq_ref[2 * s + 1] = hi.astype(BF16)
        lo, hi = _split_pair(p[:, 256 + s * LANES:256 + (s + 1) * LANES])
        dk_ref[2 * s] = (lo + _key_bias_features(_diff_slope(2 * s) * pos)).astype(BF16)
        dk_ref[2 * s + 1] = (hi + _key_bias_features(_diff_slope(2 * s + 1) * pos)).astype(BF16)
        lo, hi = _split_pair(p[:, 512 + s * LANES:512 + (s + 1) * LANES])
        dv_ref[2 * s] = _with_ones(lo).astype(BF16)
        dv_ref[2 * s + 1] = _with_ones(hi).astype(BF16)

    lscale = HEAD_DIM ** -0.5 * LOG2E
    for i in range(9):
        seg = proj(wdil_ref, i * BRANCH_W, BRANCH_W)
        if i < 3:
            seg = seg * lscale
        dil_refs[i][0] = seg[:, :LANES]
        dil_refs[i][1] = seg[:, LANES:]


def _project(x2d, batch, seq, apply_ln, lng, lnb, weights, wuq, wukv, nq, nkv, gains, tables, tm):
    T = x2d.shape[0]
    nst = seq // tm
    grid = (T // tm,)
    row = lambda i: (i, 0)
    const = lambda i: (0, 0)
    tab = lambda i: (i % nst, 0)
    head = lambda i: (i // nst, 0, i % nst, 0)

    in_specs = [
        pl.BlockSpec((tm, D_MODEL), row),
        pl.BlockSpec((1, D_MODEL), const),
        pl.BlockSpec((1, D_MODEL), const),
        _resident((D_MODEL, _SMALL_COLS), const),
        _resident((D_MODEL, _DIFF_COLS), const),
        _resident((D_MODEL, _DIL_COLS), const),
        _resident((256, 8 * LANES), const),
        _resident((LANES, 8 * LANES), const),
        pl.BlockSpec((1, 256), const),
        pl.BlockSpec((1, LANES), const),
        pl.BlockSpec((4, LANES), const),
    ] + [pl.BlockSpec((tm, LANES), tab)] * 4

    def heads_out(n):
        return (jax.ShapeDtypeStruct((batch, n, seq, LANES), BF16),
                pl.BlockSpec((None, n, tm, LANES), head))

    outs = []
    if apply_ln:
        outs.append((jax.ShapeDtypeStruct((T, D_MODEL), F32), pl.BlockSpec((tm, D_MODEL), row)))
    outs += [heads_out(MLA_HEADS)] * 3
    outs += [heads_out(GQA_Q_HEADS), heads_out(GQA_KV_HEADS), heads_out(GQA_KV_HEADS)]
    outs += [heads_out(DIFF_HEADS)] * 3
    outs += [(jax.ShapeDtypeStruct((batch, 2, seq, LANES), F32), pl.BlockSpec((None, 2, tm, LANES), head))] * 9

    res = pl.pallas_call(
        functools.partial(_proj_kernel, apply_ln, nst),
        grid=grid,
        in_specs=in_specs,
        out_specs=[o[1] for o in outs],
        out_shape=[o[0] for o in outs],
        compiler_params=_cparams(("parallel",)),
        name="proj_prep",
    )(x2d, lng, lnb, *weights, wuq, wukv, nq, nkv, gains, *tables)
    return res


def _normalize(acc):
    return acc / acc[:, HALF:HALF + 1]


def _scores(q, k):
    return lax.dot_general(q, k, (((1,), (1,)), ((), ())), preferred_element_type=F32)


def _softmax_update(s, v, acc_ref, m_ref, idx):
    blocks = [s[:, j * LANES:(j + 1) * LANES] for j in range(s.shape[1] // LANES)]
    part = functools.reduce(jnp.maximum, blocks)
    m_prev = m_ref[idx]
    m_new = jnp.maximum(m_prev, jnp.max(part, axis=-1, keepdims=True))
    p = jnp.concatenate([jnp.exp2((b - m_new).astype(BF16)) for b in blocks], axis=1)
    acc_ref[idx] = (jnp.exp2(m_prev - m_new) * acc_ref[idx]
                    + jnp.dot(p, v, preferred_element_type=F32))
    m_ref[idx] = m_new


def _flash_kernel(n_rep, tk, q_ref, k_ref, v_ref, o_ref, acc_ref, m_ref):
    n_heads = q_ref.shape[0]
    seq = k_ref.shape[1]
    acc_ref[...] = jnp.zeros_like(acc_ref)
    m_ref[...] = jnp.full_like(m_ref, NEG)

    def body(c, carry):
        ks = pl.multiple_of(c * tk, tk)
        scores = [_scores(q_ref[h], k_ref[h // n_rep, pl.ds(ks, tk), :]) for h in range(n_heads)]
        for h in range(n_heads):
            _softmax_update(scores[h], v_ref[h // n_rep, pl.ds(ks, tk), :], acc_ref, m_ref, h)
        return carry

    lax.fori_loop(0, seq // tk, body, 0)
    outs = [_normalize(acc_ref[h]) for h in range(n_heads)]
    for s in range(n_heads // 2):
        o_ref[:, s * LANES:(s + 1) * LANES] = _join_pair(outs[2 * s], outs[2 * s + 1]).astype(o_ref.dtype)


def _flash(q, k, v, tq, tk):
    batch, n_heads, seq, _ = q.shape
    n_kv = k.shape[1]
    return pl.pallas_call(
        functools.partial(_flash_kernel, n_heads // n_kv, tk),
        grid=(batch, seq // tq),
        in_specs=[
            pl.BlockSpec((None, n_heads, tq, LANES), lambda b, i: (b, 0, i, 0)),
            _resident((None, n_kv, seq, LANES), lambda b, i: (b, 0, 0, 0)),
            _resident((None, n_kv, seq, LANES), lambda b, i: (b, 0, 0, 0)),
        ],
        out_specs=pl.BlockSpec((None, tq, n_heads * HALF), lambda b, i: (b, i, 0)),
        out_shape=jax.ShapeDtypeStruct((batch, seq, n_heads * HALF), BF16),
        scratch_shapes=[pltpu.VMEM((n_heads, tq, LANES), F32), pltpu.VMEM((n_heads, tq, LANES), F32)],
        compiler_params=_cparams(("parallel", "parallel")),
        name="flash_attn",
    )(q, k, v)


def _diff_kernel(lam_init, q_ref, k_ref, v_ref, lp_ref, sub_ref, o_ref, acc_ref, m_ref, qv_ref):
    n_heads, tq, _ = q_ref.shape
    seq = k_ref.shape[1]
    tile = pl.program_id(1)
    t0 = tile * tq
    lp = lp_ref[...]
    lam = (jnp.exp(jnp.sum(lp[0:1] * lp[1:2], keepdims=True))
           - jnp.exp(jnp.sum(lp[2:3] * lp[3:4], keepdims=True)) + lam_init)
    lane = _lane((tq, LANES))
    tpos = (t0 + lax.broadcasted_iota(jnp.int32, (tq, LANES), 0)).astype(F32)
    in_f = (lane >= _FEAT) & (lane < _FEAT + 3)
    for h in range(n_heads):
        q = q_ref[h].astype(F32)
        g1, g2, g3 = _split3(2.0 * _diff_slope(h) * tpos)
        left = jnp.where(in_f, 1.0, 0.0)
        right = jnp.where(in_f, -1.0, jnp.where(lane == _FEAT + 3, g1, jnp.where(
            lane == _FEAT + 4, g2, jnp.where(lane == _FEAT + 5, g3, 0.0))))
        for m in range(2):
            qm = jnp.where((lane >= m * DIFF_D) & (lane < (m + 1) * DIFF_D), q, 0.0)
            qv_ref[0, 2 * h + m] = (qm + left).astype(BF16)
            qv_ref[1, 2 * h + m] = (qm + right).astype(BF16)
    acc_ref[...] = jnp.zeros_like(acc_ref)
    m_ref[...] = jnp.full_like(m_ref, NEG)
    chains = range(2 * n_heads)

    def update(scores, ks):
        for i in chains:
            _softmax_update(scores[i], v_ref[i // 2, pl.ds(ks, tq), :], acc_ref, m_ref, i)

    ks = pl.multiple_of(t0, tq)
    update([jnp.minimum(_scores(qv_ref[0, i], k_ref[i // 2, pl.ds(ks, tq), :]),
                        _scores(qv_ref[1, i], k_ref[i // 2, pl.ds(ks, tq), :])) for i in chains], ks)

    def body(c, carry):
        side = (c >= tile).astype(jnp.int32)
        ks = pl.multiple_of((c + side) * tq, tq)
        update([_scores(qv_ref[side, i], k_ref[i // 2, pl.ds(ks, tq), :]) for i in chains], ks)
        return carry

    lax.fori_loop(0, seq // tq - 1, body, 0)

    outs = []
    for h in range(n_heads):
        o = _normalize(acc_ref[2 * h]) - lam * _normalize(acc_ref[2 * h + 1])
        o = jnp.where(lane < DIFF_V, o, 0.0)
        ms = jnp.sum(o * o, axis=-1, keepdims=True) * (1.0 / DIFF_V)
        outs.append(o * lax.rsqrt(ms + DIFF_NORM_EPS) * sub_ref[...] * (1.0 - lam_init))
    for s in range(n_heads // 2):
        o_ref[:, s * LANES:(s + 1) * LANES] = _join_pair(outs[2 * s], outs[2 * s + 1]).astype(o_ref.dtype)


def _diff_attention(q, k, v, lam_params, subln, lam_init, tq):
    batch, n_heads, seq, _ = q.shape
    return pl.pallas_call(
        functools.partial(_diff_kernel, lam_init),
        grid=(batch, seq // tq),
        in_specs=[
            pl.BlockSpec((None, n_heads, tq, LANES), lambda b, i: (b, 0, i, 0)),
            _resident((None, n_heads, seq, LANES), lambda b, i: (b, 0, 0, 0)),
            _resident((None, n_heads, seq, LANES), lambda b, i: (b, 0, 0, 0)),
            pl.BlockSpec((4, DIFF_D), lambda b, i: (0, 0)),
            pl.BlockSpec((1, LANES), lambda b, i: (0, 0)),
        ],
        out_specs=pl.BlockSpec((None, tq, n_heads * HALF), lambda b, i: (b, i, 0)),
        out_shape=jax.ShapeDtypeStruct((batch, seq, n_heads * HALF), BF16),
        scratch_shapes=[pltpu.VMEM((2 * n_heads, tq, LANES), F32)] * 2
        + [pltpu.VMEM((2, 2 * n_heads, tq, LANES), BF16)],
        compiler_params=_cparams(("parallel", "parallel")),
        name="diff_attn",
    )(q, k, v, lam_params, subln)


_DIL_BLOCK = 2048
_DIL_Q = 128


def _dil_kernel(group, dil, seq, q_ref, kp_ref, kc_ref, kn_ref, vp_ref, vc_ref, vn_ref,
                o_ref, lse_ref, kw_ref, vw_ref):
    tb = q_ref.shape[1]
    halo = DIL_GROUPS[group][0] // 2
    assert halo == kp_ref.shape[1] and halo == (_DIL_Q // 2) * dil
    for w_ref, p_ref, c_ref, n_ref in ((kw_ref, kp_ref, kc_ref, kn_ref), (vw_ref, vp_ref, vc_ref, vn_ref)):
        w_ref[:, 0:halo] = p_ref[...]
        w_ref[:, halo:halo + tb] = c_ref[...]
        w_ref[:, halo + tb:halo + tb + halo] = n_ref[...]
    t0 = pl.program_id(1) * tb
    shape = (_DIL_Q, 2 * _DIL_Q)
    rel = lax.broadcasted_iota(jnp.int32, shape, 1) - lax.broadcasted_iota(jnp.int32, shape, 0) - _DIL_Q // 2
    absrel = jnp.abs(rel).astype(F32)
    band = jnp.where(absrel <= _DIL_Q // 2, 0.0, -NEG)
    n_slopes = len(DIL_GROUPS) * DIL_HEADS
    bias = [2.0 ** (-8.0 * (group * DIL_HEADS + h + 1) / n_slopes) * dil * LOG2E * absrel + band
            for h in range(DIL_HEADS)]
    col = lax.broadcasted_iota(jnp.int32, (1, 2 * _DIL_Q), 1)
    lo_mask = _lane((_DIL_Q, LANES)) < HALF
    for unit in range(tb // _DIL_Q):
        first = (unit // dil) * _DIL_Q * dil + unit % dil
        rows = pl.ds(first, _DIL_Q, stride=dil)
        win = pl.ds(first, 2 * _DIL_Q, stride=dil)
        key_token = t0 + (first - halo) + dil * col
        off_seq = jnp.where(key_token >= 0, jnp.where(key_token < seq, 0.0, NEG), NEG)
        for s in range(2):
            qs = q_ref[s, rows, :].astype(BF16)
            ksl = kw_ref[s, win, :].astype(BF16)
            vsl = vw_ref[s, win, :].astype(BF16)
            res = []
            for part in range(2):
                keep = lo_mask if part == 0 else jnp.logical_not(lo_mask)
                sc = _scores(jnp.where(keep, qs, jnp.zeros_like(qs)), ksl) - bias[2 * s + part] + off_seq
                m = jnp.max(sc, axis=-1, keepdims=True)
                p = jnp.exp2(sc - m)
                l = jnp.sum(p, axis=-1, keepdims=True)
                o = jnp.dot(p.astype(BF16), vsl, preferred_element_type=F32) / l
                res.append((o, m + jnp.log2(l)))
            o_ref[s, rows, :] = jnp.where(lo_mask, res[0][0], res[1][0])
            lse_ref[s, rows, :] = jnp.where(lo_mask, res[0][1], res[1][1])


def _dilated_group(q, k, v, group):
    batch, _, seq, _ = q.shape
    dil = DIL_GROUPS[group][1]
    halo = DIL_GROUPS[group][0] // 2
    tb = _DIL_BLOCK
    assert seq % tb == 0 and seq // dil >= 2 * _DIL_Q and tb % (_DIL_Q * dil) == 0
    per = tb // halo
    cur = pl.BlockSpec((None, 2, tb, LANES), lambda b, i: (b, 0, i, 0))
    prev = pl.BlockSpec((None, 2, halo, LANES), lambda b, i: (b, 0, jnp.maximum(i * per - 1, 0), 0))
    nxt = pl.BlockSpec((None, 2, halo, LANES),
                       lambda b, i: (b, 0, jnp.minimum((i + 1) * per, seq // halo - 1), 0))
    out_shape = jax.ShapeDtypeStruct((batch, 2, seq, LANES), F32)
    o, lse = pl.pallas_call(
        functools.partial(_dil_kernel, group, dil, seq),
        grid=(batch, seq // tb),
        in_specs=[cur, prev, cur, nxt, prev, cur, nxt],
        out_specs=[cur, cur],
        out_shape=[out_shape, out_shape],
        scratch_shapes=[pltpu.VMEM((2, tb + 2 * halo, LANES), F32)] * 2,
        compiler_params=_cparams(("parallel", "parallel")),
        name=f"dilated_g{group}",
    )(q, k, k, k, v, v, v)
    return o, lse


def _mix_kernel(h_ref, ya_ref, yc_ref, yd_ref, o0_ref, o1_ref, o2_ref, l0_ref, l1_ref, l2_ref,
                wg_ref, bg_ref, wb_ref, wo_ref, lng_ref, lnb_ref, out_ref, *maybe_bf16_ref):
    h = h_ref[...]
    hb = h.astype(BF16)
    slabs = []
    for s in range(2):
        l0, l1, l2 = l0_ref[s], l1_ref[s], l2_ref[s]
        mx = jnp.maximum(jnp.maximum(l0, l1), l2)
        e0, e1, e2 = jnp.exp2(l0 - mx), jnp.exp2(l1 - mx), jnp.exp2(l2 - mx)
        slabs.append(((e0 * o0_ref[s] + e1 * o1_ref[s] + e2 * o2_ref[s]) / (e0 + e1 + e2)).astype(BF16))
    ys = (ya_ref[...], jnp.concatenate(slabs, axis=1), yc_ref[...], yd_ref[...])
    acc = None
    for n in range(N_BRANCH):
        logit = jnp.dot(hb, wg_ref[:, n * D_MODEL:(n + 1) * D_MODEL],
                        preferred_element_type=F32) + bg_ref[n:n + 1, :]
        gate = 1.0 / (1.0 + jnp.exp(-logit))
        term = gate * jnp.dot(ys[n], wb_ref[n], preferred_element_type=F32)
        acc = term if acc is None else acc + term
    m = jnp.dot(acc.astype(BF16), wo_ref[...], preferred_element_type=F32)
    out = _layernorm(DEEPNORM_ALPHA * h + m, lng_ref[...], lnb_ref[...])
    out_ref[...] = out
    for r in maybe_bf16_ref:
        r[...] = out.astype(BF16)


def _mix(h, ya, yc, yd, dil_o, dil_lse, wg, bg, wb, wo, lng, lnb, tm, with_bf16_copy):
    T = h.shape[0]
    nst = dil_o[0].shape[2] // tm
    slab = pl.BlockSpec((None, 2, tm, LANES), lambda i: (i // nst, 0, i % nst, 0))
    row = lambda i: (i, 0)
    const = lambda i: (0, 0)
    wide = pl.BlockSpec((tm, D_MODEL), row)
    narrow = pl.BlockSpec((tm, BRANCH_W), row)
    return pl.pallas_call(
        _mix_kernel,
        grid=(T // tm,),
        in_specs=[wide] + [narrow] * 3 + [slab] * 6 + [
            _resident((D_MODEL, N_BRANCH * D_MODEL), const),
            pl.BlockSpec((N_BRANCH, D_MODEL), const),
            _resident((N_BRANCH, BRANCH_W, D_MODEL), lambda i: (0, 0, 0)),
            _resident((D_MODEL, D_MODEL), const),
            pl.BlockSpec((1, D_MODEL), const),
            pl.BlockSpec((1, D_MODEL), const),
        ],
        out_specs=[wide] * (2 if with_bf16_copy else 1),
        out_shape=[jax.ShapeDtypeStruct((T, D_MODEL), F32)]
        + ([jax.ShapeDtypeStruct((T, D_MODEL), BF16)] if with_bf16_copy else []),
        compiler_params=_cparams(("parallel",)),
        name="gate_mix",
    )(h, ya, yc, yd, *dil_o, *dil_lse, wg, bg, wb, wo, lng, lnb)


def _swiglu_step(xb_ref, w1_ref, w3_ref, w2_ref, acc_ref):
    @pl.when(pl.program_id(1) == 0)
    def _():
        acc_ref[...] = jnp.zeros_like(acc_ref)

    xb = xb_ref[...]
    tf = w1_ref.shape[1]
    mid = (tf // LANES + 1) // 2 * LANES
    total = None
    for lo, hi in ((0, mid), (mid, tf)):
        a = jnp.dot(xb, w1_ref[:, lo:hi].astype(BF16), preferred_element_type=F32)
        b = jnp.dot(xb, w3_ref[:, lo:hi].astype(BF16), preferred_element_type=F32)
        g = (a / (1.0 + jnp.exp(-a)) * b).astype(BF16)
        part = jnp.dot(g, w2_ref[lo:hi, :].astype(BF16), preferred_element_type=F32)
        total = part if total is None else total + part
    acc_ref[...] += total


def _ffn_dense_kernel(x_ref, w1_ref, w3_ref, w2_ref, lng_ref, lnb_ref, o_ref, acc_ref, xb_ref):
    @pl.when(pl.program_id(1) == 0)
    def _():
        xb_ref[...] = x_ref[...].astype(BF16)

    _swiglu_step(xb_ref, w1_ref, w3_ref, w2_ref, acc_ref)

    @pl.when(pl.program_id(1) == pl.num_programs(1) - 1)
    def _():
        o_ref[...] = _layernorm(DEEPNORM_ALPHA * x_ref[...] + acc_ref[...], lng_ref[...], lnb_ref[...])


def _ffn_dense(x, w13, w2, lng, lnb, tm, tf):
    T = x.shape[0]
    nf = D_FF // tf
    return pl.pallas_call(
        _ffn_dense_kernel,
        grid=(T // tm, nf),
        in_specs=[
            pl.BlockSpec((tm, D_MODEL), lambda i, j: (i, 0)),
            pl.BlockSpec((D_MODEL, tf), lambda i, j: (0, j)),
            pl.BlockSpec((D_MODEL, tf), lambda i, j: (0, nf + j)),
            pl.BlockSpec((tf, D_MODEL), lambda i, j: (j, 0)),
            pl.BlockSpec((1, D_MODEL), lambda i, j: (0, 0)),
            pl.BlockSpec((1, D_MODEL), lambda i, j: (0, 0)),
        ],
        out_specs=pl.BlockSpec((tm, D_MODEL), lambda i, j: (i, 0)),
        out_shape=jax.ShapeDtypeStruct((T, D_MODEL), F32),
        scratch_shapes=[pltpu.VMEM((tm, D_MODEL), F32), pltpu.VMEM((tm, D_MODEL), BF16)],
        compiler_params=_cparams(("parallel", "arbitrary")),
        name="ffn_dense",
    )(x, w13, w13, w2, lng, lnb)


def _ffn_expert_kernel(te_ref, used_ref, x_ref, w1_ref, w3_ref, w2_ref, o_ref, acc_ref):
    del te_ref
    live = pl.program_id(0) < used_ref[0]
    last = pl.program_id(1) == pl.num_programs(1) - 1

    @pl.when(live)
    def _():
        _swiglu_step(x_ref, w1_ref, w3_ref, w2_ref, acc_ref)

    @pl.when(live & last)
    def _():
        o_ref[...] = acc_ref[...].astype(o_ref.dtype)

    @pl.when(jnp.logical_not(live) & last)
    def _():
        o_ref[...] = jnp.zeros_like(o_ref)


def _ffn_experts(tile_expert, tiles_used, x_sorted, w13, w2, tm, tf):
    rows = x_sorted.shape[0]
    nf = D_FF // tf
    fblk = lambda i, j, nu: jnp.where(i < nu[0], j, nf - 1)
    grid_spec = pltpu.PrefetchScalarGridSpec(
        num_scalar_prefetch=2,
        grid=(rows // tm, nf),
        in_specs=[
            pl.BlockSpec((tm, D_MODEL), lambda i, j, te, nu: (i, 0)),
            pl.BlockSpec((None, D_MODEL, tf), lambda i, j, te, nu: (te[i], 0, fblk(i, j, nu))),
            pl.BlockSpec((None, D_MODEL, tf), lambda i, j, te, nu: (te[i], 0, nf + fblk(i, j, nu))),
            pl.BlockSpec((None, tf, D_MODEL), lambda i, j, te, nu: (te[i], fblk(i, j, nu), 0)),
        ],
        out_specs=pl.BlockSpec((tm, D_MODEL), lambda i, j, te, nu: (i, 0)),
        scratch_shapes=[pltpu.VMEM((tm, D_MODEL), F32)],
    )
    return pl.pallas_call(
        _ffn_expert_kernel,
        grid_spec=grid_spec,
        out_shape=jax.ShapeDtypeStruct((rows, D_MODEL), BF16),
        compiler_params=_cparams(("parallel", "arbitrary")),
        name="ffn_experts",
    )(tile_expert, tiles_used, x_sorted, w13, w13, w2)


def _router_kernel(x_ref, wr_ref, idx_ref, wt_ref):
    logits = jnp.dot(x_ref[...], wr_ref[...], preferred_element_type=F32,
                     precision=lax.Precision.HIGHEST)
    lane = _lane(logits.shape)
    logits = jnp.where(lane < N_EXPERTS, logits, -jnp.inf)
    v1 = jnp.max(logits, axis=-1, keepdims=True)
    i1 = jnp.min(jnp.where(logits == v1, lane, LANES), axis=-1, keepdims=True)
    rest = jnp.where(lane == i1, -jnp.inf, logits)
    v2 = jnp.max(rest, axis=-1, keepdims=True)
    i2 = jnp.min(jnp.where(rest == v2, lane, LANES), axis=-1, keepdims=True)
    e2 = jnp.exp(v2 - v1)
    w1 = 1.0 / (1.0 + e2)
    idx_ref[...] = jnp.where(lane == 0, i1, jnp.where(lane == 1, i2, 0))
    wt_ref[...] = jnp.where(lane == 0, w1, jnp.where(lane == 1, e2 * w1, 0.0))


def _router(x, w_router, tm):
    T = x.shape[0]
    wr = jnp.pad(w_router, ((0, 0), (0, LANES - N_EXPERTS)))
    row = lambda i: (i, 0)
    return pl.pallas_call(
        _router_kernel,
        grid=(T // tm,),
        in_specs=[pl.BlockSpec((tm, D_MODEL), row), pl.BlockSpec((D_MODEL, LANES), lambda i: (0, 0))],
        out_specs=[pl.BlockSpec((tm, LANES), row), pl.BlockSpec((tm, LANES), row)],
        out_shape=[jax.ShapeDtypeStruct((T, LANES), jnp.int32), jax.ShapeDtypeStruct((T, LANES), F32)],
        compiler_params=_cparams(("parallel",)),
        name="router",
    )(x, wr)


def _moe_out_kernel(h_ref, y0_ref, y1_ref, wt_ref, lng_ref, lnb_ref, o_ref):
    wt = wt_ref[...]
    f = wt[:, 0:1] * y0_ref[...] + wt[:, 1:2] * y1_ref[...]
    o_ref[...] = _layernorm(DEEPNORM_ALPHA * h_ref[...] + f, lng_ref[...], lnb_ref[...])


def _moe_out(h, y0, y1, wt, lng, lnb, tm):
    T = h.shape[0]
    row = lambda i: (i, 0)
    const = lambda i: (0, 0)
    wide = pl.BlockSpec((tm, D_MODEL), row)
    return pl.pallas_call(
        _moe_out_kernel,
        grid=(T // tm,),
        in_specs=[wide, wide, wide, pl.BlockSpec((tm, LANES), row),
                  pl.BlockSpec((1, D_MODEL), const), pl.BlockSpec((1, D_MODEL), const)],
        out_specs=wide,
        out_shape=jax.ShapeDtypeStruct((T, D_MODEL), F32),
        compiler_params=_cparams(("parallel",)),
        name="moe_out",
    )(h, y0, y1, wt, lng, lnb)


def _moe(h, h_bf16, w_router, w13, w2, lng, lnb, tm_router, tm_e, tf):
    T = h.shape[0]
    idx, wt = _router(h, w_router, tm_router)
    e_flat = jnp.concatenate([idx[:, s] for s in range(TOP_K)])
    onehot = (e_flat[None, :] == jnp.arange(N_EXPERTS, dtype=jnp.int32)[:, None]).astype(jnp.int32)
    counts = jnp.sum(onehot, axis=1)
    rank = jnp.sum((jnp.cumsum(onehot, axis=1) - onehot) * onehot, axis=0)
    padded = (counts + tm_e - 1) // tm_e * tm_e
    ends = jnp.cumsum(padded)
    pos = jnp.sum(onehot * (ends - padded)[:, None], axis=0) + rank
    rows = TOP_K * T + N_EXPERTS * tm_e
    token = jnp.arange(TOP_K * T, dtype=jnp.int32) % T
    src = (jnp.arange(rows, dtype=jnp.int32) % T).at[pos].set(
        token, unique_indices=True, mode="promise_in_bounds")
    tile_start = jnp.arange(rows // tm_e, dtype=jnp.int32) * tm_e
    tile_expert = jnp.minimum(jnp.sum((tile_start[:, None] >= ends[None, :]).astype(jnp.int32), axis=1),
                              N_EXPERTS - 1)
    x_sorted = h_bf16.at[src].get(mode="promise_in_bounds")
    tiles_used = (ends[N_EXPERTS - 1:] // tm_e).astype(jnp.int32)
    y = _ffn_experts(tile_expert, tiles_used, x_sorted, w13, w2, tm_e, tf)
    y0 = y.at[pos[:T]].get(mode="promise_in_bounds")
    y1 = y.at[pos[T:]].get(mode="promise_in_bounds")
    return _moe_out(h, y0, y1, wt, lng, lnb, tm_router)


def _rot_cols(w, block):
    d, n = w.shape
    w4 = w.reshape(d, n // block, 2, block // 2)
    return jnp.concatenate([-w4[:, :, 1], w4[:, :, 0]], axis=2).reshape(d, n)


def _attention_weights(w_in):
    o = np.cumsum([0, MLA_Q_LORA, MLA_KV_LORA, MLA_ROPE, 2304, 256, 256, 512 + 256, N_BRANCH * D_MODEL])
    cq, ckv, kr, dil, gq, gkv, diff, gate = (w_in[:, o[i]:o[i + 1]] for i in range(8))
    z = lambda n: jnp.zeros((D_MODEL, n), w_in.dtype)
    mla = [cq, z(64), ckv, z(64), kr, z(32), z(64), _rot_cols(kr, MLA_ROPE), z(32)]
    gqa = [gq, _rot_cols(gq, 32), gkv[:, :128], _rot_cols(gkv[:, :128], 32), gkv[:, 128:]]
    small = jnp.concatenate(mla + gqa, axis=1)
    return [w.astype(BF16) for w in (small, diff, dil)], gate.astype(BF16)


def _mla_up_weights(w_uq, w_ukv):
    zq = lambda n: jnp.zeros((MLA_Q_LORA, n), w_uq.dtype)
    per = MLA_NOPE + MLA_ROPE
    plain, rot = [], []
    for h in range(MLA_HEADS):
        nope = w_uq[:, h * per:h * per + MLA_NOPE]
        rope = w_uq[:, h * per + MLA_NOPE:(h + 1) * per]
        plain += [nope, rope, zq(32)]
        rot += [zq(64), _rot_cols(rope, MLA_ROPE), zq(32)]
    wuq = jnp.concatenate(plain + rot, axis=1)
    wuq = jnp.pad(wuq, ((0, 256 - MLA_Q_LORA), (0, 0)))
    zk = jnp.zeros((MLA_KV_LORA, 64), w_ukv.dtype)
    per = MLA_NOPE + MLA_V
    ks, vs = [], []
    for h in range(MLA_HEADS):
        ks += [w_ukv[:, h * per:h * per + MLA_NOPE], zk]
        vs += [w_ukv[:, h * per + MLA_NOPE:(h + 1) * per], zk]
    wukv = jnp.concatenate(ks + vs, axis=1)
    return wuq.astype(BF16), wukv.astype(BF16)


def _rope_angles(pos, dim):
    inv = ROPE_THETA ** (-(jnp.arange(0, dim, 2, dtype=F32) / dim))
    return pos[:, None] * inv[None, :]


def _position_tables(seq):
    rows = seq // GRID_W
    row_idx = jnp.repeat(jnp.arange(rows, dtype=F32), GRID_W)
    col_idx = jnp.tile(jnp.arange(GRID_W, dtype=F32), rows)
    a1 = _rope_angles(jnp.arange(seq, dtype=F32), MLA_ROPE)
    ar = _rope_angles(row_idx, HEAD_DIM // 2)
    ac = _rope_angles(col_idx, HEAD_DIM // 2)
    z = lambda n: jnp.zeros((seq, n), F32)
    c1 = jnp.concatenate([jnp.ones((seq, MLA_NOPE), F32)] + [jnp.cos(a1)] * 2 + [z(32)], axis=1)
    s1 = jnp.concatenate([z(MLA_NOPE)] + [jnp.sin(a1)] * 2 + [z(32)], axis=1)
    ca = jnp.concatenate(([jnp.cos(ar)] * 2 + [jnp.cos(ac)] * 2) * 2, axis=1)
    sa = jnp.concatenate(([jnp.sin(ar)] * 2 + [jnp.sin(ac)] * 2) * 2, axis=1)
    return [c1, s1, ca, sa]


def _gqa_gain_rows(q_norm, k_norm):
    def swap(g):
        return jnp.concatenate([g[16:32], g[0:16], g[48:64], g[32:48]])

    sg = HEAD_DIM ** -0.5 * LOG2E
    rows = [sg * q_norm, sg * swap(q_norm), k_norm, swap(k_norm)]
    return jnp.stack([jnp.concatenate([r, r]) for r in rows])


def _pick(n, pref):
    t = min(n, pref)
    assert n % t == 0, (n, pref)
    return t


def kernel(x, ln_emb_g, ln_emb_b, w_in, b_gate, mla_q_norm, mla_kv_norm, mla_w_uq, mla_w_ukv,
           gqa_q_norm, gqa_k_norm, diff_lambda, diff_subln, w_branch, w_out, ln1_g, ln1_b,
           ffn_w13, ffn_w2, moe_router, moe_w13, moe_w2, ln2_g, ln2_b):
    batch, seq, _ = x.shape
    T = batch * seq
    tm = _pick(seq, 512)
    tq = _pick(seq, 512)
    tk = _pick(seq, 1024)
    row = lambda v: v.reshape(1, -1)
    pos_tabs = _position_tables(seq)

    h = x.reshape(T, D_MODEL)
    for l in range(DEPTH):
        lam_init = 0.8 - 0.6 * float(np.exp(-0.3 * l))
        weights, wg = _attention_weights(w_in[l])
        wuq, wukv = _mla_up_weights(mla_w_uq[l], mla_w_ukv[l])
        gains = _gqa_gain_rows(gqa_q_norm[l], gqa_k_norm[l])
        nq = jnp.pad(mla_q_norm[l], (0, 256 - MLA_Q_LORA)).reshape(1, 256)
        res = _project(h, batch, seq, l == 0, row(ln_emb_g), row(ln_emb_b), weights, wuq, wukv,
                       nq, row(mla_kv_norm[l]), gains, pos_tabs, tm)
        if l == 0:
            h, res = res[0], res[1:]
        mq, mk, mv, gq, gk, gv, dq, dk, dv = res[:9]
        dil = res[9:]
        ya = _flash(mq, mk, mv, tk, tk).reshape(T, BRANCH_W)
        yc = _flash(gq, gk, gv, tk, tk).reshape(T, BRANCH_W)
        sub = jnp.pad(diff_subln[l], (0, LANES - DIFF_V)).reshape(1, LANES)
        yd = _diff_attention(dq, dk, dv, diff_lambda[l], sub, lam_init, tq).reshape(T, BRANCH_W)
        dil_o, dil_lse = [], []
        for g in range(len(DIL_GROUPS)):
            o, lse = _dilated_group(dil[g], dil[3 + g], dil[6 + g], g)
            dil_o.append(o)
            dil_lse.append(lse)
        dense = l % 2 == 0
        mixed = _mix(h, ya, yc, yd, dil_o, dil_lse, wg, b_gate[l], w_branch[l].astype(BF16),
                     w_out[l].astype(BF16), row(ln1_g[l]), row(ln1_b[l]), tm, not dense)
        if dense:
            h = _ffn_dense(mixed[0], ffn_w13[l // 2], ffn_w2[l // 2],
                           row(ln2_g[l]), row(ln2_b[l]), _pick(T, 1024), 896)
        else:
            h = _moe(mixed[0], mixed[1], moe_router[l // 2], moe_w13[l // 2], moe_w2[l // 2],
                     row(ln2_g[l]), row(ln2_b[l]), tm, _pick(T, 1024), 896)
    return h.reshape(batch, seq, D_MODEL)
```

```python
import functools
from typing import NamedTuple

import jax
import jax.numpy as jnp
import numpy as np
from jax import lax
from jax.experimental import pallas as pl
from jax.experimental.pallas import tpu as pltpu

D_MODEL = 1024
GRID_W = 64
HEAD_DIM = 64
N_BRANCH = 4
BRANCH_W = 256
MLA_HEADS = 4
MLA_NOPE = 64
MLA_ROPE = 32
MLA_V = 64
MLA_Q_LORA = 192
MLA_KV_LORA = 128
DIL_GROUPS = ((128, 1), (512, 4), (2048, 16))
DIL_HEADS = 4
GQA_Q_HEADS = 4
GQA_KV_HEADS = 2
DIFF_HEADS = 4
DIFF_D = 32
DIFF_V = 64
D_FF = 3584
N_EXPERTS = 8
TOP_K = 2
ROPE_THETA = 10000.0
LN_EPS = 1e-5
RMS_EPS = 1e-6
DIFF_NORM_EPS = 1e-5
DEPTH = 2
DEEPNORM_ALPHA = (2 * DEPTH) ** 0.25

LANES = 128
HALF = LANES // 2
VMEM_BYTES = 64 * 1024 * 1024
VMEM_LIMIT = VMEM_BYTES - 8 * 1024 * 1024

LOG2E = 1.4426950408889634
NEG = -1e30
BF16 = jnp.bfloat16
F32 = jnp.float32

_Q_LORA_PAD = 2 * LANES
_MLA_COLS = _Q_LORA_PAD + MLA_KV_LORA + 2 * LANES
_GQA_COLS = 7 * LANES
_DIFF_COLS = 3 * DIFF_HEADS * DIFF_V
_DIL_COLS = 3 * len(DIL_GROUPS) * BRANCH_W
_SMALL_COLS = _MLA_COLS + _GQA_COLS


def _cparams(sem):
    return pltpu.CompilerParams(dimension_semantics=sem, vmem_limit_bytes=VMEM_LIMIT)


def _resident(shape, index_map):
    return pl.BlockSpec(shape, index_map, pipeline_mode=pl.Buffered(1))


def _layernorm(x, g, b):
    mu = jnp.mean(x, axis=-1, keepdims=True)
    xc = x - mu
    var = jnp.mean(xc * xc, axis=-1, keepdims=True)
    return xc * lax.rsqrt(var + LN_EPS) * g + b


def _lane(shape):
    return lax.broadcasted_iota(jnp.int32, shape, len(shape) - 1)


def _split_pair(x):
    lo_mask = _lane(x.shape) < HALF
    lo = jnp.where(lo_mask, x, 0.0)
    hi = jnp.where(lo_mask, pltpu.roll(x, HALF, axis=1), 0.0)
    return lo, hi


def _join_pair(lo, hi):
    return jnp.where(_lane(lo.shape) < HALF, lo, pltpu.roll(hi, HALF, axis=1))


def _with_ones(v):
    return jnp.where(_lane(v.shape) == HALF, 1.0, v)


def _diff_slope(h):
    return 2.0 ** (-8.0 * (h + 1) / DIFF_HEADS) * LOG2E


def _split3(val):
    f1 = val.astype(BF16).astype(F32)
    r1 = val - f1
    f2 = r1.astype(BF16).astype(F32)
    f3 = (r1 - f2).astype(BF16).astype(F32)
    return f1, f2, f3


_FEAT = 2 * DIFF_D


def _key_bias_features(val):
    f1, f2, f3 = _split3(val)
    lane = _lane(val.shape)
    ones = jnp.where((lane >= _FEAT + 3) & (lane < _FEAT + 6), 1.0, 0.0)
    return jnp.where(lane == _FEAT, f1, jnp.where(lane == _FEAT + 1, f2, jnp.where(lane == _FEAT + 2, f3, ones)))


def _proj_kernel(apply_ln, tiles_per_seq, *refs):
    (x_ref, lng_ref, lnb_ref, ws_ref, wdiff_ref, wdil_ref, wuq_ref, wukv_ref, nq_ref, nkv_ref,
     gains_ref, c1_ref, s1_ref, ca_ref, sa_ref) = refs[:15]
    outs = refs[15:]
    if apply_ln:
        h_ref, outs = outs[0], outs[1:]
    (mq_ref, mk_ref, mv_ref, gq_ref, gk_ref, gv_ref, dq_ref, dk_ref, dv_ref) = outs[:9]
    dil_refs = outs[9:]

    x = x_ref[...]
    if apply_ln:
        x = _layernorm(x, lng_ref[...], lnb_ref[...])
        h_ref[...] = x
    hb = x.astype(BF16)

    def proj(w_ref, off, n):
        return jnp.dot(hb, w_ref[:, off:off + n], preferred_element_type=F32)

    p = proj(ws_ref, 0, _MLA_COLS)
    cq, ckv, kr, krr = p[:, 0:256], p[:, 256:384], p[:, 384:512], p[:, 512:640]
    rq = lax.rsqrt(jnp.sum(cq * cq, axis=-1, keepdims=True) * (1.0 / MLA_Q_LORA) + RMS_EPS)
    cqn = (cq * rq * nq_ref[...]).astype(BF16)
    q2 = jnp.dot(cqn, wuq_ref[...], preferred_element_type=F32)
    rkv = lax.rsqrt(jnp.sum(ckv * ckv, axis=-1, keepdims=True) * (1.0 / MLA_KV_LORA) + RMS_EPS)
    ckvn = (ckv * rkv * nkv_ref[...]).astype(BF16)
    kv2 = jnp.dot(ckvn, wukv_ref[...], preferred_element_type=F32)
    c1, s1 = c1_ref[...], s1_ref[...]
    k_rope = kr * c1 + krr * s1
    mla_scale = (MLA_NOPE + MLA_ROPE) ** -0.5 * LOG2E
    cq_t, sq_t = c1 * mla_scale, s1 * mla_scale
    for h in range(MLA_HEADS):
        qh = q2[:, h * LANES:(h + 1) * LANES]
        qrh = q2[:, (MLA_HEADS + h) * LANES:(MLA_HEADS + h + 1) * LANES]
        mq_ref[h] = (qh * cq_t + qrh * sq_t).astype(BF16)
        mk_ref[h] = (kv2[:, h * LANES:(h + 1) * LANES] + k_rope).astype(BF16)
        vh = kv2[:, (MLA_HEADS + h) * LANES:(MLA_HEADS + h + 1) * LANES]
        mv_ref[h] = _with_ones(vh).astype(BF16)

    p = proj(ws_ref, _MLA_COLS, _GQA_COLS)

    def head_rms(xs):
        ss = xs * xs
        lo_mask = _lane(xs.shape) < HALF
        s_all = jnp.sum(ss, axis=-1, keepdims=True)
        s_lo = jnp.sum(jnp.where(lo_mask, ss, 0.0), axis=-1, keepdims=True)
        return lax.rsqrt(jnp.where(lo_mask, s_lo, s_all - s_lo) * (1.0 / HEAD_DIM) + RMS_EPS)

    ca, sa = ca_ref[...], sa_ref[...]
    aq, bq = ca * gains_ref[0:1, :], sa * gains_ref[1:2, :]
    ak, bk = ca * gains_ref[2:3, :], sa * gains_ref[3:4, :]
    for s in range(2):
        qs = p[:, s * LANES:(s + 1) * LANES]
        qrs = p[:, 256 + s * LANES:256 + (s + 1) * LANES]
        qn = head_rms(qs) * (qs * aq + qrs * bq)
        lo, hi = _split_pair(qn)
        gq_ref[2 * s] = lo.astype(BF16)
        gq_ref[2 * s + 1] = hi.astype(BF16)
    ks, krs = p[:, 512:640], p[:, 640:768]
    kn = head_rms(ks) * (ks * ak + krs * bk)
    lo, hi = _split_pair(kn)
    gk_ref[0] = lo.astype(BF16)
    gk_ref[1] = hi.astype(BF16)
    lo, hi = _split_pair(p[:, 768:896])
    gv_ref[0] = _with_ones(lo).astype(BF16)
    gv_ref[1] = _with_ones(hi).astype(BF16)

    p = proj(wdiff_ref, 0, _DIFF_COLS)
    dscale = DIFF_D ** -0.5 * LOG2E
    tm = x.shape[0]
    pos = (pl.program_id(0) % tiles_per_seq * tm
           + lax.broadcasted_iota(jnp.int32, (tm, LANES), 0)).astype(F32)
    for s in range(2):
        lo, hi = _split_pair(p[:, s * LANES:(s + 1) * LANES] * dscale)
        dq_ref[2 * s] = lo.astype(BF16)
        dq_ref[2 * s + 1] = hi.astype(BF16)
        lo, hi = _split_pair(p[:, 256 + s * LANES:256 + (s + 1) * LANES])
        dk_ref[2 * s] = (lo + _key_bias_features(_diff_slope(2 * s) * pos)).astype(BF16)
        dk_ref[2 * s + 1] = (hi + _key_bias_features(_diff_slope(2 * s + 1) * pos)).astype(BF16)
        lo, hi = _split_pair(p[:, 512 + s * LANES:512 + (s + 1) * LANES])
        dv_ref[2 * s] = _with_ones(lo).astype(BF16)
        dv_ref[2 * s + 1] = _with_ones(hi).astype(BF16)

    lscale = HEAD_DIM ** -0.5 * LOG2E
    for i in range(9):
        seg = proj(wdil_ref, i * BRANCH_W, BRANCH_W)
        if i < 3:
            seg = seg * lscale
        dil_refs[i][0] = seg[:, :LANES]
        dil_refs[i][1] = seg[:, LANES:]


def _project(x2d, batch, seq, apply_ln, lng, lnb, weights, wuq, wukv, nq, nkv, gains, tables, tm):
    T = x2d.shape[0]
    nst = seq // tm
    grid = (T // tm,)
    row = lambda i: (i, 0)
    const = lambda i: (0, 0)
    tab = lambda i: (i % nst, 0)
    head = lambda i: (i // nst, 0, i % nst, 0)

    in_specs = [
        pl.BlockSpec((tm, D_MODEL), row),
        pl.BlockSpec((1, D_MODEL), const),
        pl.BlockSpec((1, D_MODEL), const),
        _resident((D_MODEL, _SMALL_COLS), const),
        _resident((D_MODEL, _DIFF_COLS), const),
        _resident((D_MODEL, _DIL_COLS), const),
        _resident((_Q_LORA_PAD, 8 * LANES), const),
        _resident((LANES, 8 * LANES), const),
        pl.BlockSpec((1, _Q_LORA_PAD), const),
        pl.BlockSpec((1, LANES), const),
        pl.BlockSpec((4, LANES), const),
    ] + [pl.BlockSpec((tm, LANES), tab)] * 4

    def heads_out(n):
        return (jax.ShapeDtypeStruct((batch, n, seq, LANES), BF16),
                pl.BlockSpec((None, n, tm, LANES), head))

    outs = []
    if apply_ln:
        outs.append((jax.ShapeDtypeStruct((T, D_MODEL), F32), pl.BlockSpec((tm, D_MODEL), row)))
    outs += [heads_out(MLA_HEADS)] * 3
    outs += [heads_out(GQA_Q_HEADS), heads_out(GQA_KV_HEADS), heads_out(GQA_KV_HEADS)]
    outs += [heads_out(DIFF_HEADS)] * 3
    outs += [(jax.ShapeDtypeStruct((batch, 2, seq, LANES), F32), pl.BlockSpec((None, 2, tm, LANES), head))] * 9

    res = pl.pallas_call(
        functools.partial(_proj_kernel, apply_ln, nst),
        grid=grid,
        in_specs=in_specs,
        out_specs=[o[1] for o in outs],
        out_shape=[o[0] for o in outs],
        compiler_params=_cparams(("parallel",)),
        name="proj_prep",
    )(x2d, lng, lnb, *weights, wuq, wukv, nq, nkv, gains, *tables)
    return res


def _normalize(acc):
    return acc / acc[:, HALF:HALF + 1]


def _scores(q, k):
    return lax.dot_general(q, k, (((1,), (1,)), ((), ())), preferred_element_type=F32)


def _softmax_update(s, v, acc_ref, m_ref, idx):
    blocks = [s[:, j * LANES:(j + 1) * LANES] for j in range(s.shape[1] // LANES)]
    part = functools.reduce(jnp.maximum, blocks)
    m_prev = m_ref[idx]
    m_new = jnp.maximum(m_prev, jnp.max(part, axis=-1, keepdims=True))
    p = jnp.concatenate([jnp.exp2((b - m_new).astype(BF16)) for b in blocks], axis=1)
    acc_ref[idx] = (jnp.exp2(m_prev - m_new) * acc_ref[idx]
                    + jnp.dot(p, v, preferred_element_type=F32))
    m_ref[idx] = m_new


def _flash_kernel(n_rep, tk, q_ref, k_ref, v_ref, o_ref, acc_ref, m_ref):
    n_heads = q_ref.shape[0]
    seq = k_ref.shape[1]
    acc_ref[...] = jnp.zeros_like(acc_ref)
    m_ref[...] = jnp.full_like(m_ref, NEG)

    def body(c, carry):
        ks = pl.multiple_of(c * tk, tk)
        scores = [_scores(q_ref[h], k_ref[h // n_rep, pl.ds(ks, tk), :]) for h in range(n_heads)]
        for h in range(n_heads):
            _softmax_update(scores[h], v_ref[h // n_rep, pl.ds(ks, tk), :], acc_ref, m_ref, h)
        return carry

    lax.fori_loop(0, seq // tk, body, 0)
    outs = [_normalize(acc_ref[h]) for h in range(n_heads)]
    for s in range(n_heads // 2):
        o_ref[:, s * LANES:(s + 1) * LANES] = _join_pair(outs[2 * s], outs[2 * s + 1]).astype(o_ref.dtype)


def _flash(q, k, v, tq, tk):
    batch, n_heads, seq, _ = q.shape
    n_kv = k.shape[1]
    return pl.pallas_call(
        functools.partial(_flash_kernel, n_heads // n_kv, tk),
        grid=(batch, seq // tq),
        in_specs=[
            pl.BlockSpec((None, n_heads, tq, LANES), lambda b, i: (b, 0, i, 0)),
            _resident((None, n_kv, seq, LANES), lambda b, i: (b, 0, 0, 0)),
            _resident((None, n_kv, seq, LANES), lambda b, i: (b, 0, 0, 0)),
        ],
        out_specs=pl.BlockSpec((None, tq, n_heads * HALF), lambda b, i: (b, i, 0)),
        out_shape=jax.ShapeDtypeStruct((batch, seq, n_heads * HALF), BF16),
        scratch_shapes=[pltpu.VMEM((n_heads, tq, LANES), F32), pltpu.VMEM((n_heads, tq, LANES), F32)],
        compiler_params=_cparams(("parallel", "parallel")),
        name="flash_attn",
    )(q, k, v)


def _diff_kernel(lam_init, q_ref, k_ref, v_ref, lp_ref, sub_ref, o_ref, acc_ref, m_ref, qv_ref):
    n_heads, tq, _ = q_ref.shape
    seq = k_ref.shape[1]
    tile = pl.program_id(1)
    t0 = tile * tq
    lp = lp_ref[...]
    lam = (jnp.exp(jnp.sum(lp[0:1] * lp[1:2], keepdims=True))
           - jnp.exp(jnp.sum(lp[2:3] * lp[3:4], keepdims=True)) + lam_init)
    lane = _lane((tq, LANES))
    tpos = (t0 + lax.broadcasted_iota(jnp.int32, (tq, LANES), 0)).astype(F32)
    in_f = (lane >= _FEAT) & (lane < _FEAT + 3)
    for h in range(n_heads):
        q = q_ref[h].astype(F32)
        g1, g2, g3 = _split3(2.0 * _diff_slope(h) * tpos)
        left = jnp.where(in_f, 1.0, 0.0)
        right = jnp.where(in_f, -1.0, jnp.where(lane == _FEAT + 3, g1, jnp.where(
            lane == _FEAT + 4, g2, jnp.where(lane == _FEAT + 5, g3, 0.0))))
        for m in range(2):
            qm = jnp.where((lane >= m * DIFF_D) & (lane < (m + 1) * DIFF_D), q, 0.0)
            qv_ref[0, 2 * h + m] = (qm + left).astype(BF16)
            qv_ref[1, 2 * h + m] = (qm + right).astype(BF16)
    acc_ref[...] = jnp.zeros_like(acc_ref)
    m_ref[...] = jnp.full_like(m_ref, NEG)
    chains = range(2 * n_heads)

    def update(scores, ks):
        for i in chains:
            _softmax_update(scores[i], v_ref[i // 2, pl.ds(ks, tq), :], acc_ref, m_ref, i)

    ks = pl.multiple_of(t0, tq)
    update([jnp.minimum(_scores(qv_ref[0, i], k_ref[i // 2, pl.ds(ks, tq), :]),
                        _scores(qv_ref[1, i], k_ref[i // 2, pl.ds(ks, tq), :])) for i in chains], ks)

    def body(c, carry):
        side = (c >= tile).astype(jnp.int32)
        ks = pl.multiple_of((c + side) * tq, tq)
        update([_scores(qv_ref[side, i], k_ref[i // 2, pl.ds(ks, tq), :]) for i in chains], ks)
        return carry

    lax.fori_loop(0, seq // tq - 1, body, 0)

    outs = []
    for h in range(n_heads):
        o = _normalize(acc_ref[2 * h]) - lam * _normalize(acc_ref[2 * h + 1])
        o = jnp.where(lane < DIFF_V, o, 0.0)
        ms = jnp.sum(o * o, axis=-1, keepdims=True) * (1.0 / DIFF_V)
        outs.append(o * lax.rsqrt(ms + DIFF_NORM_EPS) * sub_ref[...] * (1.0 - lam_init))
    for s in range(n_heads // 2):
        o_ref[:, s * LANES:(s + 1) * LANES] = _join_pair(outs[2 * s], outs[2 * s + 1]).astype(o_ref.dtype)


def _diff_attention(q, k, v, lam_params, subln, lam_init, tq):
    batch, n_heads, seq, _ = q.shape
    return pl.pallas_call(
        functools.partial(_diff_kernel, lam_init),
        grid=(batch, seq // tq),
        in_specs=[
            pl.BlockSpec((None, n_heads, tq, LANES), lambda b, i: (b, 0, i, 0)),
            _resident((None, n_heads, seq, LANES), lambda b, i: (b, 0, 0, 0)),
            _resident((None, n_heads, seq, LANES), lambda b, i: (b, 0, 0, 0)),
            pl.BlockSpec((4, DIFF_D), lambda b, i: (0, 0)),
            pl.BlockSpec((1, LANES), lambda b, i: (0, 0)),
        ],
        out_specs=pl.BlockSpec((None, tq, n_heads * HALF), lambda b, i: (b, i, 0)),
        out_shape=jax.ShapeDtypeStruct((batch, seq, n_heads * HALF), BF16),
        scratch_shapes=[pltpu.VMEM((2 * n_heads, tq, LANES), F32)] * 2
        + [pltpu.VMEM((2, 2 * n_heads, tq, LANES), BF16)],
        compiler_params=_cparams(("parallel", "parallel")),
        name="diff_attn",
    )(q, k, v, lam_params, subln)


_DIL_BLOCK = 2048
_DIL_Q = 128


def _dil_kernel(group, dil, seq, q_ref, kp_ref, kc_ref, kn_ref, vp_ref, vc_ref, vn_ref,
                o_ref, lse_ref, kw_ref, vw_ref):
    tb = q_ref.shape[1]
    halo = DIL_GROUPS[group][0] // 2
    assert halo == kp_ref.shape[1] and halo == (_DIL_Q // 2) * dil
    for w_ref, p_ref, c_ref, n_ref in ((kw_ref, kp_ref, kc_ref, kn_ref), (vw_ref, vp_ref, vc_ref, vn_ref)):
        w_ref[:, 0:halo] = p_ref[...]
        w_ref[:, halo:halo + tb] = c_ref[...]
        w_ref[:, halo + tb:halo + tb + halo] = n_ref[...]
    t0 = pl.program_id(1) * tb
    shape = (_DIL_Q, 2 * _DIL_Q)
    rel = lax.broadcasted_iota(jnp.int32, shape, 1) - lax.broadcasted_iota(jnp.int32, shape, 0) - _DIL_Q // 2
    absrel = jnp.abs(rel).astype(F32)
    band = jnp.where(absrel <= _DIL_Q // 2, 0.0, -NEG)
    n_slopes = len(DIL_GROUPS) * DIL_HEADS
    bias = [2.0 ** (-8.0 * (group * DIL_HEADS + h + 1) / n_slopes) * dil * LOG2E * absrel + band
            for h in range(DIL_HEADS)]
    col = lax.broadcasted_iota(jnp.int32, (1, 2 * _DIL_Q), 1)
    lo_mask = _lane((_DIL_Q, LANES)) < HALF
    for unit in range(tb // _DIL_Q):
        first = (unit // dil) * _DIL_Q * dil + unit % dil
        rows = pl.ds(first, _DIL_Q, stride=dil)
        win = pl.ds(first, 2 * _DIL_Q, stride=dil)
        key_token = t0 + (first - halo) + dil * col
        off_seq = jnp.where(key_token >= 0, jnp.where(key_token < seq, 0.0, NEG), NEG)
        for s in range(2):
            qs = q_ref[s, rows, :].astype(BF16)
            ksl = kw_ref[s, win, :].astype(BF16)
            vsl = vw_ref[s, win, :].astype(BF16)
            res = []
            for part in range(2):
                keep = lo_mask if part == 0 else jnp.logical_not(lo_mask)
                sc = _scores(jnp.where(keep, qs, jnp.zeros_like(qs)), ksl) - bias[2 * s + part] + off_seq
                m = jnp.max(sc, axis=-1, keepdims=True)
                p = jnp.exp2(sc - m)
                l = jnp.sum(p, axis=-1, keepdims=True)
                o = jnp.dot(p.astype(BF16), vsl, preferred_element_type=F32) / l
                res.append((o, m + jnp.log2(l)))
            o_ref[s, rows, :] = jnp.where(lo_mask, res[0][0], res[1][0])
            lse_ref[s, rows, :] = jnp.where(lo_mask, res[0][1], res[1][1])


def _dilated_group(q, k, v, group):
    batch, _, seq, _ = q.shape
    dil = DIL_GROUPS[group][1]
    halo = DIL_GROUPS[group][0] // 2
    tb = _DIL_BLOCK
    assert seq % tb == 0 and seq // dil >= 2 * _DIL_Q and tb % (_DIL_Q * dil) == 0
    per = tb // halo
    cur = pl.BlockSpec((None, 2, tb, LANES), lambda b, i: (b, 0, i, 0))
    prev = pl.BlockSpec((None, 2, halo, LANES), lambda b, i: (b, 0, jnp.maximum(i * per - 1, 0), 0))
    nxt = pl.BlockSpec((None, 2, halo, LANES),
                       lambda b, i: (b, 0, jnp.minimum((i + 1) * per, seq // halo - 1), 0))
    out_shape = jax.ShapeDtypeStruct((batch, 2, seq, LANES), F32)
    o, lse = pl.pallas_call(
        functools.partial(_dil_kernel, group, dil, seq),
        grid=(batch, seq // tb),
        in_specs=[cur, prev, cur, nxt, prev, cur, nxt],
        out_specs=[cur, cur],
        out_shape=[out_shape, out_shape],
        scratch_shapes=[pltpu.VMEM((2, tb + 2 * halo, LANES), F32)] * 2,
        compiler_params=_cparams(("parallel", "parallel")),
        name=f"dilated_g{group}",
    )(q, k, k, k, v, v, v)
    return o, lse


def _mix_kernel(h_ref, ya_ref, yc_ref, yd_ref, o0_ref, o1_ref, o2_ref, l0_ref, l1_ref, l2_ref,
                wg_ref, bg_ref, wb_ref, wo_ref, lng_ref, lnb_ref, out_ref, *maybe_bf16_ref):
    h = h_ref[...]
    hb = h.astype(BF16)
    slabs = []
    for s in range(2):
        l0, l1, l2 = l0_ref[s], l1_ref[s], l2_ref[s]
        mx = jnp.maximum(jnp.maximum(l0, l1), l2)
        e0, e1, e2 = jnp.exp2(l0 - mx), jnp.exp2(l1 - mx), jnp.exp2(l2 - mx)
        slabs.append(((e0 * o0_ref[s] + e1 * o1_ref[s] + e2 * o2_ref[s]) / (e0 + e1 + e2)).astype(BF16))
    ys = (ya_ref[...], jnp.concatenate(slabs, axis=1), yc_ref[...], yd_ref[...])
    acc = None
    for n in range(N_BRANCH):
        logit = jnp.dot(hb, wg_ref[:, n * D_MODEL:(n + 1) * D_MODEL],
                        preferred_element_type=F32) + bg_ref[n:n + 1, :]
        gate = 1.0 / (1.0 + jnp.exp(-logit))
        term = gate * jnp.dot(ys[n], wb_ref[n], preferred_element_type=F32)
        acc = term if acc is None else acc + term
    m = jnp.dot(acc.astype(BF16), wo_ref[...], preferred_element_type=F32)
    out = _layernorm(DEEPNORM_ALPHA * h + m, lng_ref[...], lnb_ref[...])
    out_ref[...] = out
    for r in maybe_bf16_ref:
        r[...] = out.astype(BF16)


def _mix(h, ya, yc, yd, dil_o, dil_lse, wg, bg, wb, wo, lng, lnb, tm, with_bf16_copy):
    T = h.shape[0]
    nst = dil_o[0].shape[2] // tm
    slab = pl.BlockSpec((None, 2, tm, LANES), lambda i: (i // nst, 0, i % nst, 0))
    row = lambda i: (i, 0)
    const = lambda i: (0, 0)
    wide = pl.BlockSpec((tm, D_MODEL), row)
    narrow = pl.BlockSpec((tm, BRANCH_W), row)
    return pl.pallas_call(
        _mix_kernel,
        grid=(T // tm,),
        in_specs=[wide] + [narrow] * 3 + [slab] * 6 + [
            _resident((D_MODEL, N_BRANCH * D_MODEL), const),
            pl.BlockSpec((N_BRANCH, D_MODEL), const),
            _resident((N_BRANCH, BRANCH_W, D_MODEL), lambda i: (0, 0, 0)),
            _resident((D_MODEL, D_MODEL), const),
            pl.BlockSpec((1, D_MODEL), const),
            pl.BlockSpec((1, D_MODEL), const),
        ],
        out_specs=[wide] * (2 if with_bf16_copy else 1),
        out_shape=[jax.ShapeDtypeStruct((T, D_MODEL), F32)]
        + ([jax.ShapeDtypeStruct((T, D_MODEL), BF16)] if with_bf16_copy else []),
        compiler_params=_cparams(("parallel",)),
        name="gate_mix",
    )(h, ya, yc, yd, *dil_o, *dil_lse, wg, bg, wb, wo, lng, lnb)


def _swiglu_step(xb_ref, w1_ref, w3_ref, w2_ref, acc_ref):
    @pl.when(pl.program_id(1) == 0)
    def _():
        acc_ref[...] = jnp.zeros_like(acc_ref)

    xb = xb_ref[...]
    tf = w1_ref.shape[1]
    mid = (tf // LANES + 1) // 2 * LANES
    total = None
    for lo, hi in ((0, mid), (mid, tf)):
        a = jnp.dot(xb, w1_ref[:, lo:hi].astype(BF16), preferred_element_type=F32)
        b = jnp.dot(xb, w3_ref[:, lo:hi].astype(BF16), preferred_element_type=F32)
        g = (a / (1.0 + jnp.exp(-a)) * b).astype(BF16)
        part = jnp.dot(g, w2_ref[lo:hi, :].astype(BF16), preferred_element_type=F32)
        total = part if total is None else total + part
    acc_ref[...] += total


def _ffn_dense_kernel(x_ref, w1_ref, w3_ref, w2_ref, lng_ref, lnb_ref, o_ref, acc_ref, xb_ref):
    @pl.when(pl.program_id(1) == 0)
    def _():
        xb_ref[...] = x_ref[...].astype(BF16)

    _swiglu_step(xb_ref, w1_ref, w3_ref, w2_ref, acc_ref)

    @pl.when(pl.program_id(1) == pl.num_programs(1) - 1)
    def _():
        o_ref[...] = _layernorm(DEEPNORM_ALPHA * x_ref[...] + acc_ref[...], lng_ref[...], lnb_ref[...])


def _ffn_dense(x, w13, w2, lng, lnb, tm, tf):
    T = x.shape[0]
    nf = D_FF // tf
    return pl.pallas_call(
        _ffn_dense_kernel,
        grid=(T // tm, nf),
        in_specs=[
            pl.BlockSpec((tm, D_MODEL), lambda i, j: (i, 0)),
            pl.BlockSpec((D_MODEL, tf), lambda i, j: (0, j)),
            pl.BlockSpec((D_MODEL, tf), lambda i, j: (0, nf + j)),
            pl.BlockSpec((tf, D_MODEL), lambda i, j: (j, 0)),
            pl.BlockSpec((1, D_MODEL), lambda i, j: (0, 0)),
            pl.BlockSpec((1, D_MODEL), lambda i, j: (0, 0)),
        ],
        out_specs=pl.BlockSpec((tm, D_MODEL), lambda i, j: (i, 0)),
        out_shape=jax.ShapeDtypeStruct((T, D_MODEL), F32),
        scratch_shapes=[pltpu.VMEM((tm, D_MODEL), F32), pltpu.VMEM((tm, D_MODEL), BF16)],
        compiler_params=_cparams(("parallel", "arbitrary")),
        name="ffn_dense",
    )(x, w13, w13, w2, lng, lnb)


def _ffn_expert_kernel(te_ref, used_ref, x_ref, w1_ref, w3_ref, w2_ref, o_ref, acc_ref):
    del te_ref
    live = pl.program_id(0) < used_ref[0]
    last = pl.program_id(1) == pl.num_programs(1) - 1

    @pl.when(live)
    def _():
        _swiglu_step(x_ref, w1_ref, w3_ref, w2_ref, acc_ref)

    @pl.when(live & last)
    def _():
        o_ref[...] = acc_ref[...].astype(o_ref.dtype)

    @pl.when(jnp.logical_not(live) & last)
    def _():
        o_ref[...] = jnp.zeros_like(o_ref)


def _ffn_experts(tile_expert, tiles_used, x_sorted, w13, w2, tm, tf):
    rows = x_sorted.shape[0]
    nf = D_FF // tf
    fblk = lambda i, j, nu: jnp.where(i < nu[0], j, nf - 1)
    grid_spec = pltpu.PrefetchScalarGridSpec(
        num_scalar_prefetch=2,
        grid=(rows // tm, nf),
        in_specs=[
            pl.BlockSpec((tm, D_MODEL), lambda i, j, te, nu: (i, 0)),
            pl.BlockSpec((None, D_MODEL, tf), lambda i, j, te, nu: (te[i], 0, fblk(i, j, nu))),
            pl.BlockSpec((None, D_MODEL, tf), lambda i, j, te, nu: (te[i], 0, nf + fblk(i, j, nu))),
            pl.BlockSpec((None, tf, D_MODEL), lambda i, j, te, nu: (te[i], fblk(i, j, nu), 0)),
        ],
        out_specs=pl.BlockSpec((tm, D_MODEL), lambda i, j, te, nu: (i, 0)),
        scratch_shapes=[pltpu.VMEM((tm, D_MODEL), F32)],
    )
    return pl.pallas_call(
        _ffn_expert_kernel,
        grid_spec=grid_spec,
        out_shape=jax.ShapeDtypeStruct((rows, D_MODEL), BF16),
        compiler_params=_cparams(("parallel", "arbitrary")),
        name="ffn_experts",
    )(tile_expert, tiles_used, x_sorted, w13, w13, w2)


def _router_kernel(x_ref, wr_ref, idx_ref, wt_ref):
    logits = jnp.dot(x_ref[...], wr_ref[...], preferred_element_type=F32,
                     precision=lax.Precision.HIGHEST)
    lane = _lane(logits.shape)
    logits = jnp.where(lane < N_EXPERTS, logits, -jnp.inf)
    v1 = jnp.max(logits, axis=-1, keepdims=True)
    i1 = jnp.min(jnp.where(logits == v1, lane, LANES), axis=-1, keepdims=True)
    rest = jnp.where(lane == i1, -jnp.inf, logits)
    v2 = jnp.max(rest, axis=-1, keepdims=True)
    i2 = jnp.min(jnp.where(rest == v2, lane, LANES), axis=-1, keepdims=True)
    e2 = jnp.exp(v2 - v1)
    w1 = 1.0 / (1.0 + e2)
    idx_ref[...] = jnp.where(lane == 0, i1, jnp.where(lane == 1, i2, 0))
    wt_ref[...] = jnp.where(lane == 0, w1, jnp.where(lane == 1, e2 * w1, 0.0))


def _router(x, w_router, tm):
    T = x.shape[0]
    wr = jnp.pad(w_router, ((0, 0), (0, LANES - N_EXPERTS)))
    row = lambda i: (i, 0)
    return pl.pallas_call(
        _router_kernel,
        grid=(T // tm,),
        in_specs=[pl.BlockSpec((tm, D_MODEL), row), pl.BlockSpec((D_MODEL, LANES), lambda i: (0, 0))],
        out_specs=[pl.BlockSpec((tm, LANES), row), pl.BlockSpec((tm, LANES), row)],
        out_shape=[jax.ShapeDtypeStruct((T, LANES), jnp.int32), jax.ShapeDtypeStruct((T, LANES), F32)],
        compiler_params=_cparams(("parallel",)),
        name="router",
    )(x, wr)


def _moe_out_kernel(h_ref, y0_ref, y1_ref, wt_ref, lng_ref, lnb_ref, o_ref):
    wt = wt_ref[...]
    f = wt[:, 0:1] * y0_ref[...] + wt[:, 1:2] * y1_ref[...]
    o_ref[...] = _layernorm(DEEPNORM_ALPHA * h_ref[...] + f, lng_ref[...], lnb_ref[...])


def _moe_out(h, y0, y1, wt, lng, lnb, tm):
    T = h.shape[0]
    row = lambda i: (i, 0)
    const = lambda i: (0, 0)
    wide = pl.BlockSpec((tm, D_MODEL), row)
    return pl.pallas_call(
        _moe_out_kernel,
        grid=(T // tm,),
        in_specs=[wide, wide, wide, pl.BlockSpec((tm, LANES), row),
                  pl.BlockSpec((1, D_MODEL), const), pl.BlockSpec((1, D_MODEL), const)],
        out_specs=wide,
        out_shape=jax.ShapeDtypeStruct((T, D_MODEL), F32),
        compiler_params=_cparams(("parallel",)),
        name="moe_out",
    )(h, y0, y1, wt, lng, lnb)


def _moe(h, h_bf16, w_router, w13, w2, lng, lnb, tm_router, tm_e, tf):
    T = h.shape[0]
    idx, wt = _router(h, w_router, tm_router)
    e_flat = jnp.concatenate([idx[:, s] for s in range(TOP_K)])
    onehot = (e_flat[None, :] == jnp.arange(N_EXPERTS, dtype=jnp.int32)[:, None]).astype(jnp.int32)
    counts = jnp.sum(onehot, axis=1)
    rank = jnp.sum((jnp.cumsum(onehot, axis=1) - onehot) * onehot, axis=0)
    padded = (counts + tm_e - 1) // tm_e * tm_e
    ends = jnp.cumsum(padded)
    pos = jnp.sum(onehot * (ends - padded)[:, None], axis=0) + rank
    rows = TOP_K * T + N_EXPERTS * tm_e
    token = jnp.arange(TOP_K * T, dtype=jnp.int32) % T
    src = (jnp.arange(rows, dtype=jnp.int32) % T).at[pos].set(
        token, unique_indices=True, mode="promise_in_bounds")
    tile_start = jnp.arange(rows // tm_e, dtype=jnp.int32) * tm_e
    tile_expert = jnp.minimum(jnp.sum((tile_start[:, None] >= ends[None, :]).astype(jnp.int32), axis=1),
                              N_EXPERTS - 1)
    x_sorted = h_bf16.at[src].get(mode="promise_in_bounds")
    tiles_used = (ends[N_EXPERTS - 1:] // tm_e).astype(jnp.int32)
    y = _ffn_experts(tile_expert, tiles_used, x_sorted, w13, w2, tm_e, tf)
    y0 = y.at[pos[:T]].get(mode="promise_in_bounds")
    y1 = y.at[pos[T:]].get(mode="promise_in_bounds")
    return _moe_out(h, y0, y1, wt, lng, lnb, tm_router)


def _rot_cols(w, block):
    d, n = w.shape
    w4 = w.reshape(d, n // block, 2, block // 2)
    return jnp.concatenate([-w4[:, :, 1], w4[:, :, 0]], axis=2).reshape(d, n)


def _attention_weights(w_in):
    o = np.cumsum([0, MLA_Q_LORA, MLA_KV_LORA, MLA_ROPE, _DIL_COLS, GQA_Q_HEADS * HEAD_DIM,
                   2 * GQA_KV_HEADS * HEAD_DIM, _DIFF_COLS, N_BRANCH * D_MODEL])
    cq, ckv, kr, dil, gq, gkv, diff, gate = (w_in[:, o[i]:o[i + 1]] for i in range(8))
    z = lambda n: jnp.zeros((D_MODEL, n), w_in.dtype)
    mla = [cq, z(64), ckv, z(64), kr, z(32), z(64), _rot_cols(kr, MLA_ROPE), z(32)]
    gqa = [gq, _rot_cols(gq, 32), gkv[:, :128], _rot_cols(gkv[:, :128], 32), gkv[:, 128:]]
    small = jnp.concatenate(mla + gqa, axis=1)
    return [w.astype(BF16) for w in (small, diff, dil)], gate.astype(BF16)


def _mla_up_weights(w_uq, w_ukv):
    zq = lambda n: jnp.zeros((MLA_Q_LORA, n), w_uq.dtype)
    per = MLA_NOPE + MLA_ROPE
    plain, rot = [], []
    for h in range(MLA_HEADS):
        nope = w_uq[:, h * per:h * per + MLA_NOPE]
        rope = w_uq[:, h * per + MLA_NOPE:(h + 1) * per]
        plain += [nope, rope, zq(32)]
        rot += [zq(64), _rot_cols(rope, MLA_ROPE), zq(32)]
    wuq = jnp.concatenate(plain + rot, axis=1)
    wuq = jnp.pad(wuq, ((0, _Q_LORA_PAD - MLA_Q_LORA), (0, 0)))
    zk = jnp.zeros((MLA_KV_LORA, 64), w_ukv.dtype)
    per = MLA_NOPE + MLA_V
    ks, vs = [], []
    for h in range(MLA_HEADS):
        ks += [w_ukv[:, h * per:h * per + MLA_NOPE], zk]
        vs += [w_ukv[:, h * per + MLA_NOPE:(h + 1) * per], zk]
    wukv = jnp.concatenate(ks + vs, axis=1)
    return wuq.astype(BF16), wukv.astype(BF16)


def _rope_angles(pos, dim):
    inv = ROPE_THETA ** (-(jnp.arange(0, dim, 2, dtype=F32) / dim))
    return pos[:, None] * inv[None, :]


def _position_tables(seq):
    rows = seq // GRID_W
    row_idx = jnp.repeat(jnp.arange(rows, dtype=F32), GRID_W)
    col_idx = jnp.tile(jnp.arange(GRID_W, dtype=F32), rows)
    a1 = _rope_angles(jnp.arange(seq, dtype=F32), MLA_ROPE)
    ar = _rope_angles(row_idx, HEAD_DIM // 2)
    ac = _rope_angles(col_idx, HEAD_DIM // 2)
    z = lambda n: jnp.zeros((seq, n), F32)
    c1 = jnp.concatenate([jnp.ones((seq, MLA_NOPE), F32)] + [jnp.cos(a1)] * 2 + [z(32)], axis=1)
    s1 = jnp.concatenate([z(MLA_NOPE)] + [jnp.sin(a1)] * 2 + [z(32)], axis=1)
    ca = jnp.concatenate(([jnp.cos(ar)] * 2 + [jnp.cos(ac)] * 2) * 2, axis=1)
    sa = jnp.concatenate(([jnp.sin(ar)] * 2 + [jnp.sin(ac)] * 2) * 2, axis=1)
    return [c1, s1, ca, sa]


def _gqa_gain_rows(q_norm, k_norm):
    def swap(g):
        return jnp.concatenate([g[16:32], g[0:16], g[48:64], g[32:48]])

    sg = HEAD_DIM ** -0.5 * LOG2E
    rows = [sg * q_norm, sg * swap(q_norm), k_norm, swap(k_norm)]
    return jnp.stack([jnp.concatenate([r, r]) for r in rows])


def _pick(n, pref):
    t = min(n, pref)
    assert n % t == 0, (n, pref)
    return t


class _Tiles(NamedTuple):
    rows: int
    flash: int
    diff: int
    ffn_rows: int
    ffn_cols: int


def _plan_tiles(seq, tokens):
    return _Tiles(rows=_pick(seq, 512), flash=_pick(seq, 1024), diff=_pick(seq, 512),
                  ffn_rows=_pick(tokens, 1024), ffn_cols=D_FF // 4)


def kernel(x, ln_emb_g, ln_emb_b, w_in, b_gate, mla_q_norm, mla_kv_norm, mla_w_uq, mla_w_ukv,
           gqa_q_norm, gqa_k_norm, diff_lambda, diff_subln, w_branch, w_out, ln1_g, ln1_b,
           ffn_w13, ffn_w2, moe_router, moe_w13, moe_w2, ln2_g, ln2_b):
    batch, seq, _ = x.shape
    T = batch * seq
    tiles = _plan_tiles(seq, T)
    row = lambda v: v.reshape(1, -1)
    pos_tabs = _position_tables(seq)

    h = x.reshape(T, D_MODEL)
    for l in range(DEPTH):
        lam_init = 0.8 - 0.6 * float(np.exp(-0.3 * l))
        weights, wg = _attention_weights(w_in[l])
        wuq, wukv = _mla_up_weights(mla_w_uq[l], mla_w_ukv[l])
        gains = _gqa_gain_rows(gqa_q_norm[l], gqa_k_norm[l])
        nq = jnp.pad(mla_q_norm[l], (0, _Q_LORA_PAD - MLA_Q_LORA)).reshape(1, _Q_LORA_PAD)
        res = _project(h, batch, seq, l == 0, row(ln_emb_g), row(ln_emb_b), weights, wuq, wukv,
                       nq, row(mla_kv_norm[l]), gains, pos_tabs, tiles.rows)
        if l == 0:
            h, res = res[0], res[1:]
        mq, mk, mv, gq, gk, gv, dq, dk, dv = res[:9]
        dil = res[9:]
        ya = _flash(mq, mk, mv, tiles.flash, tiles.flash).reshape(T, BRANCH_W)
        yc = _flash(gq, gk, gv, tiles.flash, tiles.flash).reshape(T, BRANCH_W)
        sub = jnp.pad(diff_subln[l], (0, LANES - DIFF_V)).reshape(1, LANES)
        yd = _diff_attention(dq, dk, dv, diff_lambda[l], sub, lam_init, tiles.diff).reshape(T, BRANCH_W)
        dil_o, dil_lse = [], []
        for g in range(len(DIL_GROUPS)):
            o, lse = _dilated_group(dil[g], dil[3 + g], dil[6 + g], g)
            dil_o.append(o)
            dil_lse.append(lse)
        dense = l % 2 == 0
        mixed = _mix(h, ya, yc, yd, dil_o, dil_lse, wg, b_gate[l], w_branch[l].astype(BF16),
                     w_out[l].astype(BF16), row(ln1_g[l]), row(ln1_b[l]), tiles.rows, not dense)
        if dense:
            h = _ffn_dense(mixed[0], ffn_w13[l // 2], ffn_w2[l // 2],
                           row(ln2_g[l]), row(ln2_b[l]), tiles.ffn_rows, tiles.ffn_cols)
        else:
            h = _moe(mixed[0], mixed[1], moe_router[l // 2], moe_w13[l // 2], moe_w2[l // 2],
                     row(ln2_g[l]), row(ln2_b[l]), tiles.rows, tiles.ffn_rows, tiles.ffn_cols)
    return h.reshape(batch, seq, D_MODEL)
```

```python
import functools
from typing import NamedTuple

import jax
import jax.numpy as jnp
import numpy as np
from jax import lax
from jax.experimental import pallas as pl
from jax.experimental.pallas import tpu as pltpu

D_MODEL = 1024
GRID_W = 64
HEAD_DIM = 64
N_BRANCH = 4
BRANCH_W = 256
MLA_HEADS = 4
MLA_NOPE = 64
MLA_ROPE = 32
MLA_V = 64
MLA_Q_LORA = 192
MLA_KV_LORA = 128
DIL_GROUPS = ((128, 1), (512, 4), (2048, 16))
DIL_HEADS = 4
GQA_Q_HEADS = 4
GQA_KV_HEADS = 2
DIFF_HEADS = 4
DIFF_D = 32
DIFF_V = 64
D_FF = 3584
N_EXPERTS = 8
TOP_K = 2
ROPE_THETA = 10000.0
LN_EPS = 1e-5
RMS_EPS = 1e-6
DIFF_NORM_EPS = 1e-5
DEPTH = 2
DEEPNORM_ALPHA = (2 * DEPTH) ** 0.25

LANES = 128
HALF = LANES // 2
MXU_COLS = 256
VMEM_BYTES = 64 * 1024 * 1024
VMEM_LIMIT = VMEM_BYTES - 8 * 1024 * 1024

LOG2E = 1.4426950408889634
NEG = -1e30
BF16 = jnp.bfloat16
F32 = jnp.float32

_Q_LORA_PAD = 2 * LANES
_MLA_COLS = _Q_LORA_PAD + MLA_KV_LORA + 2 * LANES
_GQA_COLS = 7 * LANES
_DIFF_COLS = 3 * DIFF_HEADS * DIFF_V
_DIL_COLS = 3 * len(DIL_GROUPS) * BRANCH_W
_SMALL_COLS = _MLA_COLS + _GQA_COLS


def _cparams(sem):
    return pltpu.CompilerParams(dimension_semantics=sem, vmem_limit_bytes=VMEM_LIMIT)


def _resident(shape, index_map):
    return pl.BlockSpec(shape, index_map, pipeline_mode=pl.Buffered(1))


def _layernorm(x, g, b):
    mu = jnp.mean(x, axis=-1, keepdims=True)
    xc = x - mu
    var = jnp.mean(xc * xc, axis=-1, keepdims=True)
    return xc * lax.rsqrt(var + LN_EPS) * g + b


def _lane(shape):
    return lax.broadcasted_iota(jnp.int32, shape, len(shape) - 1)


def _split_pair(x):
    lo_mask = _lane(x.shape) < HALF
    lo = jnp.where(lo_mask, x, 0.0)
    hi = jnp.where(lo_mask, pltpu.roll(x, HALF, axis=1), 0.0)
    return lo, hi


def _join_pair(lo, hi):
    return jnp.where(_lane(lo.shape) < HALF, lo, pltpu.roll(hi, HALF, axis=1))


def _with_ones(v):
    return jnp.where(_lane(v.shape) == HALF, 1.0, v)


def _diff_slope(h):
    return 2.0 ** (-8.0 * (h + 1) / DIFF_HEADS) * LOG2E


def _split3(val):
    f1 = val.astype(BF16).astype(F32)
    r1 = val - f1
    f2 = r1.astype(BF16).astype(F32)
    f3 = (r1 - f2).astype(BF16).astype(F32)
    return f1, f2, f3


_FEAT = 2 * DIFF_D


def _key_bias_features(val):
    f1, f2, f3 = _split3(val)
    lane = _lane(val.shape)
    ones = jnp.where((lane >= _FEAT + 3) & (lane < _FEAT + 6), 1.0, 0.0)
    return jnp.where(lane == _FEAT, f1, jnp.where(lane == _FEAT + 1, f2, jnp.where(lane == _FEAT + 2, f3, ones)))


def _proj_kernel(apply_ln, tiles_per_seq, *refs):
    (x_ref, lng_ref, lnb_ref, ws_ref, wdiff_ref, wdil_ref, wuq_ref, wukv_ref, nq_ref, nkv_ref,
     gains_ref, c1_ref, s1_ref, ca_ref, sa_ref) = refs[:15]
    outs = refs[15:]
    if apply_ln:
        h_ref, outs = outs[0], outs[1:]
    (mq_ref, mk_ref, mv_ref, gq_ref, gk_ref, gv_ref, dq_ref, dk_ref, dv_ref) = outs[:9]
    dil_refs = outs[9:]

    x = x_ref[...]
    if apply_ln:
        x = _layernorm(x, lng_ref[...], lnb_ref[...])
        h_ref[...] = x
    hb = x.astype(BF16)

    def proj(w_ref, off, n):
        return jnp.dot(hb, w_ref[:, off:off + n], preferred_element_type=F32)

    p_small = proj(ws_ref, 0, _SMALL_COLS)
    p = p_small[:, :_MLA_COLS]
    cq, ckv, kr, krr = p[:, 0:256], p[:, 256:384], p[:, 384:512], p[:, 512:640]
    rq = lax.rsqrt(jnp.sum(cq * cq, axis=-1, keepdims=True) * (1.0 / MLA_Q_LORA) + RMS_EPS)
    cqn = (cq * rq * nq_ref[...]).astype(BF16)
    q2 = jnp.dot(cqn, wuq_ref[...], preferred_element_type=F32)
    rkv = lax.rsqrt(jnp.sum(ckv * ckv, axis=-1, keepdims=True) * (1.0 / MLA_KV_LORA) + RMS_EPS)
    ckvn = (ckv * rkv * nkv_ref[...]).astype(BF16)
    kv2 = jnp.dot(ckvn, wukv_ref[...], preferred_element_type=F32)
    c1, s1 = c1_ref[...], s1_ref[...]
    k_rope = kr * c1 + krr * s1
    mla_scale = (MLA_NOPE + MLA_ROPE) ** -0.5 * LOG2E
    cq_t, sq_t = c1 * mla_scale, s1 * mla_scale
    for h in range(MLA_HEADS):
        qh = q2[:, h * LANES:(h + 1) * LANES]
        qrh = q2[:, (MLA_HEADS + h) * LANES:(MLA_HEADS + h + 1) * LANES]
        mq_ref[h] = (qh * cq_t + qrh * sq_t).astype(BF16)
        mk_ref[h] = (kv2[:, h * LANES:(h + 1) * LANES] + k_rope).astype(BF16)
        vh = kv2[:, (MLA_HEADS + h) * LANES:(MLA_HEADS + h + 1) * LANES]
        mv_ref[h] = _with_ones(vh).astype(BF16)

    p = p_small[:, _MLA_COLS:]

    def head_rms(xs):
        ss = xs * xs
        lo_mask = _lane(xs.shape) < HALF
        s_all = jnp.sum(ss, axis=-1, keepdims=True)
        s_lo = jnp.sum(jnp.where(lo_mask, ss, 0.0), axis=-1, keepdims=True)
        return lax.rsqrt(jnp.where(lo_mask, s_lo, s_all - s_lo) * (1.0 / HEAD_DIM) + RMS_EPS)

    ca, sa = ca_ref[...], sa_ref[...]
    aq, bq = ca * gains_ref[0:1, :], sa * gains_ref[1:2, :]
    ak, bk = ca * gains_ref[2:3, :], sa * gains_ref[3:4, :]
    for s in range(2):
        qs = p[:, s * LANES:(s + 1) * LANES]
        qrs = p[:, 256 + s * LANES:256 + (s + 1) * LANES]
        qn = head_rms(qs) * (qs * aq + qrs * bq)
        lo, hi = _split_pair(qn)
        gq_ref[2 * s] = lo.astype(BF16)
        gq_ref[2 * s + 1] = hi.astype(BF16)
    ks, krs = p[:, 512:640], p[:, 640:768]
    kn = head_rms(ks) * (ks * ak + krs * bk)
    lo, hi = _split_pair(kn)
    gk_ref[0] = lo.astype(BF16)
    gk_ref[1] = hi.astype(BF16)
    lo, hi = _split_pair(p[:, 768:896])
    gv_ref[0] = _with_ones(lo).astype(BF16)
    gv_ref[1] = _with_ones(hi).astype(BF16)

    p = proj(wdiff_ref, 0, _DIFF_COLS)
    dscale = DIFF_D ** -0.5 * LOG2E
    tm = x.shape[0]
    pos = (pl.program_id(0) % tiles_per_seq * tm
           + lax.broadcasted_iota(jnp.int32, (tm, LANES), 0)).astype(F32)
    for s in range(2):
        lo, hi = _split_pair(p[:, s * LANES:(s + 1) * LANES] * dscale)
        dq_ref[2 * s] = lo.astype(BF16)
        dq_ref[2 * s + 1] = hi.astype(BF16)
        lo, hi = _split_pair(p[:, 256 + s * LANES:256 + (s + 1) * LANES])
        dk_ref[2 * s] = (lo + _key_bias_features(_diff_slope(2 * s) * pos)).astype(BF16)
        dk_ref[2 * s + 1] = (hi + _key_bias_features(_diff_slope(2 * s + 1) * pos)).astype(BF16)
        lo, hi = _split_pair(p[:, 512 + s * LANES:512 + (s + 1) * LANES])
        dv_ref[2 * s] = _with_ones(lo).astype(BF16)
        dv_ref[2 * s + 1] = _with_ones(hi).astype(BF16)

    lscale = HEAD_DIM ** -0.5 * LOG2E
    for i in range(9):
        seg = proj(wdil_ref, i * BRANCH_W, BRANCH_W)
        if i < 3:
            seg = seg * lscale
        dil_refs[i][0] = seg[:, :LANES]
        dil_refs[i][1] = seg[:, LANES:]


def _project(x2d, batch, seq, apply_ln, lng, lnb, weights, wuq, wukv, nq, nkv, gains, tables, tm):
    T = x2d.shape[0]
    nst = seq // tm
    grid = (T // tm,)
    row = lambda i: (i, 0)
    const = lambda i: (0, 0)
    tab = lambda i: (i % nst, 0)
    head = lambda i: (i // nst, 0, i % nst, 0)

    in_specs = [
        pl.BlockSpec((tm, D_MODEL), row),
        pl.BlockSpec((1, D_MODEL), const),
        pl.BlockSpec((1, D_MODEL), const),
        _resident((D_MODEL, _SMALL_COLS), const),
        _resident((D_MODEL, _DIFF_COLS), const),
        _resident((D_MODEL, _DIL_COLS), const),
        _resident((_Q_LORA_PAD, 8 * LANES), const),
        _resident((LANES, 8 * LANES), const),
        pl.BlockSpec((1, _Q_LORA_PAD), const),
        pl.BlockSpec((1, LANES), const),
        pl.BlockSpec((4, LANES), const),
    ] + [pl.BlockSpec((tm, LANES), tab)] * 4

    def heads_out(n):
        return (jax.ShapeDtypeStruct((batch, n, seq, LANES), BF16),
                pl.BlockSpec((None, n, tm, LANES), head))

    outs = []
    if apply_ln:
        outs.append((jax.ShapeDtypeStruct((T, D_MODEL), F32), pl.BlockSpec((tm, D_MODEL), row)))
    outs += [heads_out(MLA_HEADS)] * 3
    outs += [heads_out(GQA_Q_HEADS), heads_out(GQA_KV_HEADS), heads_out(GQA_KV_HEADS)]
    outs += [heads_out(DIFF_HEADS)] * 3
    outs += [(jax.ShapeDtypeStruct((batch, 2, seq, LANES), F32), pl.BlockSpec((None, 2, tm, LANES), head))] * 9

    res = pl.pallas_call(
        functools.partial(_proj_kernel, apply_ln, nst),
        grid=grid,
        in_specs=in_specs,
        out_specs=[o[1] for o in outs],
        out_shape=[o[0] for o in outs],
        compiler_params=_cparams(("parallel",)),
        name="proj_prep",
    )(x2d, lng, lnb, *weights, wuq, wukv, nq, nkv, gains, *tables)
    return res


def _normalize(acc):
    return acc / acc[:, HALF:HALF + 1]


def _scores(q, k):
    return lax.dot_general(q, k, (((1,), (1,)), ((), ())), preferred_element_type=F32)


def _softmax_update(s, v, acc_ref, m_ref, idx):
    blocks = [s[:, j * LANES:(j + 1) * LANES] for j in range(s.shape[1] // LANES)]
    part = functools.reduce(jnp.maximum, blocks)
    m_prev = m_ref[idx]
    m_new = jnp.maximum(m_prev, jnp.max(part, axis=-1, keepdims=True))
    p = jnp.concatenate([jnp.exp2((b - m_new).astype(BF16)) for b in blocks], axis=1)
    acc_ref[idx] = (jnp.exp2(m_prev - m_new) * acc_ref[idx]
                    + jnp.dot(p, v, preferred_element_type=F32))
    m_ref[idx] = m_new


def _flash_kernel(n_rep, tk, q_ref, k_ref, v_ref, o_ref, acc_ref, m_ref):
    n_heads = q_ref.shape[0]
    seq = k_ref.shape[1]
    acc_ref[...] = jnp.zeros_like(acc_ref)
    m_ref[...] = jnp.full_like(m_ref, NEG)

    def body(c, carry):
        ks = pl.multiple_of(c * tk, tk)
        scores = [_scores(q_ref[h], k_ref[h // n_rep, pl.ds(ks, tk), :]) for h in range(n_heads)]
        for h in range(n_heads):
            _softmax_update(scores[h], v_ref[h // n_rep, pl.ds(ks, tk), :], acc_ref, m_ref, h)
        return carry

    lax.fori_loop(0, seq // tk, body, 0)
    outs = [_normalize(acc_ref[h]) for h in range(n_heads)]
    for s in range(n_heads // 2):
        o_ref[:, s * LANES:(s + 1) * LANES] = _join_pair(outs[2 * s], outs[2 * s + 1]).astype(o_ref.dtype)


def _flash(q, k, v, tq, tk):
    batch, n_heads, seq, _ = q.shape
    n_kv = k.shape[1]
    return pl.pallas_call(
        functools.partial(_flash_kernel, n_heads // n_kv, tk),
        grid=(batch, seq // tq),
        in_specs=[
            pl.BlockSpec((None, n_heads, tq, LANES), lambda b, i: (b, 0, i, 0)),
            _resident((None, n_kv, seq, LANES), lambda b, i: (b, 0, 0, 0)),
            _resident((None, n_kv, seq, LANES), lambda b, i: (b, 0, 0, 0)),
        ],
        out_specs=pl.BlockSpec((None, tq, n_heads * HALF), lambda b, i: (b, i, 0)),
        out_shape=jax.ShapeDtypeStruct((batch, seq, n_heads * HALF), BF16),
        scratch_shapes=[pltpu.VMEM((n_heads, tq, LANES), F32), pltpu.VMEM((n_heads, tq, LANES), F32)],
        compiler_params=_cparams(("parallel", "parallel")),
        name="flash_attn",
    )(q, k, v)


def _diff_kernel(lam_init, q_ref, k_ref, v_ref, lp_ref, sub_ref, o_ref, acc_ref, m_ref, qv_ref):
    n_heads, tq, _ = q_ref.shape
    seq = k_ref.shape[1]
    tile = pl.program_id(1)
    t0 = tile * tq
    lp = lp_ref[...]
    lam = (jnp.exp(jnp.sum(lp[0:1] * lp[1:2], keepdims=True))
           - jnp.exp(jnp.sum(lp[2:3] * lp[3:4], keepdims=True)) + lam_init)
    lane = _lane((tq, LANES))
    tpos = (t0 + lax.broadcasted_iota(jnp.int32, (tq, LANES), 0)).astype(F32)
    in_f = (lane >= _FEAT) & (lane < _FEAT + 3)
    for h in range(n_heads):
        q = q_ref[h].astype(F32)
        g1, g2, g3 = _split3(2.0 * _diff_slope(h) * tpos)
        left = jnp.where(in_f, 1.0, 0.0)
        right = jnp.where(in_f, -1.0, jnp.where(lane == _FEAT + 3, g1, jnp.where(
            lane == _FEAT + 4, g2, jnp.where(lane == _FEAT + 5, g3, 0.0))))
        for m in range(2):
            qm = jnp.where((lane >= m * DIFF_D) & (lane < (m + 1) * DIFF_D), q, 0.0)
            qv_ref[0, 2 * h + m] = (qm + left).astype(BF16)
            qv_ref[1, 2 * h + m] = (qm + right).astype(BF16)
    acc_ref[...] = jnp.zeros_like(acc_ref)
    m_ref[...] = jnp.full_like(m_ref, NEG)
    chains = range(2 * n_heads)

    def update(scores, ks):
        for i in chains:
            _softmax_update(scores[i], v_ref[i // 2, pl.ds(ks, tq), :], acc_ref, m_ref, i)

    ks = pl.multiple_of(t0, tq)
    update([jnp.minimum(_scores(qv_ref[0, i], k_ref[i // 2, pl.ds(ks, tq), :]),
                        _scores(qv_ref[1, i], k_ref[i // 2, pl.ds(ks, tq), :])) for i in chains], ks)

    def body(c, carry):
        side = (c >= tile).astype(jnp.int32)
        ks = pl.multiple_of((c + side) * tq, tq)
        update([_scores(qv_ref[side, i], k_ref[i // 2, pl.ds(ks, tq), :]) for i in chains], ks)
        return carry

    lax.fori_loop(0, seq // tq - 1, body, 0)

    outs = []
    for h in range(n_heads):
        o = _normalize(acc_ref[2 * h]) - lam * _normalize(acc_ref[2 * h + 1])
        o = jnp.where(lane < DIFF_V, o, 0.0)
        ms = jnp.sum(o * o, axis=-1, keepdims=True) * (1.0 / DIFF_V)
        outs.append(o * lax.rsqrt(ms + DIFF_NORM_EPS) * sub_ref[...] * (1.0 - lam_init))
    for s in range(n_heads // 2):
        o_ref[:, s * LANES:(s + 1) * LANES] = _join_pair(outs[2 * s], outs[2 * s + 1]).astype(o_ref.dtype)


def _diff_attention(q, k, v, lam_params, subln, lam_init, tq):
    batch, n_heads, seq, _ = q.shape
    return pl.pallas_call(
        functools.partial(_diff_kernel, lam_init),
        grid=(batch, seq // tq),
        in_specs=[
            pl.BlockSpec((None, n_heads, tq, LANES), lambda b, i: (b, 0, i, 0)),
            _resident((None, n_heads, seq, LANES), lambda b, i: (b, 0, 0, 0)),
            _resident((None, n_heads, seq, LANES), lambda b, i: (b, 0, 0, 0)),
            pl.BlockSpec((4, DIFF_D), lambda b, i: (0, 0)),
            pl.BlockSpec((1, LANES), lambda b, i: (0, 0)),
        ],
        out_specs=pl.BlockSpec((None, tq, n_heads * HALF), lambda b, i: (b, i, 0)),
        out_shape=jax.ShapeDtypeStruct((batch, seq, n_heads * HALF), BF16),
        scratch_shapes=[pltpu.VMEM((2 * n_heads, tq, LANES), F32)] * 2
        + [pltpu.VMEM((2, 2 * n_heads, tq, LANES), BF16)],
        compiler_params=_cparams(("parallel", "parallel")),
        name="diff_attn",
    )(q, k, v, lam_params, subln)


_DIL_BLOCK = 2048
_DIL_Q = 128


def _dil_kernel(group, dil, seq, q_ref, kp_ref, kc_ref, kn_ref, vp_ref, vc_ref, vn_ref,
                o_ref, lse_ref, kw_ref, vw_ref):
    tb = q_ref.shape[1]
    halo = DIL_GROUPS[group][0] // 2
    assert halo == kp_ref.shape[1] and halo == (_DIL_Q // 2) * dil
    for w_ref, p_ref, c_ref, n_ref in ((kw_ref, kp_ref, kc_ref, kn_ref), (vw_ref, vp_ref, vc_ref, vn_ref)):
        w_ref[:, 0:halo] = p_ref[...]
        w_ref[:, halo:halo + tb] = c_ref[...]
        w_ref[:, halo + tb:halo + tb + halo] = n_ref[...]
    t0 = pl.program_id(1) * tb
    shape = (_DIL_Q, 2 * _DIL_Q)
    rel = lax.broadcasted_iota(jnp.int32, shape, 1) - lax.broadcasted_iota(jnp.int32, shape, 0) - _DIL_Q // 2
    absrel = jnp.abs(rel).astype(F32)
    band = jnp.where(absrel <= _DIL_Q // 2, 0.0, -NEG)
    n_slopes = len(DIL_GROUPS) * DIL_HEADS
    bias = [2.0 ** (-8.0 * (group * DIL_HEADS + h + 1) / n_slopes) * dil * LOG2E * absrel + band
            for h in range(DIL_HEADS)]
    col = lax.broadcasted_iota(jnp.int32, (1, 2 * _DIL_Q), 1)
    lo_mask = _lane((_DIL_Q, LANES)) < HALF
    for unit in range(tb // _DIL_Q):
        first = (unit // dil) * _DIL_Q * dil + unit % dil
        rows = pl.ds(first, _DIL_Q, stride=dil)
        win = pl.ds(first, 2 * _DIL_Q, stride=dil)
        key_token = t0 + (first - halo) + dil * col
        off_seq = jnp.where(key_token >= 0, jnp.where(key_token < seq, 0.0, NEG), NEG)
        for s in range(2):
            qs = q_ref[s, rows, :].astype(BF16)
            ksl = kw_ref[s, win, :].astype(BF16)
            vsl = vw_ref[s, win, :].astype(BF16)
            res = []
            for part in range(2):
                keep = lo_mask if part == 0 else jnp.logical_not(lo_mask)
                sc = _scores(jnp.where(keep, qs, jnp.zeros_like(qs)), ksl) - bias[2 * s + part] + off_seq
                m = jnp.max(sc, axis=-1, keepdims=True)
                p = jnp.exp2(sc - m)
                l = jnp.sum(p, axis=-1, keepdims=True)
                o = jnp.dot(p.astype(BF16), vsl, preferred_element_type=F32) / l
                res.append((o, m + jnp.log2(l)))
            o_ref[s, rows, :] = jnp.where(lo_mask, res[0][0], res[1][0])
            lse_ref[s, rows, :] = jnp.where(lo_mask, res[0][1], res[1][1])


def _dilated_group(q, k, v, group):
    batch, _, seq, _ = q.shape
    dil = DIL_GROUPS[group][1]
    halo = DIL_GROUPS[group][0] // 2
    tb = _DIL_BLOCK
    assert seq % tb == 0 and seq // dil >= 2 * _DIL_Q and tb % (_DIL_Q * dil) == 0
    per = tb // halo
    cur = pl.BlockSpec((None, 2, tb, LANES), lambda b, i: (b, 0, i, 0))
    prev = pl.BlockSpec((None, 2, halo, LANES), lambda b, i: (b, 0, jnp.maximum(i * per - 1, 0), 0))
    nxt = pl.BlockSpec((None, 2, halo, LANES),
                       lambda b, i: (b, 0, jnp.minimum((i + 1) * per, seq // halo - 1), 0))
    out_shape = jax.ShapeDtypeStruct((batch, 2, seq, LANES), F32)
    o, lse = pl.pallas_call(
        functools.partial(_dil_kernel, group, dil, seq),
        grid=(batch, seq // tb),
        in_specs=[cur, prev, cur, nxt, prev, cur, nxt],
        out_specs=[cur, cur],
        out_shape=[out_shape, out_shape],
        scratch_shapes=[pltpu.VMEM((2, tb + 2 * halo, LANES), F32)] * 2,
        compiler_params=_cparams(("parallel", "parallel")),
        name=f"dilated_g{group}",
    )(q, k, k, k, v, v, v)
    return o, lse


def _mix_kernel(h_ref, ya_ref, yc_ref, yd_ref, o0_ref, o1_ref, o2_ref, l0_ref, l1_ref, l2_ref,
                wg_ref, bg_ref, wb_ref, wo_ref, lng_ref, lnb_ref, out_ref, *maybe_bf16_ref):
    h = h_ref[...]
    hb = h.astype(BF16)
    slabs = []
    for s in range(2):
        l0, l1, l2 = l0_ref[s], l1_ref[s], l2_ref[s]
        mx = jnp.maximum(jnp.maximum(l0, l1), l2)
        e0, e1, e2 = jnp.exp2(l0 - mx), jnp.exp2(l1 - mx), jnp.exp2(l2 - mx)
        slabs.append(((e0 * o0_ref[s] + e1 * o1_ref[s] + e2 * o2_ref[s]) / (e0 + e1 + e2)).astype(BF16))
    ys = (ya_ref[...], jnp.concatenate(slabs, axis=1), yc_ref[...], yd_ref[...])
    acc = None
    for n in range(N_BRANCH):
        logit = jnp.dot(hb, wg_ref[:, n * D_MODEL:(n + 1) * D_MODEL],
                        preferred_element_type=F32) + bg_ref[n:n + 1, :]
        gate = 1.0 / (1.0 + jnp.exp(-logit))
        term = gate * jnp.dot(ys[n], wb_ref[n], preferred_element_type=F32)
        acc = term if acc is None else acc + term
    m = jnp.dot(acc.astype(BF16), wo_ref[...], preferred_element_type=F32)
    out = _layernorm(DEEPNORM_ALPHA * h + m, lng_ref[...], lnb_ref[...])
    out_ref[...] = out
    for r in maybe_bf16_ref:
        r[...] = out.astype(BF16)


def _mix(h, ya, yc, yd, dil_o, dil_lse, wg, bg, wb, wo, lng, lnb, tm, with_bf16_copy):
    T = h.shape[0]
    nst = dil_o[0].shape[2] // tm
    slab = pl.BlockSpec((None, 2, tm, LANES), lambda i: (i // nst, 0, i % nst, 0))
    row = lambda i: (i, 0)
    const = lambda i: (0, 0)
    wide = pl.BlockSpec((tm, D_MODEL), row)
    narrow = pl.BlockSpec((tm, BRANCH_W), row)
    return pl.pallas_call(
        _mix_kernel,
        grid=(T // tm,),
        in_specs=[wide] + [narrow] * 3 + [slab] * 6 + [
            _resident((D_MODEL, N_BRANCH * D_MODEL), const),
            pl.BlockSpec((N_BRANCH, D_MODEL), const),
            _resident((N_BRANCH, BRANCH_W, D_MODEL), lambda i: (0, 0, 0)),
            _resident((D_MODEL, D_MODEL), const),
            pl.BlockSpec((1, D_MODEL), const),
            pl.BlockSpec((1, D_MODEL), const),
        ],
        out_specs=[wide] * (2 if with_bf16_copy else 1),
        out_shape=[jax.ShapeDtypeStruct((T, D_MODEL), F32)]
        + ([jax.ShapeDtypeStruct((T, D_MODEL), BF16)] if with_bf16_copy else []),
        compiler_params=_cparams(("parallel",)),
        name="gate_mix",
    )(h, ya, yc, yd, *dil_o, *dil_lse, wg, bg, wb, wo, lng, lnb)


def _swiglu_step(xb_ref, w1_ref, w3_ref, w2_ref, acc_ref):
    @pl.when(pl.program_id(1) == 0)
    def _():
        acc_ref[...] = jnp.zeros_like(acc_ref)

    xb = xb_ref[...]
    tf = w1_ref.shape[1]
    full = tf // MXU_COLS * MXU_COLS
    mid = (full // MXU_COLS + 1) // 2 * MXU_COLS
    gated = []

    def gate(a, b):
        gated.append((a / (1.0 + jnp.exp(-a)) * b).astype(BF16))

    for lo, hi in ((0, mid), (mid, full)):
        gate(jnp.dot(xb, w1_ref[:, lo:hi].astype(BF16), preferred_element_type=F32),
             jnp.dot(xb, w3_ref[:, lo:hi].astype(BF16), preferred_element_type=F32))
    if full < tf:
        w13 = jnp.concatenate([w1_ref[:, full:tf].astype(BF16), w3_ref[:, full:tf].astype(BF16)], axis=1)
        ab = jnp.dot(xb, w13, preferred_element_type=F32)
        gate(ab[:, :tf - full], ab[:, tf - full:])
    acc_ref[...] += jnp.dot(jnp.concatenate(gated, axis=1), w2_ref[...].astype(BF16),
                            preferred_element_type=F32)


def _ffn_dense_kernel(x_ref, w1_ref, w3_ref, w2_ref, lng_ref, lnb_ref, o_ref, acc_ref, xb_ref):
    @pl.when(pl.program_id(1) == 0)
    def _():
        xb_ref[...] = x_ref[...].astype(BF16)

    _swiglu_step(xb_ref, w1_ref, w3_ref, w2_ref, acc_ref)

    @pl.when(pl.program_id(1) == pl.num_programs(1) - 1)
    def _():
        o_ref[...] = _layernorm(DEEPNORM_ALPHA * x_ref[...] + acc_ref[...], lng_ref[...], lnb_ref[...])


def _ffn_dense(x, w13, w2, lng, lnb, tm, tf):
    T = x.shape[0]
    nf = D_FF // tf
    return pl.pallas_call(
        _ffn_dense_kernel,
        grid=(T // tm, nf),
        in_specs=[
            pl.BlockSpec((tm, D_MODEL), lambda i, j: (i, 0)),
            pl.BlockSpec((D_MODEL, tf), lambda i, j: (0, j)),
            pl.BlockSpec((D_MODEL, tf), lambda i, j: (0, nf + j)),
            pl.BlockSpec((tf, D_MODEL), lambda i, j: (j, 0)),
            pl.BlockSpec((1, D_MODEL), lambda i, j: (0, 0)),
            pl.BlockSpec((1, D_MODEL), lambda i, j: (0, 0)),
        ],
        out_specs=pl.BlockSpec((tm, D_MODEL), lambda i, j: (i, 0)),
        out_shape=jax.ShapeDtypeStruct((T, D_MODEL), F32),
        scratch_shapes=[pltpu.VMEM((tm, D_MODEL), F32), pltpu.VMEM((tm, D_MODEL), BF16)],
        compiler_params=_cparams(("parallel", "arbitrary")),
        name="ffn_dense",
    )(x, w13, w13, w2, lng, lnb)


def _ffn_expert_kernel(te_ref, used_ref, x_ref, w1_ref, w3_ref, w2_ref, o_ref, acc_ref):
    del te_ref
    live = pl.program_id(0) < used_ref[0]
    last = pl.program_id(1) == pl.num_programs(1) - 1

    @pl.when(live)
    def _():
        _swiglu_step(x_ref, w1_ref, w3_ref, w2_ref, acc_ref)

    @pl.when(live & last)
    def _():
        o_ref[...] = acc_ref[...].astype(o_ref.dtype)

    @pl.when(jnp.logical_not(live) & last)
    def _():
        o_ref[...] = jnp.zeros_like(o_ref)


def _ffn_experts(tile_expert, tiles_used, x_sorted, w13, w2, tm, tf):
    rows = x_sorted.shape[0]
    nf = D_FF // tf
    fblk = lambda i, j, nu: jnp.where(i < nu[0], j, nf - 1)
    grid_spec = pltpu.PrefetchScalarGridSpec(
        num_scalar_prefetch=2,
        grid=(rows // tm, nf),
        in_specs=[
            pl.BlockSpec((tm, D_MODEL), lambda i, j, te, nu: (i, 0)),
            pl.BlockSpec((None, D_MODEL, tf), lambda i, j, te, nu: (te[i], 0, fblk(i, j, nu))),
            pl.BlockSpec((None, D_MODEL, tf), lambda i, j, te, nu: (te[i], 0, nf + fblk(i, j, nu))),
            pl.BlockSpec((None, tf, D_MODEL), lambda i, j, te, nu: (te[i], fblk(i, j, nu), 0)),
        ],
        out_specs=pl.BlockSpec((tm, D_MODEL), lambda i, j, te, nu: (i, 0)),
        scratch_shapes=[pltpu.VMEM((tm, D_MODEL), F32)],
    )
    return pl.pallas_call(
        _ffn_expert_kernel,
        grid_spec=grid_spec,
        out_shape=jax.ShapeDtypeStruct((rows, D_MODEL), BF16),
        compiler_params=_cparams(("parallel", "arbitrary")),
        name="ffn_experts",
    )(tile_expert, tiles_used, x_sorted, w13, w13, w2)


def _router_kernel(x_ref, wr_ref, idx_ref, wt_ref):
    logits = jnp.dot(x_ref[...], wr_ref[...], preferred_element_type=F32,
                     precision=lax.Precision.HIGHEST)
    lane = _lane(logits.shape)
    logits = jnp.where(lane < N_EXPERTS, logits, -jnp.inf)
    v1 = jnp.max(logits, axis=-1, keepdims=True)
    i1 = jnp.min(jnp.where(logits == v1, lane, LANES), axis=-1, keepdims=True)
    rest = jnp.where(lane == i1, -jnp.inf, logits)
    v2 = jnp.max(rest, axis=-1, keepdims=True)
    i2 = jnp.min(jnp.where(rest == v2, lane, LANES), axis=-1, keepdims=True)
    e2 = jnp.exp(v2 - v1)
    w1 = 1.0 / (1.0 + e2)
    idx_ref[...] = jnp.where(lane == 0, i1, jnp.where(lane == 1, i2, 0))
    wt_ref[...] = jnp.where(lane == 0, w1, jnp.where(lane == 1, e2 * w1, 0.0))


def _router(x, w_router, tm):
    T = x.shape[0]
    wr = jnp.pad(w_router, ((0, 0), (0, LANES - N_EXPERTS)))
    row = lambda i: (i, 0)
    return pl.pallas_call(
        _router_kernel,
        grid=(T // tm,),
        in_specs=[pl.BlockSpec((tm, D_MODEL), row), pl.BlockSpec((D_MODEL, LANES), lambda i: (0, 0))],
        out_specs=[pl.BlockSpec((tm, LANES), row), pl.BlockSpec((tm, LANES), row)],
        out_shape=[jax.ShapeDtypeStruct((T, LANES), jnp.int32), jax.ShapeDtypeStruct((T, LANES), F32)],
        compiler_params=_cparams(("parallel",)),
        name="router",
    )(x, wr)


def _moe_out_kernel(h_ref, y0_ref, y1_ref, wt_ref, lng_ref, lnb_ref, o_ref):
    wt = wt_ref[...]
    f = wt[:, 0:1] * y0_ref[...] + wt[:, 1:2] * y1_ref[...]
    o_ref[...] = _layernorm(DEEPNORM_ALPHA * h_ref[...] + f, lng_ref[...], lnb_ref[...])


def _moe_out(h, y0, y1, wt, lng, lnb, tm):
    T = h.shape[0]
    row = lambda i: (i, 0)
    const = lambda i: (0, 0)
    wide = pl.BlockSpec((tm, D_MODEL), row)
    return pl.pallas_call(
        _moe_out_kernel,
        grid=(T // tm,),
        in_specs=[wide, wide, wide, pl.BlockSpec((tm, LANES), row),
                  pl.BlockSpec((1, D_MODEL), const), pl.BlockSpec((1, D_MODEL), const)],
        out_specs=wide,
        out_shape=jax.ShapeDtypeStruct((T, D_MODEL), F32),
        compiler_params=_cparams(("parallel",)),
        name="moe_out",
    )(h, y0, y1, wt, lng, lnb)


def _moe(h, h_bf16, w_router, w13, w2, lng, lnb, tm_router, tm_e, tf):
    T = h.shape[0]
    idx, wt = _router(h, w_router, tm_router)
    e_flat = jnp.concatenate([idx[:, s] for s in range(TOP_K)])
    onehot = (e_flat[None, :] == jnp.arange(N_EXPERTS, dtype=jnp.int32)[:, None]).astype(jnp.int32)
    counts = jnp.sum(onehot, axis=1)
    rank = jnp.sum((jnp.cumsum(onehot, axis=1) - onehot) * onehot, axis=0)
    padded = (counts + tm_e - 1) // tm_e * tm_e
    ends = jnp.cumsum(padded)
    pos = jnp.sum(onehot * (ends - padded)[:, None], axis=0) + rank
    rows = TOP_K * T + N_EXPERTS * tm_e
    token = jnp.arange(TOP_K * T, dtype=jnp.int32) % T
    src = (jnp.arange(rows, dtype=jnp.int32) % T).at[pos].set(
        token, unique_indices=True, mode="promise_in_bounds")
    tile_start = jnp.arange(rows // tm_e, dtype=jnp.int32) * tm_e
    tile_expert = jnp.minimum(jnp.sum((tile_start[:, None] >= ends[None, :]).astype(jnp.int32), axis=1),
                              N_EXPERTS - 1)
    x_sorted = h_bf16.at[src].get(mode="promise_in_bounds")
    tiles_used = (ends[N_EXPERTS - 1:] // tm_e).astype(jnp.int32)
    y = _ffn_experts(tile_expert, tiles_used, x_sorted, w13, w2, tm_e, tf)
    y0 = y.at[pos[:T]].get(mode="promise_in_bounds")
    y1 = y.at[pos[T:]].get(mode="promise_in_bounds")
    return _moe_out(h, y0, y1, wt, lng, lnb, tm_router)


def _rot_cols(w, block):
    d, n = w.shape
    w4 = w.reshape(d, n // block, 2, block // 2)
    return jnp.concatenate([-w4[:, :, 1], w4[:, :, 0]], axis=2).reshape(d, n)


def _attention_weights(w_in):
    o = np.cumsum([0, MLA_Q_LORA, MLA_KV_LORA, MLA_ROPE, _DIL_COLS, GQA_Q_HEADS * HEAD_DIM,
                   2 * GQA_KV_HEADS * HEAD_DIM, _DIFF_COLS, N_BRANCH * D_MODEL])
    cq, ckv, kr, dil, gq, gkv, diff, gate = (w_in[:, o[i]:o[i + 1]] for i in range(8))
    z = lambda n: jnp.zeros((D_MODEL, n), w_in.dtype)
    mla = [cq, z(64), ckv, z(64), kr, z(32), z(64), _rot_cols(kr, MLA_ROPE), z(32)]
    gqa = [gq, _rot_cols(gq, 32), gkv[:, :128], _rot_cols(gkv[:, :128], 32), gkv[:, 128:]]
    small = jnp.concatenate(mla + gqa, axis=1)
    return [w.astype(BF16) for w in (small, diff, dil)], gate.astype(BF16)


def _mla_up_weights(w_uq, w_ukv):
    zq = lambda n: jnp.zeros((MLA_Q_LORA, n), w_uq.dtype)
    per = MLA_NOPE + MLA_ROPE
    plain, rot = [], []
    for h in range(MLA_HEADS):
        nope = w_uq[:, h * per:h * per + MLA_NOPE]
        rope = w_uq[:, h * per + MLA_NOPE:(h + 1) * per]
        plain += [nope, rope, zq(32)]
        rot += [zq(64), _rot_cols(rope, MLA_ROPE), zq(32)]
    wuq = jnp.concatenate(plain + rot, axis=1)
    wuq = jnp.pad(wuq, ((0, _Q_LORA_PAD - MLA_Q_LORA), (0, 0)))
    zk = jnp.zeros((MLA_KV_LORA, 64), w_ukv.dtype)
    per = MLA_NOPE + MLA_V
    ks, vs = [], []
    for h in range(MLA_HEADS):
        ks += [w_ukv[:, h * per:h * per + MLA_NOPE], zk]
        vs += [w_ukv[:, h * per + MLA_NOPE:(h + 1) * per], zk]
    wukv = jnp.concatenate(ks + vs, axis=1)
    return wuq.astype(BF16), wukv.astype(BF16)


def _rope_angles(pos, dim):
    inv = ROPE_THETA ** (-(jnp.arange(0, dim, 2, dtype=F32) / dim))
    return pos[:, None] * inv[None, :]


def _position_tables(seq):
    rows = seq // GRID_W
    row_idx = jnp.repeat(jnp.arange(rows, dtype=F32), GRID_W)
    col_idx = jnp.tile(jnp.arange(GRID_W, dtype=F32), rows)
    a1 = _rope_angles(jnp.arange(seq, dtype=F32), MLA_ROPE)
    ar = _rope_angles(row_idx, HEAD_DIM // 2)
    ac = _rope_angles(col_idx, HEAD_DIM // 2)
    z = lambda n: jnp.zeros((seq, n), F32)
    c1 = jnp.concatenate([jnp.ones((seq, MLA_NOPE), F32)] + [jnp.cos(a1)] * 2 + [z(32)], axis=1)
    s1 = jnp.concatenate([z(MLA_NOPE)] + [jnp.sin(a1)] * 2 + [z(32)], axis=1)
    ca = jnp.concatenate(([jnp.cos(ar)] * 2 + [jnp.cos(ac)] * 2) * 2, axis=1)
    sa = jnp.concatenate(([jnp.sin(ar)] * 2 + [jnp.sin(ac)] * 2) * 2, axis=1)
    return [c1, s1, ca, sa]


def _gqa_gain_rows(q_norm, k_norm):
    def swap(g):
        return jnp.concatenate([g[16:32], g[0:16], g[48:64], g[32:48]])

    sg = HEAD_DIM ** -0.5 * LOG2E
    rows = [sg * q_norm, sg * swap(q_norm), k_norm, swap(k_norm)]
    return jnp.stack([jnp.concatenate([r, r]) for r in rows])


def _pick(n, pref):
    t = min(n, pref)
    assert n % t == 0, (n, pref)
    return t


class _Tiles(NamedTuple):
    rows: int
    flash: int
    diff: int
    ffn_rows: int
    ffn_cols: int


def _plan_tiles(seq, tokens):
    return _Tiles(rows=_pick(seq, 512), flash=_pick(seq, 1024), diff=_pick(seq, 512),
                  ffn_rows=_pick(tokens, 1024), ffn_cols=D_FF // 4)


def kernel(x, ln_emb_g, ln_emb_b, w_in, b_gate, mla_q_norm, mla_kv_norm, mla_w_uq, mla_w_ukv,
           gqa_q_norm, gqa_k_norm, diff_lambda, diff_subln, w_branch, w_out, ln1_g, ln1_b,
           ffn_w13, ffn_w2, moe_router, moe_w13, moe_w2, ln2_g, ln2_b):
    batch, seq, _ = x.shape
    T = batch * seq
    tiles = _plan_tiles(seq, T)
    row = lambda v: v.reshape(1, -1)
    pos_tabs = _position_tables(seq)

    h = x.reshape(T, D_MODEL)
    for l in range(DEPTH):
        lam_init = 0.8 - 0.6 * float(np.exp(-0.3 * l))
        weights, wg = _attention_weights(w_in[l])
        wuq, wukv = _mla_up_weights(mla_w_uq[l], mla_w_ukv[l])
        gains = _gqa_gain_rows(gqa_q_norm[l], gqa_k_norm[l])
        nq = jnp.pad(mla_q_norm[l], (0, _Q_LORA_PAD - MLA_Q_LORA)).reshape(1, _Q_LORA_PAD)
        res = _project(h, batch, seq, l == 0, row(ln_emb_g), row(ln_emb_b), weights, wuq, wukv,
                       nq, row(mla_kv_norm[l]), gains, pos_tabs, tiles.rows)
        if l == 0:
            h, res = res[0], res[1:]
        mq, mk, mv, gq, gk, gv, dq, dk, dv = res[:9]
        dil = res[9:]
        ya = _flash(mq, mk, mv, tiles.flash, tiles.flash).reshape(T, BRANCH_W)
        yc = _flash(gq, gk, gv, tiles.flash, tiles.flash).reshape(T, BRANCH_W)
        sub = jnp.pad(diff_subln[l], (0, LANES - DIFF_V)).reshape(1, LANES)
        yd = _diff_attention(dq, dk, dv, diff_lambda[l], sub, lam_init, tiles.diff).reshape(T, BRANCH_W)
        dil_o, dil_lse = [], []
        for g in range(len(DIL_GROUPS)):
            o, lse = _dilated_group(dil[g], dil[3 + g], dil[6 + g], g)
            dil_o.append(o)
            dil_lse.append(lse)
        dense = l % 2 == 0
        mixed = _mix(h, ya, yc, yd, dil_o, dil_lse, wg, b_gate[l], w_branch[l].astype(BF16),
                     w_out[l].astype(BF16), row(ln1_g[l]), row(ln1_b[l]), tiles.rows, not dense)
        if dense:
            h = _ffn_dense(mixed[0], ffn_w13[l // 2], ffn_w2[l // 2],
                           row(ln2_g[l]), row(ln2_b[l]), tiles.ffn_rows, tiles.ffn_cols)
        else:
            h = _moe(mixed[0], mixed[1], moe_router[l // 2], moe_w13[l // 2], moe_w2[l // 2],
                     row(ln2_g[l]), row(ln2_b[l]), tiles.rows, tiles.ffn_rows, tiles.ffn_cols)
    return h.reshape(batch, seq, D_MODEL)
```

```python
import functools
from typing import NamedTuple

import jax
import jax.numpy as jnp
import numpy as np
from jax import lax
from jax.experimental import pallas as pl
from jax.experimental.pallas import tpu as pltpu

D_MODEL = 1024
GRID_W = 64
HEAD_DIM = 64
N_BRANCH = 4
BRANCH_W = 256
MLA_HEADS = 4
MLA_NOPE = 64
MLA_ROPE = 32
MLA_V = 64
MLA_Q_LORA = 192
MLA_KV_LORA = 128
DIL_GROUPS = ((128, 1), (512, 4), (2048, 16))
DIL_HEADS = 4
GQA_Q_HEADS = 4
GQA_KV_HEADS = 2
DIFF_HEADS = 4
DIFF_D = 32
DIFF_V = 64
D_FF = 3584
N_EXPERTS = 8
TOP_K = 2
ROPE_THETA = 10000.0
LN_EPS = 1e-5
RMS_EPS = 1e-6
DIFF_NORM_EPS = 1e-5
DEPTH = 2
DEEPNORM_ALPHA = (2 * DEPTH) ** 0.25

LANES = 128
HALF = LANES // 2
MXU_COLS = 256
VMEM_BYTES = 64 * 1024 * 1024
VMEM_LIMIT = VMEM_BYTES - 8 * 1024 * 1024

LOG2E = 1.4426950408889634
NEG = -1e30
BF16 = jnp.bfloat16
F32 = jnp.float32

_Q_LORA_PAD = 2 * LANES
_MLA_COLS = _Q_LORA_PAD + MLA_KV_LORA + 2 * LANES
_GQA_COLS = 7 * LANES
_DIFF_COLS = 3 * DIFF_HEADS * DIFF_V
_DIL_COLS = 3 * len(DIL_GROUPS) * BRANCH_W
_SMALL_COLS = _MLA_COLS + _GQA_COLS


def _cparams(sem):
    return pltpu.CompilerParams(dimension_semantics=sem, vmem_limit_bytes=VMEM_LIMIT)


def _resident(shape, index_map):
    return pl.BlockSpec(shape, index_map, pipeline_mode=pl.Buffered(1))


def _layernorm(x, g, b):
    mu = jnp.mean(x, axis=-1, keepdims=True)
    xc = x - mu
    var = jnp.mean(xc * xc, axis=-1, keepdims=True)
    return xc * lax.rsqrt(var + LN_EPS) * g + b


def _lane(shape):
    return lax.broadcasted_iota(jnp.int32, shape, len(shape) - 1)


def _split_pair(x):
    lo_mask = _lane(x.shape) < HALF
    lo = jnp.where(lo_mask, x, 0.0)
    hi = jnp.where(lo_mask, pltpu.roll(x, HALF, axis=1), 0.0)
    return lo, hi


def _join_pair(lo, hi):
    return jnp.where(_lane(lo.shape) < HALF, lo, pltpu.roll(hi, HALF, axis=1))


def _with_ones(v):
    return jnp.where(_lane(v.shape) == HALF, 1.0, v)


def _diff_slope(h):
    return 2.0 ** (-8.0 * (h + 1) / DIFF_HEADS) * LOG2E


def _split3(val):
    f1 = val.astype(BF16).astype(F32)
    r1 = val - f1
    f2 = r1.astype(BF16).astype(F32)
    f3 = (r1 - f2).astype(BF16).astype(F32)
    return f1, f2, f3


_FEAT = 2 * DIFF_D


def _key_bias_features(val):
    f1, f2, f3 = _split3(val)
    lane = _lane(val.shape)
    ones = jnp.where((lane >= _FEAT + 3) & (lane < _FEAT + 6), 1.0, 0.0)
    return jnp.where(lane == _FEAT, f1, jnp.where(lane == _FEAT + 1, f2, jnp.where(lane == _FEAT + 2, f3, ones)))


def _proj_kernel(apply_ln, tiles_per_seq, *refs):
    (x_ref, lng_ref, lnb_ref, ws_ref, wdiff_ref, wdil_ref, wuq_ref, wukv_ref, nq_ref, nkv_ref,
     gains_ref, c1_ref, s1_ref, ca_ref, sa_ref) = refs[:15]
    outs = refs[15:]
    if apply_ln:
        h_ref, outs = outs[0], outs[1:]
    (mq_ref, mk_ref, mv_ref, gq_ref, gk_ref, gv_ref, dq_ref, dk_ref, dv_ref) = outs[:9]
    dil_refs = outs[9:]

    x = x_ref[...]
    if apply_ln:
        x = _layernorm(x, lng_ref[...], lnb_ref[...])
        h_ref[...] = x
    hb = x.astype(BF16)

    def proj(w_ref, off, n):
        return jnp.dot(hb, w_ref[:, off:off + n], preferred_element_type=F32)

    p_small = proj(ws_ref, 0, _SMALL_COLS)
    p = p_small[:, :_MLA_COLS]
    cq, ckv, kr, krr = p[:, 0:256], p[:, 256:384], p[:, 384:512], p[:, 512:640]
    rq = lax.rsqrt(jnp.sum(cq * cq, axis=-1, keepdims=True) * (1.0 / MLA_Q_LORA) + RMS_EPS)
    cqn = (cq * rq * nq_ref[...]).astype(BF16)
    q2 = jnp.dot(cqn, wuq_ref[...], preferred_element_type=F32)
    rkv = lax.rsqrt(jnp.sum(ckv * ckv, axis=-1, keepdims=True) * (1.0 / MLA_KV_LORA) + RMS_EPS)
    ckvn = (ckv * rkv * nkv_ref[...]).astype(BF16)
    kv2 = jnp.dot(ckvn, wukv_ref[...], preferred_element_type=F32)
    c1, s1 = c1_ref[...], s1_ref[...]
    k_rope = kr * c1 + krr * s1
    mla_scale = (MLA_NOPE + MLA_ROPE) ** -0.5 * LOG2E
    cq_t, sq_t = c1 * mla_scale, s1 * mla_scale
    for h in range(MLA_HEADS):
        qh = q2[:, h * LANES:(h + 1) * LANES]
        qrh = q2[:, (MLA_HEADS + h) * LANES:(MLA_HEADS + h + 1) * LANES]
        mq_ref[h] = (qh * cq_t + qrh * sq_t).astype(BF16)
        mk_ref[h] = (kv2[:, h * LANES:(h + 1) * LANES] + k_rope).astype(BF16)
        vh = kv2[:, (MLA_HEADS + h) * LANES:(MLA_HEADS + h + 1) * LANES]
        mv_ref[h] = _with_ones(vh).astype(BF16)

    p = p_small[:, _MLA_COLS:]

    def head_rms(xs):
        ss = xs * xs
        lo_mask = _lane(xs.shape) < HALF
        s_all = jnp.sum(ss, axis=-1, keepdims=True)
        s_lo = jnp.sum(jnp.where(lo_mask, ss, 0.0), axis=-1, keepdims=True)
        return lax.rsqrt(jnp.where(lo_mask, s_lo, s_all - s_lo) * (1.0 / HEAD_DIM) + RMS_EPS)

    ca, sa = ca_ref[...], sa_ref[...]
    aq, bq = ca * gains_ref[0:1, :], sa * gains_ref[1:2, :]
    ak, bk = ca * gains_ref[2:3, :], sa * gains_ref[3:4, :]
    for s in range(2):
        qs = p[:, s * LANES:(s + 1) * LANES]
        qrs = p[:, 256 + s * LANES:256 + (s + 1) * LANES]
        qn = head_rms(qs) * (qs * aq + qrs * bq)
        lo, hi = _split_pair(qn)
        gq_ref[2 * s] = lo.astype(BF16)
        gq_ref[2 * s + 1] = hi.astype(BF16)
    ks, krs = p[:, 512:640], p[:, 640:768]
    kn = head_rms(ks) * (ks * ak + krs * bk)
    lo, hi = _split_pair(kn)
    gk_ref[0] = lo.astype(BF16)
    gk_ref[1] = hi.astype(BF16)
    lo, hi = _split_pair(p[:, 768:896])
    gv_ref[0] = _with_ones(lo).astype(BF16)
    gv_ref[1] = _with_ones(hi).astype(BF16)

    p = proj(wdiff_ref, 0, _DIFF_COLS)
    dscale = DIFF_D ** -0.5 * LOG2E
    tm = x.shape[0]
    pos = (pl.program_id(0) % tiles_per_seq * tm
           + lax.broadcasted_iota(jnp.int32, (tm, LANES), 0)).astype(F32)
    for s in range(2):
        lo, hi = _split_pair(p[:, s * LANES:(s + 1) * LANES] * dscale)
        dq_ref[2 * s] = lo.astype(BF16)
        dq_ref[2 * s + 1] = hi.astype(BF16)
        lo, hi = _split_pair(p[:, 256 + s * LANES:256 + (s + 1) * LANES])
        dk_ref[2 * s] = (lo + _key_bias_features(_diff_slope(2 * s) * pos)).astype(BF16)
        dk_ref[2 * s + 1] = (hi + _key_bias_features(_diff_slope(2 * s + 1) * pos)).astype(BF16)
        lo, hi = _split_pair(p[:, 512 + s * LANES:512 + (s + 1) * LANES])
        dv_ref[2 * s] = _with_ones(lo).astype(BF16)
        dv_ref[2 * s + 1] = _with_ones(hi).astype(BF16)

    lscale = HEAD_DIM ** -0.5 * LOG2E
    for i in range(9):
        seg = proj(wdil_ref, i * BRANCH_W, BRANCH_W)
        if i < 3:
            seg = seg * lscale
        dil_refs[i][0] = seg[:, :LANES]
        dil_refs[i][1] = seg[:, LANES:]


def _project(x2d, batch, seq, apply_ln, lng, lnb, weights, wuq, wukv, nq, nkv, gains, tables, tm):
    T = x2d.shape[0]
    nst = seq // tm
    grid = (T // tm,)
    row = lambda i: (i, 0)
    const = lambda i: (0, 0)
    tab = lambda i: (i % nst, 0)
    head = lambda i: (i // nst, 0, i % nst, 0)

    in_specs = [
        pl.BlockSpec((tm, D_MODEL), row),
        pl.BlockSpec((1, D_MODEL), const),
        pl.BlockSpec((1, D_MODEL), const),
        _resident((D_MODEL, _SMALL_COLS), const),
        _resident((D_MODEL, _DIFF_COLS), const),
        _resident((D_MODEL, _DIL_COLS), const),
        _resident((_Q_LORA_PAD, 8 * LANES), const),
        _resident((LANES, 8 * LANES), const),
        pl.BlockSpec((1, _Q_LORA_PAD), const),
        pl.BlockSpec((1, LANES), const),
        pl.BlockSpec((4, LANES), const),
    ] + [pl.BlockSpec((tm, LANES), tab)] * 4

    def heads_out(n):
        return (jax.ShapeDtypeStruct((batch, n, seq, LANES), BF16),
                pl.BlockSpec((None, n, tm, LANES), head))

    outs = []
    if apply_ln:
        outs.append((jax.ShapeDtypeStruct((T, D_MODEL), F32), pl.BlockSpec((tm, D_MODEL), row)))
    outs += [heads_out(MLA_HEADS)] * 3
    outs += [heads_out(GQA_Q_HEADS), heads_out(GQA_KV_HEADS), heads_out(GQA_KV_HEADS)]
    outs += [heads_out(DIFF_HEADS)] * 3
    outs += [(jax.ShapeDtypeStruct((batch, 2, seq, LANES), F32), pl.BlockSpec((None, 2, tm, LANES), head))] * 9

    res = pl.pallas_call(
        functools.partial(_proj_kernel, apply_ln, nst),
        grid=grid,
        in_specs=in_specs,
        out_specs=[o[1] for o in outs],
        out_shape=[o[0] for o in outs],
        compiler_params=_cparams(("parallel",)),
        name="proj_prep",
    )(x2d, lng, lnb, *weights, wuq, wukv, nq, nkv, gains, *tables)
    return res


def _normalize(acc):
    return acc / acc[:, HALF:HALF + 1]


def _scores(q, k):
    return lax.dot_general(q, k, (((1,), (1,)), ((), ())), preferred_element_type=F32)


def _softmax_update(s, v, acc_ref, m_ref, idx):
    blocks = [s[:, j * LANES:(j + 1) * LANES] for j in range(s.shape[1] // LANES)]
    part = functools.reduce(jnp.maximum, blocks)
    m_prev = m_ref[idx]
    m_new = jnp.maximum(m_prev, jnp.max(part, axis=-1, keepdims=True))
    p = jnp.concatenate([jnp.exp2((b - m_new).astype(BF16)) for b in blocks], axis=1)
    acc_ref[idx] = (jnp.exp2(m_prev - m_new) * acc_ref[idx]
                    + jnp.dot(p, v, preferred_element_type=F32))
    m_ref[idx] = m_new


def _flash_kernel(n_rep, tk, q_ref, k_ref, v_ref, o_ref, acc_ref, m_ref):
    n_heads = q_ref.shape[0]
    seq = k_ref.shape[1]
    acc_ref[...] = jnp.zeros_like(acc_ref)
    m_ref[...] = jnp.full_like(m_ref, NEG)

    def body(c, carry):
        ks = pl.multiple_of(c * tk, tk)
        scores = [_scores(q_ref[h], k_ref[h // n_rep, pl.ds(ks, tk), :]) for h in range(n_heads)]
        for h in range(n_heads):
            _softmax_update(scores[h], v_ref[h // n_rep, pl.ds(ks, tk), :], acc_ref, m_ref, h)
        return carry

    lax.fori_loop(0, seq // tk, body, 0)
    outs = [_normalize(acc_ref[h]) for h in range(n_heads)]
    for s in range(n_heads // 2):
        o_ref[:, s * LANES:(s + 1) * LANES] = _join_pair(outs[2 * s], outs[2 * s + 1]).astype(o_ref.dtype)


def _flash(q, k, v, tq, tk):
    batch, n_heads, seq, _ = q.shape
    n_kv = k.shape[1]
    return pl.pallas_call(
        functools.partial(_flash_kernel, n_heads // n_kv, tk),
        grid=(batch, seq // tq),
        in_specs=[
            pl.BlockSpec((None, n_heads, tq, LANES), lambda b, i: (b, 0, i, 0)),
            _resident((None, n_kv, seq, LANES), lambda b, i: (b, 0, 0, 0)),
            _resident((None, n_kv, seq, LANES), lambda b, i: (b, 0, 0, 0)),
        ],
        out_specs=pl.BlockSpec((None, tq, n_heads * HALF), lambda b, i: (b, i, 0)),
        out_shape=jax.ShapeDtypeStruct((batch, seq, n_heads * HALF), BF16),
        scratch_shapes=[pltpu.VMEM((n_heads, tq, LANES), F32), pltpu.VMEM((n_heads, tq, LANES), F32)],
        compiler_params=_cparams(("parallel", "parallel")),
        name="flash_attn",
    )(q, k, v)


def _diff_kernel(lam_init, q_ref, k_ref, v_ref, lp_ref, sub_ref, o_ref, acc_ref, m_ref, qv_ref):
    n_heads, tq, _ = q_ref.shape
    seq = k_ref.shape[1]
    tile = pl.program_id(1)
    t0 = tile * tq
    lp = lp_ref[...]
    lam = (jnp.exp(jnp.sum(lp[0:1] * lp[1:2], keepdims=True))
           - jnp.exp(jnp.sum(lp[2:3] * lp[3:4], keepdims=True)) + lam_init)
    lane = _lane((tq, LANES))
    tpos = (t0 + lax.broadcasted_iota(jnp.int32, (tq, LANES), 0)).astype(F32)
    in_f = (lane >= _FEAT) & (lane < _FEAT + 3)
    for h in range(n_heads):
        q = q_ref[h].astype(F32)
        g1, g2, g3 = _split3(2.0 * _diff_slope(h) * tpos)
        left = jnp.where(in_f, 1.0, 0.0)
        right = jnp.where(in_f, -1.0, jnp.where(lane == _FEAT + 3, g1, jnp.where(
            lane == _FEAT + 4, g2, jnp.where(lane == _FEAT + 5, g3, 0.0))))
        for m in range(2):
            qm = jnp.where((lane >= m * DIFF_D) & (lane < (m + 1) * DIFF_D), q, 0.0)
            qv_ref[0, 2 * h + m] = (qm + left).astype(BF16)
            qv_ref[1, 2 * h + m] = (qm + right).astype(BF16)
    acc_ref[...] = jnp.zeros_like(acc_ref)
    m_ref[...] = jnp.full_like(m_ref, NEG)
    chains = range(2 * n_heads)

    def update(scores, ks):
        for i in chains:
            _softmax_update(scores[i], v_ref[i // 2, pl.ds(ks, tq), :], acc_ref, m_ref, i)

    ks = pl.multiple_of(t0, tq)
    update([jnp.minimum(_scores(qv_ref[0, i], k_ref[i // 2, pl.ds(ks, tq), :]),
                        _scores(qv_ref[1, i], k_ref[i // 2, pl.ds(ks, tq), :])) for i in chains], ks)

    def body(c, carry):
        side = (c >= tile).astype(jnp.int32)
        ks = pl.multiple_of((c + side) * tq, tq)
        update([_scores(qv_ref[side, i], k_ref[i // 2, pl.ds(ks, tq), :]) for i in chains], ks)
        return carry

    n_other = seq // tq - 1
    if n_other % 2:
        body(n_other - 1, 0)
    lax.fori_loop(0, n_other // 2 * 2, body, 0, unroll=2)

    outs = []
    for h in range(n_heads):
        o = _normalize(acc_ref[2 * h]) - lam * _normalize(acc_ref[2 * h + 1])
        o = jnp.where(lane < DIFF_V, o, 0.0)
        ms = jnp.sum(o * o, axis=-1, keepdims=True) * (1.0 / DIFF_V)
        outs.append(o * lax.rsqrt(ms + DIFF_NORM_EPS) * sub_ref[...] * (1.0 - lam_init))
    for s in range(n_heads // 2):
        o_ref[:, s * LANES:(s + 1) * LANES] = _join_pair(outs[2 * s], outs[2 * s + 1]).astype(o_ref.dtype)


def _diff_attention(q, k, v, lam_params, subln, lam_init, tq):
    batch, n_heads, seq, _ = q.shape
    return pl.pallas_call(
        functools.partial(_diff_kernel, lam_init),
        grid=(batch, seq // tq),
        in_specs=[
            pl.BlockSpec((None, n_heads, tq, LANES), lambda b, i: (b, 0, i, 0)),
            _resident((None, n_heads, seq, LANES), lambda b, i: (b, 0, 0, 0)),
            _resident((None, n_heads, seq, LANES), lambda b, i: (b, 0, 0, 0)),
            pl.BlockSpec((4, DIFF_D), lambda b, i: (0, 0)),
            pl.BlockSpec((1, LANES), lambda b, i: (0, 0)),
        ],
        out_specs=pl.BlockSpec((None, tq, n_heads * HALF), lambda b, i: (b, i, 0)),
        out_shape=jax.ShapeDtypeStruct((batch, seq, n_heads * HALF), BF16),
        scratch_shapes=[pltpu.VMEM((2 * n_heads, tq, LANES), F32)] * 2
        + [pltpu.VMEM((2, 2 * n_heads, tq, LANES), BF16)],
        compiler_params=_cparams(("parallel", "parallel")),
        name="diff_attn",
    )(q, k, v, lam_params, subln)


_DIL_BLOCK = 2048
_DIL_Q = 128


def _dil_kernel(group, dil, seq, q_ref, kp_ref, kc_ref, kn_ref, vp_ref, vc_ref, vn_ref,
                o_ref, lse_ref, kw_ref, vw_ref):
    tb = q_ref.shape[1]
    halo = DIL_GROUPS[group][0] // 2
    assert halo == kp_ref.shape[1] and halo == (_DIL_Q // 2) * dil
    for w_ref, p_ref, c_ref, n_ref in ((kw_ref, kp_ref, kc_ref, kn_ref), (vw_ref, vp_ref, vc_ref, vn_ref)):
        w_ref[:, 0:halo] = p_ref[...]
        w_ref[:, halo:halo + tb] = c_ref[...]
        w_ref[:, halo + tb:halo + tb + halo] = n_ref[...]
    t0 = pl.program_id(1) * tb
    shape = (_DIL_Q, 2 * _DIL_Q)
    rel = lax.broadcasted_iota(jnp.int32, shape, 1) - lax.broadcasted_iota(jnp.int32, shape, 0) - _DIL_Q // 2
    absrel = jnp.abs(rel).astype(F32)
    band = jnp.where(absrel <= _DIL_Q // 2, 0.0, -NEG)
    n_slopes = len(DIL_GROUPS) * DIL_HEADS
    bias = [2.0 ** (-8.0 * (group * DIL_HEADS + h + 1) / n_slopes) * dil * LOG2E * absrel + band
            for h in range(DIL_HEADS)]
    col = lax.broadcasted_iota(jnp.int32, (1, 2 * _DIL_Q), 1)
    lo_mask = _lane((_DIL_Q, LANES)) < HALF
    for unit in range(tb // _DIL_Q):
        first = (unit // dil) * _DIL_Q * dil + unit % dil
        rows = pl.ds(first, _DIL_Q, stride=dil)
        win = pl.ds(first, 2 * _DIL_Q, stride=dil)
        key_token = t0 + (first - halo) + dil * col
        off_seq = jnp.where(key_token >= 0, jnp.where(key_token < seq, 0.0, NEG), NEG)
        for s in range(2):
            qs = q_ref[s, rows, :].astype(BF16)
            ksl = kw_ref[s, win, :].astype(BF16)
            vsl = vw_ref[s, win, :].astype(BF16)
            res = []
            for part in range(2):
                keep = lo_mask if part == 0 else jnp.logical_not(lo_mask)
                sc = _scores(jnp.where(keep, qs, jnp.zeros_like(qs)), ksl) - bias[2 * s + part] + off_seq
                m = jnp.max(sc, axis=-1, keepdims=True)
                p = jnp.exp2(sc - m)
                l = jnp.sum(p, axis=-1, keepdims=True)
                o = jnp.dot(p.astype(BF16), vsl, preferred_element_type=F32) / l
                res.append((o, m + jnp.log2(l)))
            o_ref[s, rows, :] = jnp.where(lo_mask, res[0][0], res[1][0])
            lse_ref[s, rows, :] = jnp.where(lo_mask, res[0][1], res[1][1])


def _dilated_group(q, k, v, group):
    batch, _, seq, _ = q.shape
    dil = DIL_GROUPS[group][1]
    halo = DIL_GROUPS[group][0] // 2
    tb = _DIL_BLOCK
    assert seq % tb == 0 and seq // dil >= 2 * _DIL_Q and tb % (_DIL_Q * dil) == 0
    per = tb // halo
    cur = pl.BlockSpec((None, 2, tb, LANES), lambda b, i: (b, 0, i, 0))
    prev = pl.BlockSpec((None, 2, halo, LANES), lambda b, i: (b, 0, jnp.maximum(i * per - 1, 0), 0))
    nxt = pl.BlockSpec((None, 2, halo, LANES),
                       lambda b, i: (b, 0, jnp.minimum((i + 1) * per, seq // halo - 1), 0))
    out_shape = jax.ShapeDtypeStruct((batch, 2, seq, LANES), F32)
    o, lse = pl.pallas_call(
        functools.partial(_dil_kernel, group, dil, seq),
        grid=(batch, seq // tb),
        in_specs=[cur, prev, cur, nxt, prev, cur, nxt],
        out_specs=[cur, cur],
        out_shape=[out_shape, out_shape],
        scratch_shapes=[pltpu.VMEM((2, tb + 2 * halo, LANES), F32)] * 2,
        compiler_params=_cparams(("parallel", "parallel")),
        name=f"dilated_g{group}",
    )(q, k, k, k, v, v, v)
    return o, lse


def _mix_kernel(h_ref, ya_ref, yc_ref, yd_ref, o0_ref, o1_ref, o2_ref, l0_ref, l1_ref, l2_ref,
                wg_ref, bg_ref, wb_ref, wo_ref, lng_ref, lnb_ref, out_ref, *maybe_bf16_ref):
    h = h_ref[...]
    hb = h.astype(BF16)
    slabs = []
    for s in range(2):
        l0, l1, l2 = l0_ref[s], l1_ref[s], l2_ref[s]
        mx = jnp.maximum(jnp.maximum(l0, l1), l2)
        e0, e1, e2 = jnp.exp2(l0 - mx), jnp.exp2(l1 - mx), jnp.exp2(l2 - mx)
        slabs.append(((e0 * o0_ref[s] + e1 * o1_ref[s] + e2 * o2_ref[s]) / (e0 + e1 + e2)).astype(BF16))
    ys = (ya_ref[...], jnp.concatenate(slabs, axis=1), yc_ref[...], yd_ref[...])
    acc = None
    for n in range(N_BRANCH):
        logit = jnp.dot(hb, wg_ref[:, n * D_MODEL:(n + 1) * D_MODEL],
                        preferred_element_type=F32) + bg_ref[n:n + 1, :]
        gate = 1.0 / (1.0 + jnp.exp(-logit))
        term = gate * jnp.dot(ys[n], wb_ref[n], preferred_element_type=F32)
        acc = term if acc is None else acc + term
    m = jnp.dot(acc.astype(BF16), wo_ref[...], preferred_element_type=F32)
    out = _layernorm(DEEPNORM_ALPHA * h + m, lng_ref[...], lnb_ref[...])
    out_ref[...] = out
    for r in maybe_bf16_ref:
        r[...] = out.astype(BF16)


def _mix(h, ya, yc, yd, dil_o, dil_lse, wg, bg, wb, wo, lng, lnb, tm, with_bf16_copy):
    T = h.shape[0]
    nst = dil_o[0].shape[2] // tm
    slab = pl.BlockSpec((None, 2, tm, LANES), lambda i: (i // nst, 0, i % nst, 0))
    row = lambda i: (i, 0)
    const = lambda i: (0, 0)
    wide = pl.BlockSpec((tm, D_MODEL), row)
    narrow = pl.BlockSpec((tm, BRANCH_W), row)
    return pl.pallas_call(
        _mix_kernel,
        grid=(T // tm,),
        in_specs=[wide] + [narrow] * 3 + [slab] * 6 + [
            _resident((D_MODEL, N_BRANCH * D_MODEL), const),
            pl.BlockSpec((N_BRANCH, D_MODEL), const),
            _resident((N_BRANCH, BRANCH_W, D_MODEL), lambda i: (0, 0, 0)),
            _resident((D_MODEL, D_MODEL), const),
            pl.BlockSpec((1, D_MODEL), const),
            pl.BlockSpec((1, D_MODEL), const),
        ],
        out_specs=[wide] * (2 if with_bf16_copy else 1),
        out_shape=[jax.ShapeDtypeStruct((T, D_MODEL), F32)]
        + ([jax.ShapeDtypeStruct((T, D_MODEL), BF16)] if with_bf16_copy else []),
        compiler_params=_cparams(("parallel",)),
        name="gate_mix",
    )(h, ya, yc, yd, *dil_o, *dil_lse, wg, bg, wb, wo, lng, lnb)


def _swiglu_step(xb_ref, w1_ref, w3_ref, w2_ref, acc_ref):
    @pl.when(pl.program_id(1) == 0)
    def _():
        acc_ref[...] = jnp.zeros_like(acc_ref)

    xb = xb_ref[...]
    tf = w1_ref.shape[1]
    full = tf // MXU_COLS * MXU_COLS
    mid = (full // MXU_COLS + 1) // 2 * MXU_COLS
    gated = []

    def gate(a, b):
        gated.append((a / (1.0 + jnp.exp(-a)) * b).astype(BF16))

    for lo, hi in ((0, mid), (mid, full)):
        gate(jnp.dot(xb, w1_ref[:, lo:hi].astype(BF16), preferred_element_type=F32),
             jnp.dot(xb, w3_ref[:, lo:hi].astype(BF16), preferred_element_type=F32))
    if full < tf:
        w13 = jnp.concatenate([w1_ref[:, full:tf].astype(BF16), w3_ref[:, full:tf].astype(BF16)], axis=1)
        ab = jnp.dot(xb, w13, preferred_element_type=F32)
        gate(ab[:, :tf - full], ab[:, tf - full:])
    acc_ref[...] += jnp.dot(jnp.concatenate(gated, axis=1), w2_ref[...].astype(BF16),
                            preferred_element_type=F32)


def _ffn_dense_kernel(x_ref, w1_ref, w3_ref, w2_ref, lng_ref, lnb_ref, o_ref, acc_ref, xb_ref):
    @pl.when(pl.program_id(1) == 0)
    def _():
        xb_ref[...] = x_ref[...].astype(BF16)

    _swiglu_step(xb_ref, w1_ref, w3_ref, w2_ref, acc_ref)

    @pl.when(pl.program_id(1) == pl.num_programs(1) - 1)
    def _():
        o_ref[...] = _layernorm(DEEPNORM_ALPHA * x_ref[...] + acc_ref[...], lng_ref[...], lnb_ref[...])


def _ffn_dense(x, w13, w2, lng, lnb, tm, tf):
    T = x.shape[0]
    nf = D_FF // tf
    return pl.pallas_call(
        _ffn_dense_kernel,
        grid=(T // tm, nf),
        in_specs=[
            pl.BlockSpec((tm, D_MODEL), lambda i, j: (i, 0)),
            pl.BlockSpec((D_MODEL, tf), lambda i, j: (0, j)),
            pl.BlockSpec((D_MODEL, tf), lambda i, j: (0, nf + j)),
            pl.BlockSpec((tf, D_MODEL), lambda i, j: (j, 0)),
            pl.BlockSpec((1, D_MODEL), lambda i, j: (0, 0)),
            pl.BlockSpec((1, D_MODEL), lambda i, j: (0, 0)),
        ],
        out_specs=pl.BlockSpec((tm, D_MODEL), lambda i, j: (i, 0)),
        out_shape=jax.ShapeDtypeStruct((T, D_MODEL), F32),
        scratch_shapes=[pltpu.VMEM((tm, D_MODEL), F32), pltpu.VMEM((tm, D_MODEL), BF16)],
        compiler_params=_cparams(("parallel", "arbitrary")),
        name="ffn_dense",
    )(x, w13, w13, w2, lng, lnb)


def _ffn_expert_kernel(te_ref, used_ref, x_ref, w1_ref, w3_ref, w2_ref, o_ref, acc_ref):
    del te_ref
    live = pl.program_id(0) < used_ref[0]
    last = pl.program_id(1) == pl.num_programs(1) - 1

    @pl.when(live)
    def _():
        _swiglu_step(x_ref, w1_ref, w3_ref, w2_ref, acc_ref)

    @pl.when(live & last)
    def _():
        o_ref[...] = acc_ref[...].astype(o_ref.dtype)

    @pl.when(jnp.logical_not(live) & last)
    def _():
        o_ref[...] = jnp.zeros_like(o_ref)


def _ffn_experts(tile_expert, tiles_used, x_sorted, w13, w2, tm, tf):
    rows = x_sorted.shape[0]
    nf = D_FF // tf
    fblk = lambda i, j, nu: jnp.where(i < nu[0], j, nf - 1)
    grid_spec = pltpu.PrefetchScalarGridSpec(
        num_scalar_prefetch=2,
        grid=(rows // tm, nf),
        in_specs=[
            pl.BlockSpec((tm, D_MODEL), lambda i, j, te, nu: (i, 0)),
            pl.BlockSpec((None, D_MODEL, tf), lambda i, j, te, nu: (te[i], 0, fblk(i, j, nu))),
            pl.BlockSpec((None, D_MODEL, tf), lambda i, j, te, nu: (te[i], 0, nf + fblk(i, j, nu))),
            pl.BlockSpec((None, tf, D_MODEL), lambda i, j, te, nu: (te[i], fblk(i, j, nu), 0)),
        ],
        out_specs=pl.BlockSpec((tm, D_MODEL), lambda i, j, te, nu: (i, 0)),
        scratch_shapes=[pltpu.VMEM((tm, D_MODEL), F32)],
    )
    return pl.pallas_call(
        _ffn_expert_kernel,
        grid_spec=grid_spec,
        out_shape=jax.ShapeDtypeStruct((rows, D_MODEL), BF16),
        compiler_params=_cparams(("parallel", "arbitrary")),
        name="ffn_experts",
    )(tile_expert, tiles_used, x_sorted, w13, w13, w2)


def _router_kernel(x_ref, wr_ref, idx_ref, wt_ref):
    logits = jnp.dot(x_ref[...], wr_ref[...], preferred_element_type=F32,
                     precision=lax.Precision.HIGHEST)
    lane = _lane(logits.shape)
    logits = jnp.where(lane < N_EXPERTS, logits, -jnp.inf)
    v1 = jnp.max(logits, axis=-1, keepdims=True)
    i1 = jnp.min(jnp.where(logits == v1, lane, LANES), axis=-1, keepdims=True)
    rest = jnp.where(lane == i1, -jnp.inf, logits)
    v2 = jnp.max(rest, axis=-1, keepdims=True)
    i2 = jnp.min(jnp.where(rest == v2, lane, LANES), axis=-1, keepdims=True)
    e2 = jnp.exp(v2 - v1)
    w1 = 1.0 / (1.0 + e2)
    idx_ref[...] = jnp.where(lane == 0, i1, jnp.where(lane == 1, i2, 0))
    wt_ref[...] = jnp.where(lane == 0, w1, jnp.where(lane == 1, e2 * w1, 0.0))


def _router(x, w_router, tm):
    T = x.shape[0]
    wr = jnp.pad(w_router, ((0, 0), (0, LANES - N_EXPERTS)))
    row = lambda i: (i, 0)
    return pl.pallas_call(
        _router_kernel,
        grid=(T // tm,),
        in_specs=[pl.BlockSpec((tm, D_MODEL), row), pl.BlockSpec((D_MODEL, LANES), lambda i: (0, 0))],
        out_specs=[pl.BlockSpec((tm, LANES), row), pl.BlockSpec((tm, LANES), row)],
        out_shape=[jax.ShapeDtypeStruct((T, LANES), jnp.int32), jax.ShapeDtypeStruct((T, LANES), F32)],
        compiler_params=_cparams(("parallel",)),
        name="router",
    )(x, wr)


def _moe_out_kernel(h_ref, y0_ref, y1_ref, wt_ref, lng_ref, lnb_ref, o_ref):
    wt = wt_ref[...]
    f = wt[:, 0:1] * y0_ref[...] + wt[:, 1:2] * y1_ref[...]
    o_ref[...] = _layernorm(DEEPNORM_ALPHA * h_ref[...] + f, lng_ref[...], lnb_ref[...])


def _moe_out(h, y0, y1, wt, lng, lnb, tm):
    T = h.shape[0]
    row = lambda i: (i, 0)
    const = lambda i: (0, 0)
    wide = pl.BlockSpec((tm, D_MODEL), row)
    return pl.pallas_call(
        _moe_out_kernel,
        grid=(T // tm,),
        in_specs=[wide, wide, wide, pl.BlockSpec((tm, LANES), row),
                  pl.BlockSpec((1, D_MODEL), const), pl.BlockSpec((1, D_MODEL), const)],
        out_specs=wide,
        out_shape=jax.ShapeDtypeStruct((T, D_MODEL), F32),
        compiler_params=_cparams(("parallel",)),
        name="moe_out",
    )(h, y0, y1, wt, lng, lnb)


def _moe(h, h_bf16, w_router, w13, w2, lng, lnb, tm_router, tm_e, tf):
    T = h.shape[0]
    idx, wt = _router(h, w_router, tm_router)
    e_flat = jnp.concatenate([idx[:, s] for s in range(TOP_K)])
    onehot = (e_flat[None, :] == jnp.arange(N_EXPERTS, dtype=jnp.int32)[:, None]).astype(jnp.int32)
    counts = jnp.sum(onehot, axis=1)
    rank = jnp.sum((jnp.cumsum(onehot, axis=1) - onehot) * onehot, axis=0)
    padded = (counts + tm_e - 1) // tm_e * tm_e
    ends = jnp.cumsum(padded)
    pos = jnp.sum(onehot * (ends - padded)[:, None], axis=0) + rank
    rows = TOP_K * T + N_EXPERTS * tm_e
    token = jnp.arange(TOP_K * T, dtype=jnp.int32) % T
    src = (jnp.arange(rows, dtype=jnp.int32) % T).at[pos].set(
        token, unique_indices=True, mode="promise_in_bounds")
    tile_start = jnp.arange(rows // tm_e, dtype=jnp.int32) * tm_e
    tile_expert = jnp.minimum(jnp.sum((tile_start[:, None] >= ends[None, :]).astype(jnp.int32), axis=1),
                              N_EXPERTS - 1)
    x_sorted = h_bf16.at[src].get(mode="promise_in_bounds")
    tiles_used = (ends[N_EXPERTS - 1:] // tm_e).astype(jnp.int32)
    y = _ffn_experts(tile_expert, tiles_used, x_sorted, w13, w2, tm_e, tf)
    y0 = y.at[pos[:T]].get(mode="promise_in_bounds")
    y1 = y.at[pos[T:]].get(mode="promise_in_bounds")
    return _moe_out(h, y0, y1, wt, lng, lnb, tm_router)


def _rot_cols(w, block):
    d, n = w.shape
    w4 = w.reshape(d, n // block, 2, block // 2)
    return jnp.concatenate([-w4[:, :, 1], w4[:, :, 0]], axis=2).reshape(d, n)


def _attention_weights(w_in):
    o = np.cumsum([0, MLA_Q_LORA, MLA_KV_LORA, MLA_ROPE, _DIL_COLS, GQA_Q_HEADS * HEAD_DIM,
                   2 * GQA_KV_HEADS * HEAD_DIM, _DIFF_COLS, N_BRANCH * D_MODEL])
    cq, ckv, kr, dil, gq, gkv, diff, gate = (w_in[:, o[i]:o[i + 1]] for i in range(8))
    z = lambda n: jnp.zeros((D_MODEL, n), w_in.dtype)
    mla = [cq, z(64), ckv, z(64), kr, z(32), z(64), _rot_cols(kr, MLA_ROPE), z(32)]
    gqa = [gq, _rot_cols(gq, 32), gkv[:, :128], _rot_cols(gkv[:, :128], 32), gkv[:, 128:]]
    small = jnp.concatenate(mla + gqa, axis=1)
    return [w.astype(BF16) for w in (small, diff, dil)], gate.astype(BF16)


def _mla_up_weights(w_uq, w_ukv):
    zq = lambda n: jnp.zeros((MLA_Q_LORA, n), w_uq.dtype)
    per = MLA_NOPE + MLA_ROPE
    plain, rot = [], []
    for h in range(MLA_HEADS):
        nope = w_uq[:, h * per:h * per + MLA_NOPE]
        rope = w_uq[:, h * per + MLA_NOPE:(h + 1) * per]
        plain += [nope, rope, zq(32)]
        rot += [zq(64), _rot_cols(rope, MLA_ROPE), zq(32)]
    wuq = jnp.concatenate(plain + rot, axis=1)
    wuq = jnp.pad(wuq, ((0, _Q_LORA_PAD - MLA_Q_LORA), (0, 0)))
    zk = jnp.zeros((MLA_KV_LORA, 64), w_ukv.dtype)
    per = MLA_NOPE + MLA_V
    ks, vs = [], []
    for h in range(MLA_HEADS):
        ks += [w_ukv[:, h * per:h * per + MLA_NOPE], zk]
        vs += [w_ukv[:, h * per + MLA_NOPE:(h + 1) * per], zk]
    wukv = jnp.concatenate(ks + vs, axis=1)
    return wuq.astype(BF16), wukv.astype(BF16)


def _rope_angles(pos, dim):
    inv = ROPE_THETA ** (-(jnp.arange(0, dim, 2, dtype=F32) / dim))
    return pos[:, None] * inv[None, :]


def _position_tables(seq):
    rows = seq // GRID_W
    row_idx = jnp.repeat(jnp.arange(rows, dtype=F32), GRID_W)
    col_idx = jnp.tile(jnp.arange(GRID_W, dtype=F32), rows)
    a1 = _rope_angles(jnp.arange(seq, dtype=F32), MLA_ROPE)
    ar = _rope_angles(row_idx, HEAD_DIM // 2)
    ac = _rope_angles(col_idx, HEAD_DIM // 2)
    z = lambda n: jnp.zeros((seq, n), F32)
    c1 = jnp.concatenate([jnp.ones((seq, MLA_NOPE), F32)] + [jnp.cos(a1)] * 2 + [z(32)], axis=1)
    s1 = jnp.concatenate([z(MLA_NOPE)] + [jnp.sin(a1)] * 2 + [z(32)], axis=1)
    ca = jnp.concatenate(([jnp.cos(ar)] * 2 + [jnp.cos(ac)] * 2) * 2, axis=1)
    sa = jnp.concatenate(([jnp.sin(ar)] * 2 + [jnp.sin(ac)] * 2) * 2, axis=1)
    return [c1, s1, ca, sa]


def _gqa_gain_rows(q_norm, k_norm):
    def swap(g):
        return jnp.concatenate([g[16:32], g[0:16], g[48:64], g[32:48]])

    sg = HEAD_DIM ** -0.5 * LOG2E
    rows = [sg * q_norm, sg * swap(q_norm), k_norm, swap(k_norm)]
    return jnp.stack([jnp.concatenate([r, r]) for r in rows])


def _pick(n, pref):
    t = min(n, pref)
    assert n % t == 0, (n, pref)
    return t


class _Tiles(NamedTuple):
    rows: int
    flash: int
    diff: int
    ffn_rows: int
    ffn_cols: int


def _plan_tiles(seq, tokens):
    return _Tiles(rows=_pick(seq, 512), flash=_pick(seq, 1024), diff=_pick(seq, 512),
                  ffn_rows=_pick(tokens, 1024), ffn_cols=D_FF // 4)


def kernel(x, ln_emb_g, ln_emb_b, w_in, b_gate, mla_q_norm, mla_kv_norm, mla_w_uq, mla_w_ukv,
           gqa_q_norm, gqa_k_norm, diff_lambda, diff_subln, w_branch, w_out, ln1_g, ln1_b,
           ffn_w13, ffn_w2, moe_router, moe_w13, moe_w2, ln2_g, ln2_b):
    batch, seq, _ = x.shape
    T = batch * seq
    tiles = _plan_tiles(seq, T)
    row = lambda v: v.reshape(1, -1)
    pos_tabs = _position_tables(seq)

    h = x.reshape(T, D_MODEL)
    for l in range(DEPTH):
        lam_init = 0.8 - 0.6 * float(np.exp(-0.3 * l))
        weights, wg = _attention_weights(w_in[l])
        wuq, wukv = _mla_up_weights(mla_w_uq[l], mla_w_ukv[l])
        gains = _gqa_gain_rows(gqa_q_norm[l], gqa_k_norm[l])
        nq = jnp.pad(mla_q_norm[l], (0, _Q_LORA_PAD - MLA_Q_LORA)).reshape(1, _Q_LORA_PAD)
        res = _project(h, batch, seq, l == 0, row(ln_emb_g), row(ln_emb_b), weights, wuq, wukv,
                       nq, row(mla_kv_norm[l]), gains, pos_tabs, tiles.rows)
        if l == 0:
            h, res = res[0], res[1:]
        mq, mk, mv, gq, gk, gv, dq, dk, dv = res[:9]
        dil = res[9:]
        ya = _flash(mq, mk, mv, tiles.flash, tiles.flash).reshape(T, BRANCH_W)
        yc = _flash(gq, gk, gv, tiles.flash, tiles.flash).reshape(T, BRANCH_W)
        sub = jnp.pad(diff_subln[l], (0, LANES - DIFF_V)).reshape(1, LANES)
        yd = _diff_attention(dq, dk, dv, diff_lambda[l], sub, lam_init, tiles.diff).reshape(T, BRANCH_W)
        dil_o, dil_lse = [], []
        for g in range(len(DIL_GROUPS)):
            o, lse = _dilated_group(dil[g], dil[3 + g], dil[6 + g], g)
            dil_o.append(o)
            dil_lse.append(lse)
        dense = l % 2 == 0
        mixed = _mix(h, ya, yc, yd, dil_o, dil_lse, wg, b_gate[l], w_branch[l].astype(BF16),
                     w_out[l].astype(BF16), row(ln1_g[l]), row(ln1_b[l]), tiles.rows, not dense)
        if dense:
            h = _ffn_dense(mixed[0], ffn_w13[l // 2], ffn_w2[l // 2],
                           row(ln2_g[l]), row(ln2_b[l]), tiles.ffn_rows, tiles.ffn_cols)
        else:
            h = _moe(mixed[0], mixed[1], moe_router[l // 2], moe_w13[l // 2], moe_w2[l // 2],
                     row(ln2_g[l]), row(ln2_b[l]), tiles.rows, tiles.ffn_rows, tiles.ffn_cols)
    return h.reshape(batch, seq, D_MODEL)
```

```python
import functools
from typing import NamedTuple

import jax
import jax.numpy as jnp
import numpy as np
from jax import lax
from jax.experimental import pallas as pl
from jax.experimental.pallas import tpu as pltpu

D_MODEL = 1024
GRID_W = 64
HEAD_DIM = 64
N_BRANCH = 4
BRANCH_W = 256
MLA_HEADS = 4
MLA_NOPE = 64
MLA_ROPE = 32
MLA_V = 64
MLA_Q_LORA = 192
MLA_KV_LORA = 128
DIL_GROUPS = ((128, 1), (512, 4), (2048, 16))
DIL_HEADS = 4
GQA_Q_HEADS = 4
GQA_KV_HEADS = 2
DIFF_HEADS = 4
DIFF_D = 32
DIFF_V = 64
D_FF = 3584
N_EXPERTS = 8
TOP_K = 2
ROPE_THETA = 10000.0
LN_EPS = 1e-5
RMS_EPS = 1e-6
DIFF_NORM_EPS = 1e-5
DEPTH = 2
DEEPNORM_ALPHA = (2 * DEPTH) ** 0.25

LANES = 128
HALF = LANES // 2
MXU_COLS = 256
VMEM_BYTES = 64 * 1024 * 1024
VMEM_LIMIT = VMEM_BYTES - 8 * 1024 * 1024

LOG2E = 1.4426950408889634
NEG = -1e30
BF16 = jnp.bfloat16
F32 = jnp.float32

_Q_LORA_PAD = 2 * LANES
_MLA_COLS = _Q_LORA_PAD + MLA_KV_LORA + 2 * LANES
_GQA_COLS = 7 * LANES
_DIFF_COLS = 3 * DIFF_HEADS * DIFF_V
_DIL_COLS = 3 * len(DIL_GROUPS) * BRANCH_W
_SMALL_COLS = _MLA_COLS + _GQA_COLS


def _cparams(sem):
    return pltpu.CompilerParams(dimension_semantics=sem, vmem_limit_bytes=VMEM_LIMIT)


def _resident(shape, index_map):
    return pl.BlockSpec(shape, index_map, pipeline_mode=pl.Buffered(1))


def _layernorm(x, g, b):
    mu = jnp.mean(x, axis=-1, keepdims=True)
    xc = x - mu
    var = jnp.mean(xc * xc, axis=-1, keepdims=True)
    return xc * lax.rsqrt(var + LN_EPS) * g + b


def _lane(shape):
    return lax.broadcasted_iota(jnp.int32, shape, len(shape) - 1)


def _split_pair(x):
    lo_mask = _lane(x.shape) < HALF
    lo = jnp.where(lo_mask, x, 0.0)
    hi = jnp.where(lo_mask, pltpu.roll(x, HALF, axis=1), 0.0)
    return lo, hi


def _join_pair(lo, hi):
    return jnp.where(_lane(lo.shape) < HALF, lo, pltpu.roll(hi, HALF, axis=1))


def _with_ones(v):
    return jnp.where(_lane(v.shape) == HALF, 1.0, v)


def _diff_slope(h):
    return 2.0 ** (-8.0 * (h + 1) / DIFF_HEADS) * LOG2E


def _split3(val):
    f1 = val.astype(BF16).astype(F32)
    r1 = val - f1
    f2 = r1.astype(BF16).astype(F32)
    f3 = (r1 - f2).astype(BF16).astype(F32)
    return f1, f2, f3


_FEAT = 2 * DIFF_D


def _key_bias_features(val):
    f1, f2, f3 = _split3(val)
    lane = _lane(val.shape)
    ones = jnp.where((lane >= _FEAT + 3) & (lane < _FEAT + 6), 1.0, 0.0)
    return jnp.where(lane == _FEAT, f1, jnp.where(lane == _FEAT + 1, f2, jnp.where(lane == _FEAT + 2, f3, ones)))


def _proj_kernel(apply_ln, tiles_per_seq, *refs):
    (x_ref, lng_ref, lnb_ref, ws_ref, wdiff_ref, wdil_ref, wuq_ref, wukv_ref, nq_ref, nkv_ref,
     gains_ref, c1_ref, s1_ref, ca_ref, sa_ref) = refs[:15]
    outs = refs[15:]
    if apply_ln:
        h_ref, outs = outs[0], outs[1:]
    (mq_ref, mk_ref, mv_ref, gq_ref, gk_ref, gv_ref, dq_ref, dk_ref, dv_ref) = outs[:9]
    dil_refs = outs[9:]

    x = x_ref[...]
    if apply_ln:
        x = _layernorm(x, lng_ref[...], lnb_ref[...])
        h_ref[...] = x
    hb = x.astype(BF16)

    def proj(w_ref, off, n):
        return jnp.dot(hb, w_ref[:, off:off + n], preferred_element_type=F32)

    p_small = proj(ws_ref, 0, _SMALL_COLS)
    p = p_small[:, :_MLA_COLS]
    cq, ckv, kr, krr = p[:, 0:256], p[:, 256:384], p[:, 384:512], p[:, 512:640]
    rq = lax.rsqrt(jnp.sum(cq * cq, axis=-1, keepdims=True) * (1.0 / MLA_Q_LORA) + RMS_EPS)
    cqn = (cq * rq * nq_ref[...]).astype(BF16)
    q2 = jnp.dot(cqn, wuq_ref[...], preferred_element_type=F32)
    rkv = lax.rsqrt(jnp.sum(ckv * ckv, axis=-1, keepdims=True) * (1.0 / MLA_KV_LORA) + RMS_EPS)
    ckvn = (ckv * rkv * nkv_ref[...]).astype(BF16)
    kv2 = jnp.dot(ckvn, wukv_ref[...], preferred_element_type=F32)
    c1, s1 = c1_ref[...], s1_ref[...]
    k_rope = kr * c1 + krr * s1
    mla_scale = (MLA_NOPE + MLA_ROPE) ** -0.5 * LOG2E
    cq_t, sq_t = c1 * mla_scale, s1 * mla_scale
    for h in range(MLA_HEADS):
        qh = q2[:, h * LANES:(h + 1) * LANES]
        qrh = q2[:, (MLA_HEADS + h) * LANES:(MLA_HEADS + h + 1) * LANES]
        mq_ref[h] = (qh * cq_t + qrh * sq_t).astype(BF16)
        mk_ref[h] = (kv2[:, h * LANES:(h + 1) * LANES] + k_rope).astype(BF16)
        vh = kv2[:, (MLA_HEADS + h) * LANES:(MLA_HEADS + h + 1) * LANES]
        mv_ref[h] = _with_ones(vh).astype(BF16)

    p = p_small[:, _MLA_COLS:]

    def head_rms(xs):
        ss = xs * xs
        lo_mask = _lane(xs.shape) < HALF
        s_all = jnp.sum(ss, axis=-1, keepdims=True)
        s_lo = jnp.sum(jnp.where(lo_mask, ss, 0.0), axis=-1, keepdims=True)
        return lax.rsqrt(jnp.where(lo_mask, s_lo, s_all - s_lo) * (1.0 / HEAD_DIM) + RMS_EPS)

    ca, sa = ca_ref[...], sa_ref[...]
    aq, bq = ca * gains_ref[0:1, :], sa * gains_ref[1:2, :]
    ak, bk = ca * gains_ref[2:3, :], sa * gains_ref[3:4, :]
    for s in range(2):
        qs = p[:, s * LANES:(s + 1) * LANES]
        qrs = p[:, 256 + s * LANES:256 + (s + 1) * LANES]
        qn = head_rms(qs) * (qs * aq + qrs * bq)
        lo, hi = _split_pair(qn)
        gq_ref[2 * s] = lo.astype(BF16)
        gq_ref[2 * s + 1] = hi.astype(BF16)
    ks, krs = p[:, 512:640], p[:, 640:768]
    kn = head_rms(ks) * (ks * ak + krs * bk)
    lo, hi = _split_pair(kn)
    gk_ref[0] = lo.astype(BF16)
    gk_ref[1] = hi.astype(BF16)
    lo, hi = _split_pair(p[:, 768:896])
    gv_ref[0] = _with_ones(lo).astype(BF16)
    gv_ref[1] = _with_ones(hi).astype(BF16)

    p = proj(wdiff_ref, 0, _DIFF_COLS)
    dscale = DIFF_D ** -0.5 * LOG2E
    tm = x.shape[0]
    pos = (pl.program_id(0) % tiles_per_seq * tm
           + lax.broadcasted_iota(jnp.int32, (tm, LANES), 0)).astype(F32)
    for s in range(2):
        lo, hi = _split_pair(p[:, s * LANES:(s + 1) * LANES] * dscale)
        dq_ref[2 * s] = lo.astype(BF16)
        dq_ref[2 * s + 1] = hi.astype(BF16)
        lo, hi = _split_pair(p[:, 256 + s * LANES:256 + (s + 1) * LANES])
        dk_ref[2 * s] = (lo + _key_bias_features(_diff_slope(2 * s) * pos)).astype(BF16)
        dk_ref[2 * s + 1] = (hi + _key_bias_features(_diff_slope(2 * s + 1) * pos)).astype(BF16)
        lo, hi = _split_pair(p[:, 512 + s * LANES:512 + (s + 1) * LANES])
        dv_ref[2 * s] = _with_ones(lo).astype(BF16)
        dv_ref[2 * s + 1] = _with_ones(hi).astype(BF16)

    lscale = HEAD_DIM ** -0.5 * LOG2E
    for i in range(9):
        seg = proj(wdil_ref, i * BRANCH_W, BRANCH_W)
        if i < 3:
            seg = seg * lscale
        dil_refs[i][0] = seg[:, :LANES]
        dil_refs[i][1] = seg[:, LANES:]


def _project(x2d, batch, seq, apply_ln, lng, lnb, weights, wuq, wukv, nq, nkv, gains, tables, tm):
    T = x2d.shape[0]
    nst = seq // tm
    grid = (T // tm,)
    row = lambda i: (i, 0)
    const = lambda i: (0, 0)
    tab = lambda i: (i % nst, 0)
    head = lambda i: (i // nst, 0, i % nst, 0)

    in_specs = [
        pl.BlockSpec((tm, D_MODEL), row),
        pl.BlockSpec((1, D_MODEL), const),
        pl.BlockSpec((1, D_MODEL), const),
        _resident((D_MODEL, _SMALL_COLS), const),
        _resident((D_MODEL, _DIFF_COLS), const),
        _resident((D_MODEL, _DIL_COLS), const),
        _resident((_Q_LORA_PAD, 8 * LANES), const),
        _resident((LANES, 8 * LANES), const),
        pl.BlockSpec((1, _Q_LORA_PAD), const),
        pl.BlockSpec((1, LANES), const),
        pl.BlockSpec((4, LANES), const),
    ] + [pl.BlockSpec((tm, LANES), tab)] * 4

    def heads_out(n):
        return (jax.ShapeDtypeStruct((batch, n, seq, LANES), BF16),
                pl.BlockSpec((None, n, tm, LANES), head))

    outs = []
    if apply_ln:
        outs.append((jax.ShapeDtypeStruct((T, D_MODEL), F32), pl.BlockSpec((tm, D_MODEL), row)))
    outs += [heads_out(MLA_HEADS)] * 3
    outs += [heads_out(GQA_Q_HEADS), heads_out(GQA_KV_HEADS), heads_out(GQA_KV_HEADS)]
    outs += [heads_out(DIFF_HEADS)] * 3
    outs += [(jax.ShapeDtypeStruct((batch, 2, seq, LANES), F32), pl.BlockSpec((None, 2, tm, LANES), head))] * 9

    res = pl.pallas_call(
        functools.partial(_proj_kernel, apply_ln, nst),
        grid=grid,
        in_specs=in_specs,
        out_specs=[o[1] for o in outs],
        out_shape=[o[0] for o in outs],
        compiler_params=_cparams(("parallel",)),
        name="proj_prep",
    )(x2d, lng, lnb, *weights, wuq, wukv, nq, nkv, gains, *tables)
    return res


def _normalize(acc):
    return acc / acc[:, HALF:HALF + 1]


def _scores(q, k):
    return lax.dot_general(q, k, (((1,), (1,)), ((), ())), preferred_element_type=F32)


def _softmax_update(s, v, acc_ref, m_ref, idx):
    blocks = [s[:, j * LANES:(j + 1) * LANES] for j in range(s.shape[1] // LANES)]
    part = functools.reduce(jnp.maximum, blocks)
    m_prev = m_ref[idx]
    m_new = jnp.maximum(m_prev, jnp.max(part, axis=-1, keepdims=True))
    p = jnp.concatenate([jnp.exp2((b - m_new).astype(BF16)) for b in blocks], axis=1)
    acc_ref[idx] = (jnp.exp2(m_prev - m_new) * acc_ref[idx]
                    + jnp.dot(p, v, preferred_element_type=F32))
    m_ref[idx] = m_new


def _flash_kernel(n_rep, tk, q_ref, k_ref, v_ref, o_ref, acc_ref, m_ref):
    n_heads = q_ref.shape[0]
    seq = k_ref.shape[1]
    acc_ref[...] = jnp.zeros_like(acc_ref)
    m_ref[...] = jnp.full_like(m_ref, NEG)

    def body(c, carry):
        ks = pl.multiple_of(c * tk, tk)
        scores = [_scores(q_ref[h], k_ref[h // n_rep, pl.ds(ks, tk), :]) for h in range(n_heads)]
        for h in range(n_heads):
            _softmax_update(scores[h], v_ref[h // n_rep, pl.ds(ks, tk), :], acc_ref, m_ref, h)
        return carry

    lax.fori_loop(0, seq // tk, body, 0)
    outs = [_normalize(acc_ref[h]) for h in range(n_heads)]
    for s in range(n_heads // 2):
        o_ref[:, s * LANES:(s + 1) * LANES] = _join_pair(outs[2 * s], outs[2 * s + 1]).astype(o_ref.dtype)


def _flash(q, k, v, tq, tk):
    batch, n_heads, seq, _ = q.shape
    n_kv = k.shape[1]
    return pl.pallas_call(
        functools.partial(_flash_kernel, n_heads // n_kv, tk),
        grid=(batch, seq // tq),
        in_specs=[
            pl.BlockSpec((None, n_heads, tq, LANES), lambda b, i: (b, 0, i, 0)),
            _resident((None, n_kv, seq, LANES), lambda b, i: (b, 0, 0, 0)),
            _resident((None, n_kv, seq, LANES), lambda b, i: (b, 0, 0, 0)),
        ],
        out_specs=pl.BlockSpec((None, tq, n_heads * HALF), lambda b, i: (b, i, 0)),
        out_shape=jax.ShapeDtypeStruct((batch, seq, n_heads * HALF), BF16),
        scratch_shapes=[pltpu.VMEM((n_heads, tq, LANES), F32), pltpu.VMEM((n_heads, tq, LANES), F32)],
        compiler_params=_cparams(("parallel", "parallel")),
        name="flash_attn",
    )(q, k, v)


def _diff_kernel(lam_init, q_ref, k_ref, v_ref, lp_ref, sub_ref, o_ref, acc_ref, m_ref, qv_ref):
    n_heads, tq, _ = q_ref.shape
    seq = k_ref.shape[1]
    tile = pl.program_id(1)
    t0 = tile * tq
    lp = lp_ref[...]
    lam = (jnp.exp(jnp.sum(lp[0:1] * lp[1:2], keepdims=True))
           - jnp.exp(jnp.sum(lp[2:3] * lp[3:4], keepdims=True)) + lam_init)
    lane = _lane((tq, LANES))
    tpos = (t0 + lax.broadcasted_iota(jnp.int32, (tq, LANES), 0)).astype(F32)
    in_f = (lane >= _FEAT) & (lane < _FEAT + 3)
    for h in range(n_heads):
        q = q_ref[h].astype(F32)
        g1, g2, g3 = _split3(2.0 * _diff_slope(h) * tpos)
        left = jnp.where(in_f, 1.0, 0.0)
        right = jnp.where(in_f, -1.0, jnp.where(lane == _FEAT + 3, g1, jnp.where(
            lane == _FEAT + 4, g2, jnp.where(lane == _FEAT + 5, g3, 0.0))))
        for m in range(2):
            qm = jnp.where((lane >= m * DIFF_D) & (lane < (m + 1) * DIFF_D), q, 0.0)
            qv_ref[0, 2 * h + m] = (qm + left).astype(BF16)
            qv_ref[1, 2 * h + m] = (qm + right).astype(BF16)
    acc_ref[...] = jnp.zeros_like(acc_ref)
    m_ref[...] = jnp.full_like(m_ref, NEG)
    chains = range(2 * n_heads)

    def update(scores, ks):
        for i in chains:
            _softmax_update(scores[i], v_ref[i // 2, pl.ds(ks, tq), :], acc_ref, m_ref, i)

    ks = pl.multiple_of(t0, tq)
    update([jnp.minimum(_scores(qv_ref[0, i], k_ref[i // 2, pl.ds(ks, tq), :]),
                        _scores(qv_ref[1, i], k_ref[i // 2, pl.ds(ks, tq), :])) for i in chains], ks)

    def body(c, carry):
        side = (c >= tile).astype(jnp.int32)
        ks = pl.multiple_of((c + side) * tq, tq)
        update([_scores(qv_ref[side, i], k_ref[i // 2, pl.ds(ks, tq), :]) for i in chains], ks)
        return carry

    n_other = seq // tq - 1
    if n_other % 2:
        body(n_other - 1, 0)
    lax.fori_loop(0, n_other // 2 * 2, body, 0, unroll=2)

    outs = []
    for h in range(n_heads):
        o = _normalize(acc_ref[2 * h]) - lam * _normalize(acc_ref[2 * h + 1])
        o = jnp.where(lane < DIFF_V, o, 0.0)
        ms = jnp.sum(o * o, axis=-1, keepdims=True) * (1.0 / DIFF_V)
        outs.append(o * lax.rsqrt(ms + DIFF_NORM_EPS) * sub_ref[...] * (1.0 - lam_init))
    for s in range(n_heads // 2):
        o_ref[:, s * LANES:(s + 1) * LANES] = _join_pair(outs[2 * s], outs[2 * s + 1]).astype(o_ref.dtype)


def _diff_attention(q, k, v, lam_params, subln, lam_init, tq):
    batch, n_heads, seq, _ = q.shape
    return pl.pallas_call(
        functools.partial(_diff_kernel, lam_init),
        grid=(batch, seq // tq),
        in_specs=[
            pl.BlockSpec((None, n_heads, tq, LANES), lambda b, i: (b, 0, i, 0)),
            _resident((None, n_heads, seq, LANES), lambda b, i: (b, 0, 0, 0)),
            _resident((None, n_heads, seq, LANES), lambda b, i: (b, 0, 0, 0)),
            pl.BlockSpec((4, DIFF_D), lambda b, i: (0, 0)),
            pl.BlockSpec((1, LANES), lambda b, i: (0, 0)),
        ],
        out_specs=pl.BlockSpec((None, tq, n_heads * HALF), lambda b, i: (b, i, 0)),
        out_shape=jax.ShapeDtypeStruct((batch, seq, n_heads * HALF), BF16),
        scratch_shapes=[pltpu.VMEM((2 * n_heads, tq, LANES), F32)] * 2
        + [pltpu.VMEM((2, 2 * n_heads, tq, LANES), BF16)],
        compiler_params=_cparams(("parallel", "parallel")),
        name="diff_attn",
    )(q, k, v, lam_params, subln)


_DIL_BLOCK = 2048
_DIL_Q = 128


def _dil_kernel(group, dil, seq, q_ref, kp_ref, kc_ref, kn_ref, vp_ref, vc_ref, vn_ref,
                o_ref, lse_ref, kw_ref, vw_ref):
    tb = q_ref.shape[1]
    halo = DIL_GROUPS[group][0] // 2
    assert halo == kp_ref.shape[1] and halo == (_DIL_Q // 2) * dil
    for w_ref, p_ref, c_ref, n_ref in ((kw_ref, kp_ref, kc_ref, kn_ref), (vw_ref, vp_ref, vc_ref, vn_ref)):
        w_ref[:, 0:halo] = p_ref[...]
        w_ref[:, halo:halo + tb] = c_ref[...]
        w_ref[:, halo + tb:halo + tb + halo] = n_ref[...]
    t0 = pl.program_id(1) * tb
    shape = (_DIL_Q, 2 * _DIL_Q)
    rel = lax.broadcasted_iota(jnp.int32, shape, 1) - lax.broadcasted_iota(jnp.int32, shape, 0) - _DIL_Q // 2
    absrel = jnp.abs(rel).astype(F32)
    band = jnp.where(absrel <= _DIL_Q // 2, 0.0, -NEG)
    n_slopes = len(DIL_GROUPS) * DIL_HEADS
    bias = [2.0 ** (-8.0 * (group * DIL_HEADS + h + 1) / n_slopes) * dil * LOG2E * absrel + band
            for h in range(DIL_HEADS)]
    col = lax.broadcasted_iota(jnp.int32, (1, 2 * _DIL_Q), 1)
    lo_mask = _lane((_DIL_Q, LANES)) < HALF
    for unit in range(tb // _DIL_Q):
        first = (unit // dil) * _DIL_Q * dil + unit % dil
        rows = pl.ds(first, _DIL_Q, stride=dil)
        win = pl.ds(first, 2 * _DIL_Q, stride=dil)
        key_token = t0 + (first - halo) + dil * col
        off_seq = jnp.where(key_token >= 0, jnp.where(key_token < seq, 0.0, NEG), NEG)
        for s in range(2):
            qs = q_ref[s, rows, :].astype(BF16)
            ksl = kw_ref[s, win, :].astype(BF16)
            vsl = vw_ref[s, win, :].astype(BF16)
            res = []
            for part in range(2):
                keep = lo_mask if part == 0 else jnp.logical_not(lo_mask)
                sc = _scores(jnp.where(keep, qs, jnp.zeros_like(qs)), ksl) - bias[2 * s + part] + off_seq
                m = jnp.max(sc, axis=-1, keepdims=True)
                p = jnp.exp2(sc - m)
                l = jnp.sum(p, axis=-1, keepdims=True)
                o = jnp.dot(p.astype(BF16), vsl, preferred_element_type=F32) / l
                res.append((o, m + jnp.log2(l)))
            o_ref[s, rows, :] = jnp.where(lo_mask, res[0][0], res[1][0])
            lse_ref[s, rows, :] = jnp.where(lo_mask, res[0][1], res[1][1])


def _dilated_group(q, k, v, group):
    batch, _, seq, _ = q.shape
    dil = DIL_GROUPS[group][1]
    halo = DIL_GROUPS[group][0] // 2
    tb = _DIL_BLOCK
    assert seq % tb == 0 and seq // dil >= 2 * _DIL_Q and tb % (_DIL_Q * dil) == 0
    per = tb // halo
    cur = pl.BlockSpec((None, 2, tb, LANES), lambda b, i: (b, 0, i, 0))
    prev = pl.BlockSpec((None, 2, halo, LANES), lambda b, i: (b, 0, jnp.maximum(i * per - 1, 0), 0))
    nxt = pl.BlockSpec((None, 2, halo, LANES),
                       lambda b, i: (b, 0, jnp.minimum((i + 1) * per, seq // halo - 1), 0))
    out_shape = jax.ShapeDtypeStruct((batch, 2, seq, LANES), F32)
    o, lse = pl.pallas_call(
        functools.partial(_dil_kernel, group, dil, seq),
        grid=(batch, seq // tb),
        in_specs=[cur, prev, cur, nxt, prev, cur, nxt],
        out_specs=[cur, cur],
        out_shape=[out_shape, out_shape],
        scratch_shapes=[pltpu.VMEM((2, tb + 2 * halo, LANES), F32)] * 2,
        compiler_params=_cparams(("parallel", "parallel")),
        name=f"dilated_g{group}",
    )(q, k, k, k, v, v, v)
    return o, lse


def _mix_kernel(h_ref, ya_ref, yc_ref, yd_ref, o0_ref, o1_ref, o2_ref, l0_ref, l1_ref, l2_ref,
                wg_ref, bg_ref, wb_ref, wo_ref, lng_ref, lnb_ref, out_ref, *maybe_bf16_ref):
    tm = h_ref.shape[0]
    for rows in (pl.ds(0, tm // 2), pl.ds(tm // 2, tm // 2)):
        h = h_ref[rows, :]
        hb = h.astype(BF16)
        slabs = []
        for s in range(2):
            l0, l1, l2 = l0_ref[s, rows, :], l1_ref[s, rows, :], l2_ref[s, rows, :]
            mx = jnp.maximum(jnp.maximum(l0, l1), l2)
            e0, e1, e2 = jnp.exp2(l0 - mx), jnp.exp2(l1 - mx), jnp.exp2(l2 - mx)
            slabs.append(((e0 * o0_ref[s, rows, :] + e1 * o1_ref[s, rows, :] + e2 * o2_ref[s, rows, :])
                          / (e0 + e1 + e2)).astype(BF16))
        ys = (ya_ref[rows, :], jnp.concatenate(slabs, axis=1), yc_ref[rows, :], yd_ref[rows, :])
        acc = None
        for n in range(N_BRANCH):
            logit = jnp.dot(hb, wg_ref[:, n * D_MODEL:(n + 1) * D_MODEL],
                            preferred_element_type=F32) + bg_ref[n:n + 1, :]
            gate = 1.0 / (1.0 + jnp.exp(-logit))
            term = gate * jnp.dot(ys[n], wb_ref[n], preferred_element_type=F32)
            acc = term if acc is None else acc + term
        m = jnp.dot(acc.astype(BF16), wo_ref[...], preferred_element_type=F32)
        out = _layernorm(DEEPNORM_ALPHA * h + m, lng_ref[...], lnb_ref[...])
        out_ref[rows, :] = out
        for r in maybe_bf16_ref:
            r[rows, :] = out.astype(BF16)


def _mix(h, ya, yc, yd, dil_o, dil_lse, wg, bg, wb, wo, lng, lnb, tm, with_bf16_copy):
    T = h.shape[0]
    nst = dil_o[0].shape[2] // tm
    slab = pl.BlockSpec((None, 2, tm, LANES), lambda i: (i // nst, 0, i % nst, 0))
    row = lambda i: (i, 0)
    const = lambda i: (0, 0)
    wide = pl.BlockSpec((tm, D_MODEL), row)
    narrow = pl.BlockSpec((tm, BRANCH_W), row)
    return pl.pallas_call(
        _mix_kernel,
        grid=(T // tm,),
        in_specs=[wide] + [narrow] * 3 + [slab] * 6 + [
            _resident((D_MODEL, N_BRANCH * D_MODEL), const),
            pl.BlockSpec((N_BRANCH, D_MODEL), const),
            _resident((N_BRANCH, BRANCH_W, D_MODEL), lambda i: (0, 0, 0)),
            _resident((D_MODEL, D_MODEL), const),
            pl.BlockSpec((1, D_MODEL), const),
            pl.BlockSpec((1, D_MODEL), const),
        ],
        out_specs=[wide] * (2 if with_bf16_copy else 1),
        out_shape=[jax.ShapeDtypeStruct((T, D_MODEL), F32)]
        + ([jax.ShapeDtypeStruct((T, D_MODEL), BF16)] if with_bf16_copy else []),
        compiler_params=_cparams(("parallel",)),
        name="gate_mix",
    )(h, ya, yc, yd, *dil_o, *dil_lse, wg, bg, wb, wo, lng, lnb)


def _swiglu_step(xb_ref, w1_ref, w3_ref, w2_ref, acc_ref):
    @pl.when(pl.program_id(1) == 0)
    def _():
        acc_ref[...] = jnp.zeros_like(acc_ref)

    xb = xb_ref[...]
    tf = w1_ref.shape[1]
    full = tf // MXU_COLS * MXU_COLS
    mid = (full // MXU_COLS + 1) // 2 * MXU_COLS
    gated = []

    def gate(a, b):
        gated.append((a / (1.0 + jnp.exp(-a)) * b).astype(BF16))

    for lo, hi in ((0, mid), (mid, full)):
        gate(jnp.dot(xb, w1_ref[:, lo:hi].astype(BF16), preferred_element_type=F32),
             jnp.dot(xb, w3_ref[:, lo:hi].astype(BF16), preferred_element_type=F32))
    if full < tf:
        w13 = jnp.concatenate([w1_ref[:, full:tf].astype(BF16), w3_ref[:, full:tf].astype(BF16)], axis=1)
        ab = jnp.dot(xb, w13, preferred_element_type=F32)
        gate(ab[:, :tf - full], ab[:, tf - full:])
    acc_ref[...] += jnp.dot(jnp.concatenate(gated, axis=1), w2_ref[...].astype(BF16),
                            preferred_element_type=F32)


def _ffn_dense_kernel(x_ref, w1_ref, w3_ref, w2_ref, lng_ref, lnb_ref, o_ref, acc_ref, xb_ref):
    @pl.when(pl.program_id(1) == 0)
    def _():
        xb_ref[...] = x_ref[...].astype(BF16)

    _swiglu_step(xb_ref, w1_ref, w3_ref, w2_ref, acc_ref)

    @pl.when(pl.program_id(1) == pl.num_programs(1) - 1)
    def _():
        o_ref[...] = _layernorm(DEEPNORM_ALPHA * x_ref[...] + acc_ref[...], lng_ref[...], lnb_ref[...])


def _ffn_dense(x, w13, w2, lng, lnb, tm, tf):
    T = x.shape[0]
    nf = D_FF // tf
    return pl.pallas_call(
        _ffn_dense_kernel,
        grid=(T // tm, nf),
        in_specs=[
            pl.BlockSpec((tm, D_MODEL), lambda i, j: (i, 0)),
            pl.BlockSpec((D_MODEL, tf), lambda i, j: (0, j)),
            pl.BlockSpec((D_MODEL, tf), lambda i, j: (0, nf + j)),
            pl.BlockSpec((tf, D_MODEL), lambda i, j: (j, 0)),
            pl.BlockSpec((1, D_MODEL), lambda i, j: (0, 0)),
            pl.BlockSpec((1, D_MODEL), lambda i, j: (0, 0)),
        ],
        out_specs=pl.BlockSpec((tm, D_MODEL), lambda i, j: (i, 0)),
        out_shape=jax.ShapeDtypeStruct((T, D_MODEL), F32),
        scratch_shapes=[pltpu.VMEM((tm, D_MODEL), F32), pltpu.VMEM((tm, D_MODEL), BF16)],
        compiler_params=_cparams(("parallel", "arbitrary")),
        name="ffn_dense",
    )(x, w13, w13, w2, lng, lnb)


def _ffn_expert_kernel(te_ref, used_ref, x_ref, w1_ref, w3_ref, w2_ref, o_ref, acc_ref):
    del te_ref
    live = pl.program_id(0) < used_ref[0]
    last = pl.program_id(1) == pl.num_programs(1) - 1

    @pl.when(live)
    def _():
        _swiglu_step(x_ref, w1_ref, w3_ref, w2_ref, acc_ref)

    @pl.when(live & last)
    def _():
        o_ref[...] = acc_ref[...].astype(o_ref.dtype)

    @pl.when(jnp.logical_not(live) & last)
    def _():
        o_ref[...] = jnp.zeros_like(o_ref)


def _ffn_experts(tile_expert, tiles_used, x_sorted, w13, w2, tm, tf):
    rows = x_sorted.shape[0]
    nf = D_FF // tf
    fblk = lambda i, j, nu: jnp.where(i < nu[0], j, nf - 1)
    grid_spec = pltpu.PrefetchScalarGridSpec(
        num_scalar_prefetch=2,
        grid=(rows // tm, nf),
        in_specs=[
            pl.BlockSpec((tm, D_MODEL), lambda i, j, te, nu: (i, 0)),
            pl.BlockSpec((None, D_MODEL, tf), lambda i, j, te, nu: (te[i], 0, fblk(i, j, nu))),
            pl.BlockSpec((None, D_MODEL, tf), lambda i, j, te, nu: (te[i], 0, nf + fblk(i, j, nu))),
            pl.BlockSpec((None, tf, D_MODEL), lambda i, j, te, nu: (te[i], fblk(i, j, nu), 0)),
        ],
        out_specs=pl.BlockSpec((tm, D_MODEL), lambda i, j, te, nu: (i, 0)),
        scratch_shapes=[pltpu.VMEM((tm, D_MODEL), F32)],
    )
    return pl.pallas_call(
        _ffn_expert_kernel,
        grid_spec=grid_spec,
        out_shape=jax.ShapeDtypeStruct((rows, D_MODEL), BF16),
        compiler_params=_cparams(("parallel", "arbitrary")),
        name="ffn_experts",
    )(tile_expert, tiles_used, x_sorted, w13, w13, w2)


def _router_kernel(x_ref, wr_ref, idx_ref, wt_ref):
    logits = jnp.dot(x_ref[...], wr_ref[...], preferred_element_type=F32,
                     precision=lax.Precision.HIGHEST)
    lane = _lane(logits.shape)
    logits = jnp.where(lane < N_EXPERTS, logits, -jnp.inf)
    v1 = jnp.max(logits, axis=-1, keepdims=True)
    i1 = jnp.min(jnp.where(logits == v1, lane, LANES), axis=-1, keepdims=True)
    rest = jnp.where(lane == i1, -jnp.inf, logits)
    v2 = jnp.max(rest, axis=-1, keepdims=True)
    i2 = jnp.min(jnp.where(rest == v2, lane, LANES), axis=-1, keepdims=True)
    e2 = jnp.exp(v2 - v1)
    w1 = 1.0 / (1.0 + e2)
    idx_ref[...] = jnp.where(lane == 0, i1, jnp.where(lane == 1, i2, 0))
    wt_ref[...] = jnp.where(lane == 0, w1, jnp.where(lane == 1, e2 * w1, 0.0))


def _router(x, w_router, tm):
    T = x.shape[0]
    wr = jnp.pad(w_router, ((0, 0), (0, LANES - N_EXPERTS)))
    row = lambda i: (i, 0)
    return pl.pallas_call(
        _router_kernel,
        grid=(T // tm,),
        in_specs=[pl.BlockSpec((tm, D_MODEL), row), pl.BlockSpec((D_MODEL, LANES), lambda i: (0, 0))],
        out_specs=[pl.BlockSpec((tm, LANES), row), pl.BlockSpec((tm, LANES), row)],
        out_shape=[jax.ShapeDtypeStruct((T, LANES), jnp.int32), jax.ShapeDtypeStruct((T, LANES), F32)],
        compiler_params=_cparams(("parallel",)),
        name="router",
    )(x, wr)


def _moe_out_kernel(h_ref, y0_ref, y1_ref, wt_ref, lng_ref, lnb_ref, o_ref):
    wt = wt_ref[...]
    f = wt[:, 0:1] * y0_ref[...] + wt[:, 1:2] * y1_ref[...]
    o_ref[...] = _layernorm(DEEPNORM_ALPHA * h_ref[...] + f, lng_ref[...], lnb_ref[...])


def _moe_out(h, y0, y1, wt, lng, lnb, tm):
    T = h.shape[0]
    row = lambda i: (i, 0)
    const = lambda i: (0, 0)
    wide = pl.BlockSpec((tm, D_MODEL), row)
    return pl.pallas_call(
        _moe_out_kernel,
        grid=(T // tm,),
        in_specs=[wide, wide, wide, pl.BlockSpec((tm, LANES), row),
                  pl.BlockSpec((1, D_MODEL), const), pl.BlockSpec((1, D_MODEL), const)],
        out_specs=wide,
        out_shape=jax.ShapeDtypeStruct((T, D_MODEL), F32),
        compiler_params=_cparams(("parallel",)),
        name="moe_out",
    )(h, y0, y1, wt, lng, lnb)


def _moe(h, h_bf16, w_router, w13, w2, lng, lnb, tm_router, tm_e, tf):
    T = h.shape[0]
    idx, wt = _router(h, w_router, tm_router)
    e_flat = jnp.concatenate([idx[:, s] for s in range(TOP_K)])
    onehot = (e_flat[None, :] == jnp.arange(N_EXPERTS, dtype=jnp.int32)[:, None]).astype(jnp.int32)
    counts = jnp.sum(onehot, axis=1)
    rank = jnp.sum((jnp.cumsum(onehot, axis=1) - onehot) * onehot, axis=0)
    padded = (counts + tm_e - 1) // tm_e * tm_e
    ends = jnp.cumsum(padded)
    pos = jnp.sum(onehot * (ends - padded)[:, None], axis=0) + rank
    rows = TOP_K * T + N_EXPERTS * tm_e
    token = jnp.arange(TOP_K * T, dtype=jnp.int32) % T
    src = (jnp.arange(rows, dtype=jnp.int32) % T).at[pos].set(
        token, unique_indices=True, mode="promise_in_bounds")
    tile_start = jnp.arange(rows // tm_e, dtype=jnp.int32) * tm_e
    tile_expert = jnp.minimum(jnp.sum((tile_start[:, None] >= ends[None, :]).astype(jnp.int32), axis=1),
                              N_EXPERTS - 1)
    x_sorted = h_bf16.at[src].get(mode="promise_in_bounds")
    tiles_used = (ends[N_EXPERTS - 1:] // tm_e).astype(jnp.int32)
    y = _ffn_experts(tile_expert, tiles_used, x_sorted, w13, w2, tm_e, tf)
    y0 = y.at[pos[:T]].get(mode="promise_in_bounds")
    y1 = y.at[pos[T:]].get(mode="promise_in_bounds")
    return _moe_out(h, y0, y1, wt, lng, lnb, tm_router)


def _rot_cols(w, block):
    d, n = w.shape
    w4 = w.reshape(d, n // block, 2, block // 2)
    return jnp.concatenate([-w4[:, :, 1], w4[:, :, 0]], axis=2).reshape(d, n)


def _attention_weights(w_in):
    o = np.cumsum([0, MLA_Q_LORA, MLA_KV_LORA, MLA_ROPE, _DIL_COLS, GQA_Q_HEADS * HEAD_DIM,
                   2 * GQA_KV_HEADS * HEAD_DIM, _DIFF_COLS, N_BRANCH * D_MODEL])
    cq, ckv, kr, dil, gq, gkv, diff, gate = (w_in[:, o[i]:o[i + 1]] for i in range(8))
    z = lambda n: jnp.zeros((D_MODEL, n), w_in.dtype)
    mla = [cq, z(64), ckv, z(64), kr, z(32), z(64), _rot_cols(kr, MLA_ROPE), z(32)]
    gqa = [gq, _rot_cols(gq, 32), gkv[:, :128], _rot_cols(gkv[:, :128], 32), gkv[:, 128:]]
    small = jnp.concatenate(mla + gqa, axis=1)
    return [w.astype(BF16) for w in (small, diff, dil)], gate.astype(BF16)


def _mla_up_weights(w_uq, w_ukv):
    zq = lambda n: jnp.zeros((MLA_Q_LORA, n), w_uq.dtype)
    per = MLA_NOPE + MLA_ROPE
    plain, rot = [], []
    for h in range(MLA_HEADS):
        nope = w_uq[:, h * per:h * per + MLA_NOPE]
        rope = w_uq[:, h * per + MLA_NOPE:(h + 1) * per]
        plain += [nope, rope, zq(32)]
        rot += [zq(64), _rot_cols(rope, MLA_ROPE), zq(32)]
    wuq = jnp.concatenate(plain + rot, axis=1)
    wuq = jnp.pad(wuq, ((0, _Q_LORA_PAD - MLA_Q_LORA), (0, 0)))
    zk = jnp.zeros((MLA_KV_LORA, 64), w_ukv.dtype)
    per = MLA_NOPE + MLA_V
    ks, vs = [], []
    for h in range(MLA_HEADS):
        ks += [w_ukv[:, h * per:h * per + MLA_NOPE], zk]
        vs += [w_ukv[:, h * per + MLA_NOPE:(h + 1) * per], zk]
    wukv = jnp.concatenate(ks + vs, axis=1)
    return wuq.astype(BF16), wukv.astype(BF16)


def _rope_angles(pos, dim):
    inv = ROPE_THETA ** (-(jnp.arange(0, dim, 2, dtype=F32) / dim))
    return pos[:, None] * inv[None, :]


def _position_tables(seq):
    rows = seq // GRID_W
    row_idx = jnp.repeat(jnp.arange(rows, dtype=F32), GRID_W)
    col_idx = jnp.tile(jnp.arange(GRID_W, dtype=F32), rows)
    a1 = _rope_angles(jnp.arange(seq, dtype=F32), MLA_ROPE)
    ar = _rope_angles(row_idx, HEAD_DIM // 2)
    ac = _rope_angles(col_idx, HEAD_DIM // 2)
    z = lambda n: jnp.zeros((seq, n), F32)
    c1 = jnp.concatenate([jnp.ones((seq, MLA_NOPE), F32)] + [jnp.cos(a1)] * 2 + [z(32)], axis=1)
    s1 = jnp.concatenate([z(MLA_NOPE)] + [jnp.sin(a1)] * 2 + [z(32)], axis=1)
    ca = jnp.concatenate(([jnp.cos(ar)] * 2 + [jnp.cos(ac)] * 2) * 2, axis=1)
    sa = jnp.concatenate(([jnp.sin(ar)] * 2 + [jnp.sin(ac)] * 2) * 2, axis=1)
    return [c1, s1, ca, sa]


def _gqa_gain_rows(q_norm, k_norm):
    def swap(g):
        return jnp.concatenate([g[16:32], g[0:16], g[48:64], g[32:48]])

    sg = HEAD_DIM ** -0.5 * LOG2E
    rows = [sg * q_norm, sg * swap(q_norm), k_norm, swap(k_norm)]
    return jnp.stack([jnp.concatenate([r, r]) for r in rows])


def _pick(n, pref):
    t = min(n, pref)
    assert n % t == 0, (n, pref)
    return t


class _Tiles(NamedTuple):
    rows: int
    flash: int
    diff: int
    ffn_rows: int
    ffn_cols: int


def _plan_tiles(seq, tokens):
    return _Tiles(rows=_pick(seq, 512), flash=_pick(seq, 1024), diff=_pick(seq, 512),
                  ffn_rows=_pick(tokens, 1024), ffn_cols=D_FF // 4)


def kernel(x, ln_emb_g, ln_emb_b, w_in, b_gate, mla_q_norm, mla_kv_norm, mla_w_uq, mla_w_ukv,
           gqa_q_norm, gqa_k_norm, diff_lambda, diff_subln, w_branch, w_out, ln1_g, ln1_b,
           ffn_w13, ffn_w2, moe_router, moe_w13, moe_w2, ln2_g, ln2_b):
    batch, seq, _ = x.shape
    T = batch * seq
    tiles = _plan_tiles(seq, T)
    row = lambda v: v.reshape(1, -1)
    pos_tabs = _position_tables(seq)

    h = x.reshape(T, D_MODEL)
    for l in range(DEPTH):
        lam_init = 0.8 - 0.6 * float(np.exp(-0.3 * l))
        weights, wg = _attention_weights(w_in[l])
        wuq, wukv = _mla_up_weights(mla_w_uq[l], mla_w_ukv[l])
        gains = _gqa_gain_rows(gqa_q_norm[l], gqa_k_norm[l])
        nq = jnp.pad(mla_q_norm[l], (0, _Q_LORA_PAD - MLA_Q_LORA)).reshape(1, _Q_LORA_PAD)
        res = _project(h, batch, seq, l == 0, row(ln_emb_g), row(ln_emb_b), weights, wuq, wukv,
                       nq, row(mla_kv_norm[l]), gains, pos_tabs, tiles.rows)
        if l == 0:
            h, res = res[0], res[1:]
        mq, mk, mv, gq, gk, gv, dq, dk, dv = res[:9]
        dil = res[9:]
        ya = _flash(mq, mk, mv, tiles.flash, tiles.flash).reshape(T, BRANCH_W)
        yc = _flash(gq, gk, gv, tiles.flash, tiles.flash).reshape(T, BRANCH_W)
        sub = jnp.pad(diff_subln[l], (0, LANES - DIFF_V)).reshape(1, LANES)
        yd = _diff_attention(dq, dk, dv, diff_lambda[l], sub, lam_init, tiles.diff).reshape(T, BRANCH_W)
        dil_o, dil_lse = [], []
        for g in range(len(DIL_GROUPS)):
            o, lse = _dilated_group(dil[g], dil[3 + g], dil[6 + g], g)
            dil_o.append(o)
            dil_lse.append(lse)
        dense = l % 2 == 0
        mixed = _mix(h, ya, yc, yd, dil_o, dil_lse, wg, b_gate[l], w_branch[l].astype(BF16),
                     w_out[l].astype(BF16), row(ln1_g[l]), row(ln1_b[l]), tiles.rows, not dense)
        if dense:
            h = _ffn_dense(mixed[0], ffn_w13[l // 2], ffn_w2[l // 2],
                           row(ln2_g[l]), row(ln2_b[l]), tiles.ffn_rows, tiles.ffn_cols)
        else:
            h = _moe(mixed[0], mixed[1], moe_router[l // 2], moe_w13[l // 2], moe_w2[l // 2],
                     row(ln2_g[l]), row(ln2_b[l]), tiles.rows, tiles.ffn_rows, tiles.ffn_cols)
    return h.reshape(batch, seq, D_MODEL)
```

```python
import functools
from typing import NamedTuple

import jax
import jax.numpy as jnp
import numpy as np
from jax import lax
from jax.experimental import pallas as pl
from jax.experimental.pallas import tpu as pltpu

D_MODEL = 1024
GRID_W = 64
HEAD_DIM = 64
N_BRANCH = 4
BRANCH_W = 256
MLA_HEADS = 4
MLA_NOPE = 64
MLA_ROPE = 32
MLA_V = 64
MLA_Q_LORA = 192
MLA_KV_LORA = 128
DIL_GROUPS = ((128, 1), (512, 4), (2048, 16))
DIL_HEADS = 4
GQA_Q_HEADS = 4
GQA_KV_HEADS = 2
DIFF_HEADS = 4
DIFF_D = 32
DIFF_V = 64
D_FF = 3584
N_EXPERTS = 8
TOP_K = 2
ROPE_THETA = 10000.0
LN_EPS = 1e-5
RMS_EPS = 1e-6
DIFF_NORM_EPS = 1e-5
DEPTH = 2
DEEPNORM_ALPHA = (2 * DEPTH) ** 0.25

LANES = 128
HALF = LANES // 2
MXU_COLS = 256
VMEM_BYTES = 64 * 1024 * 1024
VMEM_LIMIT = VMEM_BYTES - 8 * 1024 * 1024

LOG2E = 1.4426950408889634
NEG = -1e30
BF16 = jnp.bfloat16
F32 = jnp.float32

_Q_LORA_PAD = 2 * LANES
_MLA_COLS = _Q_LORA_PAD + MLA_KV_LORA + 2 * LANES
_GQA_COLS = 7 * LANES
_DIFF_COLS = 3 * DIFF_HEADS * DIFF_V
_DIL_COLS = 3 * len(DIL_GROUPS) * BRANCH_W
_SMALL_COLS = _MLA_COLS + _GQA_COLS


def _cparams(sem):
    return pltpu.CompilerParams(dimension_semantics=sem, vmem_limit_bytes=VMEM_LIMIT)


def _resident(shape, index_map):
    return pl.BlockSpec(shape, index_map, pipeline_mode=pl.Buffered(1))


def _layernorm(x, g, b):
    mu = jnp.mean(x, axis=-1, keepdims=True)
    xc = x - mu
    var = jnp.mean(xc * xc, axis=-1, keepdims=True)
    return xc * lax.rsqrt(var + LN_EPS) * g + b


def _lane(shape):
    return lax.broadcasted_iota(jnp.int32, shape, len(shape) - 1)


def _split_pair(x):
    lo_mask = _lane(x.shape) < HALF
    lo = jnp.where(lo_mask, x, 0.0)
    hi = jnp.where(lo_mask, pltpu.roll(x, HALF, axis=1), 0.0)
    return lo, hi


def _join_pair(lo, hi):
    return jnp.where(_lane(lo.shape) < HALF, lo, pltpu.roll(hi, HALF, axis=1))


def _with_ones(v):
    return jnp.where(_lane(v.shape) == HALF, 1.0, v)


def _diff_slope(h):
    return 2.0 ** (-8.0 * (h + 1) / DIFF_HEADS) * LOG2E


def _split3(val):
    f1 = val.astype(BF16).astype(F32)
    r1 = val - f1
    f2 = r1.astype(BF16).astype(F32)
    f3 = (r1 - f2).astype(BF16).astype(F32)
    return f1, f2, f3


_FEAT = 2 * DIFF_D


def _key_bias_features(val):
    f1, f2, f3 = _split3(val)
    lane = _lane(val.shape)
    ones = jnp.where((lane >= _FEAT + 3) & (lane < _FEAT + 6), 1.0, 0.0)
    return jnp.where(lane == _FEAT, f1, jnp.where(lane == _FEAT + 1, f2, jnp.where(lane == _FEAT + 2, f3, ones)))


def _proj_kernel(apply_ln, tiles_per_seq, *refs):
    (x_ref, lng_ref, lnb_ref, ws_ref, wdiff_ref, wdil_ref, wuq_ref, wukv_ref, nq_ref, nkv_ref,
     gains_ref, c1_ref, s1_ref, ca_ref, sa_ref) = refs[:15]
    outs = refs[15:]
    if apply_ln:
        h_ref, outs = outs[0], outs[1:]
    (mq_ref, mk_ref, mv_ref, gq_ref, gk_ref, gv_ref, dq_ref, dk_ref, dv_ref) = outs[:9]
    dil_refs = outs[9:]

    x = x_ref[...]
    if apply_ln:
        x = _layernorm(x, lng_ref[...], lnb_ref[...])
        h_ref[...] = x
    hb = x.astype(BF16)

    def proj(w_ref, off, n):
        return jnp.dot(hb, w_ref[:, off:off + n], preferred_element_type=F32)

    p_small = proj(ws_ref, 0, _SMALL_COLS)
    p = p_small[:, :_MLA_COLS]
    cq, ckv, kr, krr = p[:, 0:256], p[:, 256:384], p[:, 384:512], p[:, 512:640]
    rq = lax.rsqrt(jnp.sum(cq * cq, axis=-1, keepdims=True) * (1.0 / MLA_Q_LORA) + RMS_EPS)
    cqn = (cq * rq * nq_ref[...]).astype(BF16)
    q2 = jnp.dot(cqn, wuq_ref[...], preferred_element_type=F32)
    rkv = lax.rsqrt(jnp.sum(ckv * ckv, axis=-1, keepdims=True) * (1.0 / MLA_KV_LORA) + RMS_EPS)
    ckvn = (ckv * rkv * nkv_ref[...]).astype(BF16)
    kv2 = jnp.dot(ckvn, wukv_ref[...], preferred_element_type=F32)
    c1, s1 = c1_ref[...], s1_ref[...]
    k_rope = kr * c1 + krr * s1
    mla_scale = (MLA_NOPE + MLA_ROPE) ** -0.5 * LOG2E
    cq_t, sq_t = c1 * mla_scale, s1 * mla_scale
    for h in range(MLA_HEADS):
        qh = q2[:, h * LANES:(h + 1) * LANES]
        qrh = q2[:, (MLA_HEADS + h) * LANES:(MLA_HEADS + h + 1) * LANES]
        mq_ref[h] = (qh * cq_t + qrh * sq_t).astype(BF16)
        mk_ref[h] = (kv2[:, h * LANES:(h + 1) * LANES] + k_rope).astype(BF16)
        vh = kv2[:, (MLA_HEADS + h) * LANES:(MLA_HEADS + h + 1) * LANES]
        mv_ref[h] = _with_ones(vh).astype(BF16)

    p = p_small[:, _MLA_COLS:]

    def head_rms(xs):
        ss = xs * xs
        lo_mask = _lane(xs.shape) < HALF
        s_all = jnp.sum(ss, axis=-1, keepdims=True)
        s_lo = jnp.sum(jnp.where(lo_mask, ss, 0.0), axis=-1, keepdims=True)
        return lax.rsqrt(jnp.where(lo_mask, s_lo, s_all - s_lo) * (1.0 / HEAD_DIM) + RMS_EPS)

    ca, sa = ca_ref[...], sa_ref[...]
    aq, bq = ca * gains_ref[0:1, :], sa * gains_ref[1:2, :]
    ak, bk = ca * gains_ref[2:3, :], sa * gains_ref[3:4, :]
    for s in range(2):
        qs = p[:, s * LANES:(s + 1) * LANES]
        qrs = p[:, 256 + s * LANES:256 + (s + 1) * LANES]
        qn = head_rms(qs) * (qs * aq + qrs * bq)
        lo, hi = _split_pair(qn)
        gq_ref[2 * s] = lo.astype(BF16)
        gq_ref[2 * s + 1] = hi.astype(BF16)
    ks, krs = p[:, 512:640], p[:, 640:768]
    kn = head_rms(ks) * (ks * ak + krs * bk)
    lo, hi = _split_pair(kn)
    gk_ref[0] = lo.astype(BF16)
    gk_ref[1] = hi.astype(BF16)
    lo, hi = _split_pair(p[:, 768:896])
    gv_ref[0] = _with_ones(lo).astype(BF16)
    gv_ref[1] = _with_ones(hi).astype(BF16)

    p = proj(wdiff_ref, 0, _DIFF_COLS)
    dscale = DIFF_D ** -0.5 * LOG2E
    tm = x.shape[0]
    pos = (pl.program_id(0) % tiles_per_seq * tm
           + lax.broadcasted_iota(jnp.int32, (tm, LANES), 0)).astype(F32)
    for s in range(2):
        lo, hi = _split_pair(p[:, s * LANES:(s + 1) * LANES] * dscale)
        dq_ref[2 * s] = lo.astype(BF16)
        dq_ref[2 * s + 1] = hi.astype(BF16)
        lo, hi = _split_pair(p[:, 256 + s * LANES:256 + (s + 1) * LANES])
        dk_ref[2 * s] = (lo + _key_bias_features(_diff_slope(2 * s) * pos)).astype(BF16)
        dk_ref[2 * s + 1] = (hi + _key_bias_features(_diff_slope(2 * s + 1) * pos)).astype(BF16)
        lo, hi = _split_pair(p[:, 512 + s * LANES:512 + (s + 1) * LANES])
        dv_ref[2 * s] = _with_ones(lo).astype(BF16)
        dv_ref[2 * s + 1] = _with_ones(hi).astype(BF16)

    lscale = HEAD_DIM ** -0.5 * LOG2E
    for i in range(9):
        seg = proj(wdil_ref, i * BRANCH_W, BRANCH_W)
        if i < 3:
            seg = seg * lscale
        dil_refs[i][0] = seg[:, :LANES]
        dil_refs[i][1] = seg[:, LANES:]


def _project(x2d, batch, seq, apply_ln, lng, lnb, weights, wuq, wukv, nq, nkv, gains, tables, tm):
    T = x2d.shape[0]
    nst = seq // tm
    grid = (T // tm,)
    row = lambda i: (i, 0)
    const = lambda i: (0, 0)
    tab = lambda i: (i % nst, 0)
    head = lambda i: (i // nst, 0, i % nst, 0)

    in_specs = [
        pl.BlockSpec((tm, D_MODEL), row),
        pl.BlockSpec((1, D_MODEL), const),
        pl.BlockSpec((1, D_MODEL), const),
        _resident((D_MODEL, _SMALL_COLS), const),
        _resident((D_MODEL, _DIFF_COLS), const),
        _resident((D_MODEL, _DIL_COLS), const),
        _resident((_Q_LORA_PAD, 8 * LANES), const),
        _resident((LANES, 8 * LANES), const),
        pl.BlockSpec((1, _Q_LORA_PAD), const),
        pl.BlockSpec((1, LANES), const),
        pl.BlockSpec((4, LANES), const),
    ] + [pl.BlockSpec((tm, LANES), tab)] * 4

    def heads_out(n):
        return (jax.ShapeDtypeStruct((batch, n, seq, LANES), BF16),
                pl.BlockSpec((None, n, tm, LANES), head))

    outs = []
    if apply_ln:
        outs.append((jax.ShapeDtypeStruct((T, D_MODEL), F32), pl.BlockSpec((tm, D_MODEL), row)))
    outs += [heads_out(MLA_HEADS)] * 3
    outs += [heads_out(GQA_Q_HEADS), heads_out(GQA_KV_HEADS), heads_out(GQA_KV_HEADS)]
    outs += [heads_out(DIFF_HEADS)] * 3
    outs += [(jax.ShapeDtypeStruct((batch, 2, seq, LANES), F32), pl.BlockSpec((None, 2, tm, LANES), head))] * 9

    res = pl.pallas_call(
        functools.partial(_proj_kernel, apply_ln, nst),
        grid=grid,
        in_specs=in_specs,
        out_specs=[o[1] for o in outs],
        out_shape=[o[0] for o in outs],
        compiler_params=_cparams(("parallel",)),
        name="proj_prep",
    )(x2d, lng, lnb, *weights, wuq, wukv, nq, nkv, gains, *tables)
    return res


def _normalize(acc):
    return acc / acc[:, HALF:HALF + 1]


def _scores(q, k):
    return lax.dot_general(q, k, (((1,), (1,)), ((), ())), preferred_element_type=F32)


def _softmax_update(s, v, acc_ref, m_ref, idx):
    blocks = [s[:, j * LANES:(j + 1) * LANES] for j in range(s.shape[1] // LANES)]
    part = functools.reduce(jnp.maximum, blocks)
    m_prev = m_ref[idx]
    m_new = jnp.maximum(m_prev, jnp.max(part, axis=-1, keepdims=True))
    p = jnp.concatenate([jnp.exp2((b - m_new).astype(BF16)) for b in blocks], axis=1)
    acc_ref[idx] = (jnp.exp2(m_prev - m_new) * acc_ref[idx]
                    + jnp.dot(p, v, preferred_element_type=F32))
    m_ref[idx] = m_new


def _flash_kernel(n_rep, tk, q_ref, k_ref, v_ref, o_ref, acc_ref, m_ref):
    n_heads = q_ref.shape[0]
    seq = k_ref.shape[1]
    acc_ref[...] = jnp.zeros_like(acc_ref)
    m_ref[...] = jnp.full_like(m_ref, NEG)

    def body(c, carry):
        ks = pl.multiple_of(c * tk, tk)
        scores = [_scores(q_ref[h], k_ref[h // n_rep, pl.ds(ks, tk), :]) for h in range(n_heads)]
        for h in range(n_heads):
            _softmax_update(scores[h], v_ref[h // n_rep, pl.ds(ks, tk), :], acc_ref, m_ref, h)
        return carry

    lax.fori_loop(0, seq // tk, body, 0)
    outs = [_normalize(acc_ref[h]) for h in range(n_heads)]
    for s in range(n_heads // 2):
        o_ref[:, s * LANES:(s + 1) * LANES] = _join_pair(outs[2 * s], outs[2 * s + 1]).astype(o_ref.dtype)


def _flash(q, k, v, tq, tk):
    batch, n_heads, seq, _ = q.shape
    n_kv = k.shape[1]
    return pl.pallas_call(
        functools.partial(_flash_kernel, n_heads // n_kv, tk),
        grid=(batch, seq // tq),
        in_specs=[
            pl.BlockSpec((None, n_heads, tq, LANES), lambda b, i: (b, 0, i, 0)),
            _resident((None, n_kv, seq, LANES), lambda b, i: (b, 0, 0, 0)),
            _resident((None, n_kv, seq, LANES), lambda b, i: (b, 0, 0, 0)),
        ],
        out_specs=pl.BlockSpec((None, tq, n_heads * HALF), lambda b, i: (b, i, 0)),
        out_shape=jax.ShapeDtypeStruct((batch, seq, n_heads * HALF), BF16),
        scratch_shapes=[pltpu.VMEM((n_heads, tq, LANES), F32), pltpu.VMEM((n_heads, tq, LANES), F32)],
        compiler_params=_cparams(("parallel", "parallel")),
        name="flash_attn",
    )(q, k, v)


def _diff_kernel(lam_init, q_ref, k_ref, v_ref, lp_ref, sub_ref, o_ref, acc_ref, m_ref, qv_ref):
    n_heads, tq, _ = q_ref.shape
    seq = k_ref.shape[1]
    tile = pl.program_id(1)
    t0 = tile * tq
    lp = lp_ref[...]
    lam = (jnp.exp(jnp.sum(lp[0:1] * lp[1:2], keepdims=True))
           - jnp.exp(jnp.sum(lp[2:3] * lp[3:4], keepdims=True)) + lam_init)
    lane = _lane((tq, LANES))
    tpos = (t0 + lax.broadcasted_iota(jnp.int32, (tq, LANES), 0)).astype(F32)
    in_f = (lane >= _FEAT) & (lane < _FEAT + 3)
    for h in range(n_heads):
        q = q_ref[h].astype(F32)
        g1, g2, g3 = _split3(2.0 * _diff_slope(h) * tpos)
        left = jnp.where(in_f, 1.0, 0.0)
        right = jnp.where(in_f, -1.0, jnp.where(lane == _FEAT + 3, g1, jnp.where(
            lane == _FEAT + 4, g2, jnp.where(lane == _FEAT + 5, g3, 0.0))))
        for m in range(2):
            qm = jnp.where((lane >= m * DIFF_D) & (lane < (m + 1) * DIFF_D), q, 0.0)
            qv_ref[0, 2 * h + m] = (qm + left).astype(BF16)
            qv_ref[1, 2 * h + m] = (qm + right).astype(BF16)
    acc_ref[...] = jnp.zeros_like(acc_ref)
    m_ref[...] = jnp.full_like(m_ref, NEG)
    chains = range(2 * n_heads)

    def update(scores, ks):
        for i in chains:
            _softmax_update(scores[i], v_ref[i // 2, pl.ds(ks, tq), :], acc_ref, m_ref, i)

    ks = pl.multiple_of(t0, tq)
    update([jnp.minimum(_scores(qv_ref[0, i], k_ref[i // 2, pl.ds(ks, tq), :]),
                        _scores(qv_ref[1, i], k_ref[i // 2, pl.ds(ks, tq), :])) for i in chains], ks)

    def body(c, carry):
        side = (c >= tile).astype(jnp.int32)
        ks = pl.multiple_of((c + side) * tq, tq)
        update([_scores(qv_ref[side, i], k_ref[i // 2, pl.ds(ks, tq), :]) for i in chains], ks)
        return carry

    n_other = seq // tq - 1
    if n_other % 2:
        body(n_other - 1, 0)
    lax.fori_loop(0, n_other // 2 * 2, body, 0, unroll=2)

    outs = []
    for h in range(n_heads):
        o = _normalize(acc_ref[2 * h]) - lam * _normalize(acc_ref[2 * h + 1])
        o = jnp.where(lane < DIFF_V, o, 0.0)
        ms = jnp.sum(o * o, axis=-1, keepdims=True) * (1.0 / DIFF_V)
        outs.append(o * lax.rsqrt(ms + DIFF_NORM_EPS) * sub_ref[...] * (1.0 - lam_init))
    for s in range(n_heads // 2):
        o_ref[:, s * LANES:(s + 1) * LANES] = _join_pair(outs[2 * s], outs[2 * s + 1]).astype(o_ref.dtype)


def _diff_attention(q, k, v, lam_params, subln, lam_init, tq):
    batch, n_heads, seq, _ = q.shape
    return pl.pallas_call(
        functools.partial(_diff_kernel, lam_init),
        grid=(batch, seq // tq),
        in_specs=[
            pl.BlockSpec((None, n_heads, tq, LANES), lambda b, i: (b, 0, i, 0)),
            _resident((None, n_heads, seq, LANES), lambda b, i: (b, 0, 0, 0)),
            _resident((None, n_heads, seq, LANES), lambda b, i: (b, 0, 0, 0)),
            pl.BlockSpec((4, DIFF_D), lambda b, i: (0, 0)),
            pl.BlockSpec((1, LANES), lambda b, i: (0, 0)),
        ],
        out_specs=pl.BlockSpec((None, tq, n_heads * HALF), lambda b, i: (b, i, 0)),
        out_shape=jax.ShapeDtypeStruct((batch, seq, n_heads * HALF), BF16),
        scratch_shapes=[pltpu.VMEM((2 * n_heads, tq, LANES), F32)] * 2
        + [pltpu.VMEM((2, 2 * n_heads, tq, LANES), BF16)],
        compiler_params=_cparams(("parallel", "parallel")),
        name="diff_attn",
    )(q, k, v, lam_params, subln)


_DIL_BLOCK = 2048
_DIL_Q = 128


def _dil_kernel(group, dil, seq, q_ref, kp_ref, kc_ref, kn_ref, vp_ref, vc_ref, vn_ref,
                o_ref, lse_ref, kw_ref, vw_ref):
    tb = q_ref.shape[1]
    halo = DIL_GROUPS[group][0] // 2
    assert halo == kp_ref.shape[1] and halo == (_DIL_Q // 2) * dil
    for w_ref, p_ref, c_ref, n_ref in ((kw_ref, kp_ref, kc_ref, kn_ref), (vw_ref, vp_ref, vc_ref, vn_ref)):
        w_ref[:, 0:halo] = p_ref[...]
        w_ref[:, halo:halo + tb] = c_ref[...]
        w_ref[:, halo + tb:halo + tb + halo] = n_ref[...]
    t0 = pl.program_id(1) * tb
    shape = (_DIL_Q, 2 * _DIL_Q)
    rel = lax.broadcasted_iota(jnp.int32, shape, 1) - lax.broadcasted_iota(jnp.int32, shape, 0) - _DIL_Q // 2
    absrel = jnp.abs(rel).astype(F32)
    band = jnp.where(absrel <= _DIL_Q // 2, 0.0, -NEG)
    n_slopes = len(DIL_GROUPS) * DIL_HEADS
    bias = [2.0 ** (-8.0 * (group * DIL_HEADS + h + 1) / n_slopes) * dil * LOG2E * absrel + band
            for h in range(DIL_HEADS)]
    col = lax.broadcasted_iota(jnp.int32, (1, 2 * _DIL_Q), 1)
    lo_mask = _lane((_DIL_Q, LANES)) < HALF
    for unit in range(tb // _DIL_Q):
        first = (unit // dil) * _DIL_Q * dil + unit % dil
        rows = pl.ds(first, _DIL_Q, stride=dil)
        win = pl.ds(first, 2 * _DIL_Q, stride=dil)
        key_token = t0 + (first - halo) + dil * col
        off_seq = jnp.where(key_token >= 0, jnp.where(key_token < seq, 0.0, NEG), NEG)
        for s in range(2):
            qs = q_ref[s, rows, :].astype(BF16)
            ksl = kw_ref[s, win, :].astype(BF16)
            vsl = vw_ref[s, win, :].astype(BF16)
            res = []
            for part in range(2):
                keep = lo_mask if part == 0 else jnp.logical_not(lo_mask)
                sc = _scores(jnp.where(keep, qs, jnp.zeros_like(qs)), ksl) - bias[2 * s + part] + off_seq
                m = jnp.max(sc, axis=-1, keepdims=True)
                p = jnp.exp2(sc - m)
                l = jnp.sum(p, axis=-1, keepdims=True)
                o = jnp.dot(p.astype(BF16), vsl, preferred_element_type=F32) / l
                res.append((o, m + jnp.log2(l)))
            o_ref[s, rows, :] = jnp.where(lo_mask, res[0][0], res[1][0])
            lse_ref[s, rows, :] = jnp.where(lo_mask, res[0][1], res[1][1])


def _dilated_group(q, k, v, group):
    batch, _, seq, _ = q.shape
    dil = DIL_GROUPS[group][1]
    halo = DIL_GROUPS[group][0] // 2
    tb = _DIL_BLOCK
    assert seq % tb == 0 and seq // dil >= 2 * _DIL_Q and tb % (_DIL_Q * dil) == 0
    per = tb // halo
    cur = pl.BlockSpec((None, 2, tb, LANES), lambda b, i: (b, 0, i, 0))
    prev = pl.BlockSpec((None, 2, halo, LANES), lambda b, i: (b, 0, jnp.maximum(i * per - 1, 0), 0))
    nxt = pl.BlockSpec((None, 2, halo, LANES),
                       lambda b, i: (b, 0, jnp.minimum((i + 1) * per, seq // halo - 1), 0))
    out_shape = jax.ShapeDtypeStruct((batch, 2, seq, LANES), F32)
    o, lse = pl.pallas_call(
        functools.partial(_dil_kernel, group, dil, seq),
        grid=(batch, seq // tb),
        in_specs=[cur, prev, cur, nxt, prev, cur, nxt],
        out_specs=[cur, cur],
        out_shape=[out_shape, out_shape],
        scratch_shapes=[pltpu.VMEM((2, tb + 2 * halo, LANES), F32)] * 2,
        compiler_params=_cparams(("parallel", "parallel")),
        name=f"dilated_g{group}",
    )(q, k, k, k, v, v, v)
    return o, lse


def _mix_kernel(h_ref, ya_ref, yc_ref, yd_ref, o0_ref, o1_ref, o2_ref, l0_ref, l1_ref, l2_ref,
                wg_ref, bg_ref, wb_ref, wo_ref, lng_ref, lnb_ref, *rest):
    wr_ref, out_ref, route_refs = (rest[0], rest[1], rest[2:]) if len(rest) > 1 else (None, rest[0], ())
    tm = h_ref.shape[0]
    for rows in (pl.ds(0, tm // 2), pl.ds(tm // 2, tm // 2)):
        h = h_ref[rows, :]
        hb = h.astype(BF16)
        slabs = []
        for s in range(2):
            l0, l1, l2 = l0_ref[s, rows, :], l1_ref[s, rows, :], l2_ref[s, rows, :]
            mx = jnp.maximum(jnp.maximum(l0, l1), l2)
            e0, e1, e2 = jnp.exp2(l0 - mx), jnp.exp2(l1 - mx), jnp.exp2(l2 - mx)
            slabs.append(((e0 * o0_ref[s, rows, :] + e1 * o1_ref[s, rows, :] + e2 * o2_ref[s, rows, :])
                          / (e0 + e1 + e2)).astype(BF16))
        ys = (ya_ref[rows, :], jnp.concatenate(slabs, axis=1), yc_ref[rows, :], yd_ref[rows, :])
        acc = None
        for n in range(N_BRANCH):
            logit = jnp.dot(hb, wg_ref[:, n * D_MODEL:(n + 1) * D_MODEL],
                            preferred_element_type=F32) + bg_ref[n:n + 1, :]
            gate = 1.0 / (1.0 + jnp.exp(-logit))
            term = gate * jnp.dot(ys[n], wb_ref[n], preferred_element_type=F32)
            acc = term if acc is None else acc + term
        m = jnp.dot(acc.astype(BF16), wo_ref[...], preferred_element_type=F32)
        out = _layernorm(DEEPNORM_ALPHA * h + m, lng_ref[...], lnb_ref[...])
        out_ref[rows, :] = out
        if route_refs:
            bf16_ref, idx_ref, wt_ref = route_refs
            bf16_ref[rows, :] = out.astype(BF16)
            idx_ref[rows, :], wt_ref[rows, :] = _route(out, wr_ref[...])


def _mix(h, ya, yc, yd, dil_o, dil_lse, wg, bg, wb, wo, lng, lnb, tm, w_router):
    T = h.shape[0]
    route = w_router is not None
    extra_in, extra_spec, extra_out, extra_shape = [], [], [], []
    if route:
        lanes = pl.BlockSpec((tm, LANES), lambda i: (i, 0))
        extra_in = [jnp.pad(w_router, ((0, 0), (0, LANES - N_EXPERTS)))]
        extra_spec = [pl.BlockSpec((D_MODEL, LANES), lambda i: (0, 0))]
        extra_out = [pl.BlockSpec((tm, D_MODEL), lambda i: (i, 0)), lanes, lanes]
        extra_shape = [jax.ShapeDtypeStruct((T, D_MODEL), BF16), jax.ShapeDtypeStruct((T, LANES), jnp.int32),
                       jax.ShapeDtypeStruct((T, LANES), F32)]
    nst = dil_o[0].shape[2] // tm
    slab = pl.BlockSpec((None, 2, tm, LANES), lambda i: (i // nst, 0, i % nst, 0))
    row = lambda i: (i, 0)
    const = lambda i: (0, 0)
    wide = pl.BlockSpec((tm, D_MODEL), row)
    narrow = pl.BlockSpec((tm, BRANCH_W), row)
    return pl.pallas_call(
        _mix_kernel,
        grid=(T // tm,),
        in_specs=[wide] + [narrow] * 3 + [slab] * 6 + [
            _resident((D_MODEL, N_BRANCH * D_MODEL), const),
            pl.BlockSpec((N_BRANCH, D_MODEL), const),
            _resident((N_BRANCH, BRANCH_W, D_MODEL), lambda i: (0, 0, 0)),
            _resident((D_MODEL, D_MODEL), const),
            pl.BlockSpec((1, D_MODEL), const),
            pl.BlockSpec((1, D_MODEL), const),
        ] + extra_spec,
        out_specs=[wide] + extra_out,
        out_shape=[jax.ShapeDtypeStruct((T, D_MODEL), F32)] + extra_shape,
        compiler_params=_cparams(("parallel",)),
        name="gate_mix",
    )(h, ya, yc, yd, *dil_o, *dil_lse, wg, bg, wb, wo, lng, lnb, *extra_in)


def _swiglu_step(xb_ref, w1_ref, w3_ref, w2_ref, acc_ref):
    @pl.when(pl.program_id(1) == 0)
    def _():
        acc_ref[...] = jnp.zeros_like(acc_ref)

    xb = xb_ref[...]
    tf = w1_ref.shape[1]
    full = tf // MXU_COLS * MXU_COLS
    mid = (full // MXU_COLS + 1) // 2 * MXU_COLS
    gated = []

    def gate(a, b):
        gated.append((a / (1.0 + jnp.exp(-a)) * b).astype(BF16))

    for lo, hi in ((0, mid), (mid, full)):
        gate(jnp.dot(xb, w1_ref[:, lo:hi].astype(BF16), preferred_element_type=F32),
             jnp.dot(xb, w3_ref[:, lo:hi].astype(BF16), preferred_element_type=F32))
    if full < tf:
        w13 = jnp.concatenate([w1_ref[:, full:tf].astype(BF16), w3_ref[:, full:tf].astype(BF16)], axis=1)
        ab = jnp.dot(xb, w13, preferred_element_type=F32)
        gate(ab[:, :tf - full], ab[:, tf - full:])
    acc_ref[...] += jnp.dot(jnp.concatenate(gated, axis=1), w2_ref[...].astype(BF16),
                            preferred_element_type=F32)


def _ffn_dense_kernel(x_ref, w1_ref, w3_ref, w2_ref, lng_ref, lnb_ref, o_ref, acc_ref, xb_ref):
    @pl.when(pl.program_id(1) == 0)
    def _():
        xb_ref[...] = x_ref[...].astype(BF16)

    _swiglu_step(xb_ref, w1_ref, w3_ref, w2_ref, acc_ref)

    @pl.when(pl.program_id(1) == pl.num_programs(1) - 1)
    def _():
        o_ref[...] = _layernorm(DEEPNORM_ALPHA * x_ref[...] + acc_ref[...], lng_ref[...], lnb_ref[...])


def _ffn_dense(x, w13, w2, lng, lnb, tm, tf):
    T = x.shape[0]
    nf = D_FF // tf
    return pl.pallas_call(
        _ffn_dense_kernel,
        grid=(T // tm, nf),
        in_specs=[
            pl.BlockSpec((tm, D_MODEL), lambda i, j: (i, 0)),
            pl.BlockSpec((D_MODEL, tf), lambda i, j: (0, j)),
            pl.BlockSpec((D_MODEL, tf), lambda i, j: (0, nf + j)),
            pl.BlockSpec((tf, D_MODEL), lambda i, j: (j, 0)),
            pl.BlockSpec((1, D_MODEL), lambda i, j: (0, 0)),
            pl.BlockSpec((1, D_MODEL), lambda i, j: (0, 0)),
        ],
        out_specs=pl.BlockSpec((tm, D_MODEL), lambda i, j: (i, 0)),
        out_shape=jax.ShapeDtypeStruct((T, D_MODEL), F32),
        scratch_shapes=[pltpu.VMEM((tm, D_MODEL), F32), pltpu.VMEM((tm, D_MODEL), BF16)],
        compiler_params=_cparams(("parallel", "arbitrary")),
        name="ffn_dense",
    )(x, w13, w13, w2, lng, lnb)


def _ffn_expert_kernel(te_ref, used_ref, x_ref, w1_ref, w3_ref, w2_ref, o_ref, acc_ref):
    del te_ref
    live = pl.program_id(0) < used_ref[0]
    last = pl.program_id(1) == pl.num_programs(1) - 1

    @pl.when(live)
    def _():
        _swiglu_step(x_ref, w1_ref, w3_ref, w2_ref, acc_ref)

    @pl.when(live & last)
    def _():
        o_ref[...] = acc_ref[...].astype(o_ref.dtype)

    @pl.when(jnp.logical_not(live) & last)
    def _():
        o_ref[...] = jnp.zeros_like(o_ref)


def _ffn_experts(tile_expert, tiles_used, x_sorted, w13, w2, tm, tf):
    rows = x_sorted.shape[0]
    nf = D_FF // tf
    fblk = lambda i, j, nu: jnp.where(i < nu[0], j, nf - 1)
    grid_spec = pltpu.PrefetchScalarGridSpec(
        num_scalar_prefetch=2,
        grid=(rows // tm, nf),
        in_specs=[
            pl.BlockSpec((tm, D_MODEL), lambda i, j, te, nu: (i, 0)),
            pl.BlockSpec((None, D_MODEL, tf), lambda i, j, te, nu: (te[i], 0, fblk(i, j, nu))),
            pl.BlockSpec((None, D_MODEL, tf), lambda i, j, te, nu: (te[i], 0, nf + fblk(i, j, nu))),
            pl.BlockSpec((None, tf, D_MODEL), lambda i, j, te, nu: (te[i], fblk(i, j, nu), 0)),
        ],
        out_specs=pl.BlockSpec((tm, D_MODEL), lambda i, j, te, nu: (i, 0)),
        scratch_shapes=[pltpu.VMEM((tm, D_MODEL), F32)],
    )
    return pl.pallas_call(
        _ffn_expert_kernel,
        grid_spec=grid_spec,
        out_shape=jax.ShapeDtypeStruct((rows, D_MODEL), BF16),
        compiler_params=_cparams(("parallel", "arbitrary")),
        name="ffn_experts",
    )(tile_expert, tiles_used, x_sorted, w13, w13, w2)


def _route(x, wr):
    logits = jnp.dot(x, wr, preferred_element_type=F32, precision=lax.Precision.HIGHEST)
    lane = _lane(logits.shape)
    logits = jnp.where(lane < N_EXPERTS, logits, -jnp.inf)
    v1 = jnp.max(logits, axis=-1, keepdims=True)
    i1 = jnp.min(jnp.where(logits == v1, lane, LANES), axis=-1, keepdims=True)
    rest = jnp.where(lane == i1, -jnp.inf, logits)
    v2 = jnp.max(rest, axis=-1, keepdims=True)
    i2 = jnp.min(jnp.where(rest == v2, lane, LANES), axis=-1, keepdims=True)
    e2 = jnp.exp(v2 - v1)
    w1 = 1.0 / (1.0 + e2)
    return (jnp.where(lane == 0, i1, jnp.where(lane == 1, i2, 0)),
            jnp.where(lane == 0, w1, jnp.where(lane == 1, e2 * w1, 0.0)))


def _moe_out_kernel(h_ref, y0_ref, y1_ref, wt_ref, lng_ref, lnb_ref, o_ref):
    wt = wt_ref[...]
    f = wt[:, 0:1] * y0_ref[...] + wt[:, 1:2] * y1_ref[...]
    o_ref[...] = _layernorm(DEEPNORM_ALPHA * h_ref[...] + f, lng_ref[...], lnb_ref[...])


def _moe_out(h, y0, y1, wt, lng, lnb, tm):
    T = h.shape[0]
    row = lambda i: (i, 0)
    const = lambda i: (0, 0)
    wide = pl.BlockSpec((tm, D_MODEL), row)
    return pl.pallas_call(
        _moe_out_kernel,
        grid=(T // tm,),
        in_specs=[wide, wide, wide, pl.BlockSpec((tm, LANES), row),
                  pl.BlockSpec((1, D_MODEL), const), pl.BlockSpec((1, D_MODEL), const)],
        out_specs=wide,
        out_shape=jax.ShapeDtypeStruct((T, D_MODEL), F32),
        compiler_params=_cparams(("parallel",)),
        name="moe_out",
    )(h, y0, y1, wt, lng, lnb)


def _moe(h, h_bf16, idx, wt, w13, w2, lng, lnb, tm_router, tm_e, tf):
    T = h.shape[0]
    e_flat = jnp.concatenate([idx[:, s] for s in range(TOP_K)])
    onehot = (e_flat[None, :] == jnp.arange(N_EXPERTS, dtype=jnp.int32)[:, None]).astype(jnp.int32)
    counts = jnp.sum(onehot, axis=1)
    rank = jnp.sum((jnp.cumsum(onehot, axis=1) - onehot) * onehot, axis=0)
    padded = (counts + tm_e - 1) // tm_e * tm_e
    ends = jnp.cumsum(padded)
    pos = jnp.sum(onehot * (ends - padded)[:, None], axis=0) + rank
    rows = TOP_K * T + N_EXPERTS * tm_e
    token = jnp.arange(TOP_K * T, dtype=jnp.int32) % T
    src = (jnp.arange(rows, dtype=jnp.int32) % T).at[pos].set(
        token, unique_indices=True, mode="promise_in_bounds")
    tile_start = jnp.arange(rows // tm_e, dtype=jnp.int32) * tm_e
    tile_expert = jnp.minimum(jnp.sum((tile_start[:, None] >= ends[None, :]).astype(jnp.int32), axis=1),
                              N_EXPERTS - 1)
    x_sorted = h_bf16.at[src].get(mode="promise_in_bounds")
    tiles_used = (ends[N_EXPERTS - 1:] // tm_e).astype(jnp.int32)
    y = _ffn_experts(tile_expert, tiles_used, x_sorted, w13, w2, tm_e, tf)
    y0 = y.at[pos[:T]].get(mode="promise_in_bounds")
    y1 = y.at[pos[T:]].get(mode="promise_in_bounds")
    return _moe_out(h, y0, y1, wt, lng, lnb, tm_router)


def _rot_cols(w, block):
    d, n = w.shape
    w4 = w.reshape(d, n // block, 2, block // 2)
    return jnp.concatenate([-w4[:, :, 1], w4[:, :, 0]], axis=2).reshape(d, n)


def _attention_weights(w_in):
    o = np.cumsum([0, MLA_Q_LORA, MLA_KV_LORA, MLA_ROPE, _DIL_COLS, GQA_Q_HEADS * HEAD_DIM,
                   2 * GQA_KV_HEADS * HEAD_DIM, _DIFF_COLS, N_BRANCH * D_MODEL])
    cq, ckv, kr, dil, gq, gkv, diff, gate = (w_in[:, o[i]:o[i + 1]] for i in range(8))
    z = lambda n: jnp.zeros((D_MODEL, n), w_in.dtype)
    mla = [cq, z(64), ckv, z(64), kr, z(32), z(64), _rot_cols(kr, MLA_ROPE), z(32)]
    gqa = [gq, _rot_cols(gq, 32), gkv[:, :128], _rot_cols(gkv[:, :128], 32), gkv[:, 128:]]
    small = jnp.concatenate(mla + gqa, axis=1)
    return [w.astype(BF16) for w in (small, diff, dil)], gate.astype(BF16)


def _mla_up_weights(w_uq, w_ukv):
    zq = lambda n: jnp.zeros((MLA_Q_LORA, n), w_uq.dtype)
    per = MLA_NOPE + MLA_ROPE
    plain, rot = [], []
    for h in range(MLA_HEADS):
        nope = w_uq[:, h * per:h * per + MLA_NOPE]
        rope = w_uq[:, h * per + MLA_NOPE:(h + 1) * per]
        plain += [nope, rope, zq(32)]
        rot += [zq(64), _rot_cols(rope, MLA_ROPE), zq(32)]
    wuq = jnp.concatenate(plain + rot, axis=1)
    wuq = jnp.pad(wuq, ((0, _Q_LORA_PAD - MLA_Q_LORA), (0, 0)))
    zk = jnp.zeros((MLA_KV_LORA, 64), w_ukv.dtype)
    per = MLA_NOPE + MLA_V
    ks, vs = [], []
    for h in range(MLA_HEADS):
        ks += [w_ukv[:, h * per:h * per + MLA_NOPE], zk]
        vs += [w_ukv[:, h * per + MLA_NOPE:(h + 1) * per], zk]
    wukv = jnp.concatenate(ks + vs, axis=1)
    return wuq.astype(BF16), wukv.astype(BF16)


def _rope_angles(pos, dim):
    inv = ROPE_THETA ** (-(jnp.arange(0, dim, 2, dtype=F32) / dim))
    return pos[:, None] * inv[None, :]


def _position_tables(seq):
    rows = seq // GRID_W
    row_idx = jnp.repeat(jnp.arange(rows, dtype=F32), GRID_W)
    col_idx = jnp.tile(jnp.arange(GRID_W, dtype=F32), rows)
    a1 = _rope_angles(jnp.arange(seq, dtype=F32), MLA_ROPE)
    ar = _rope_angles(row_idx, HEAD_DIM // 2)
    ac = _rope_angles(col_idx, HEAD_DIM // 2)
    z = lambda n: jnp.zeros((seq, n), F32)
    c1 = jnp.concatenate([jnp.ones((seq, MLA_NOPE), F32)] + [jnp.cos(a1)] * 2 + [z(32)], axis=1)
    s1 = jnp.concatenate([z(MLA_NOPE)] + [jnp.sin(a1)] * 2 + [z(32)], axis=1)
    ca = jnp.concatenate(([jnp.cos(ar)] * 2 + [jnp.cos(ac)] * 2) * 2, axis=1)
    sa = jnp.concatenate(([jnp.sin(ar)] * 2 + [jnp.sin(ac)] * 2) * 2, axis=1)
    return [c1, s1, ca, sa]


def _gqa_gain_rows(q_norm, k_norm):
    def swap(g):
        return jnp.concatenate([g[16:32], g[0:16], g[48:64], g[32:48]])

    sg = HEAD_DIM ** -0.5 * LOG2E
    rows = [sg * q_norm, sg * swap(q_norm), k_norm, swap(k_norm)]
    return jnp.stack([jnp.concatenate([r, r]) for r in rows])


def _pick(n, pref):
    t = min(n, pref)
    assert n % t == 0, (n, pref)
    return t


class _Tiles(NamedTuple):
    rows: int
    flash: int
    diff: int
    ffn_rows: int
    ffn_cols: int


def _plan_tiles(seq, tokens):
    return _Tiles(rows=_pick(seq, 512), flash=_pick(seq, 1024), diff=_pick(seq, 512),
                  ffn_rows=_pick(tokens, 1024), ffn_cols=D_FF // 4)


def kernel(x, ln_emb_g, ln_emb_b, w_in, b_gate, mla_q_norm, mla_kv_norm, mla_w_uq, mla_w_ukv,
           gqa_q_norm, gqa_k_norm, diff_lambda, diff_subln, w_branch, w_out, ln1_g, ln1_b,
           ffn_w13, ffn_w2, moe_router, moe_w13, moe_w2, ln2_g, ln2_b):
    batch, seq, _ = x.shape
    T = batch * seq
    tiles = _plan_tiles(seq, T)
    row = lambda v: v.reshape(1, -1)
    pos_tabs = _position_tables(seq)

    h = x.reshape(T, D_MODEL)
    for l in range(DEPTH):
        lam_init = 0.8 - 0.6 * float(np.exp(-0.3 * l))
        weights, wg = _attention_weights(w_in[l])
        wuq, wukv = _mla_up_weights(mla_w_uq[l], mla_w_ukv[l])
        gains = _gqa_gain_rows(gqa_q_norm[l], gqa_k_norm[l])
        nq = jnp.pad(mla_q_norm[l], (0, _Q_LORA_PAD - MLA_Q_LORA)).reshape(1, _Q_LORA_PAD)
        res = _project(h, batch, seq, l == 0, row(ln_emb_g), row(ln_emb_b), weights, wuq, wukv,
                       nq, row(mla_kv_norm[l]), gains, pos_tabs, tiles.rows)
        if l == 0:
            h, res = res[0], res[1:]
        mq, mk, mv, gq, gk, gv, dq, dk, dv = res[:9]
        dil = res[9:]
        ya = _flash(mq, mk, mv, tiles.flash, tiles.flash).reshape(T, BRANCH_W)
        yc = _flash(gq, gk, gv, tiles.flash, tiles.flash).reshape(T, BRANCH_W)
        sub = jnp.pad(diff_subln[l], (0, LANES - DIFF_V)).reshape(1, LANES)
        yd = _diff_attention(dq, dk, dv, diff_lambda[l], sub, lam_init, tiles.diff).reshape(T, BRANCH_W)
        dil_o, dil_lse = [], []
        for g in range(len(DIL_GROUPS)):
            o, lse = _dilated_group(dil[g], dil[3 + g], dil[6 + g], g)
            dil_o.append(o)
            dil_lse.append(lse)
        dense = l % 2 == 0
        mixed = _mix(h, ya, yc, yd, dil_o, dil_lse, wg, b_gate[l], w_branch[l].astype(BF16),
                     w_out[l].astype(BF16), row(ln1_g[l]), row(ln1_b[l]), tiles.rows,
                     None if dense else moe_router[l // 2])
        if dense:
            h = _ffn_dense(mixed[0], ffn_w13[l // 2], ffn_w2[l // 2],
                           row(ln2_g[l]), row(ln2_b[l]), tiles.ffn_rows, tiles.ffn_cols)
        else:
            h = _moe(*mixed, moe_w13[l // 2], moe_w2[l // 2],
                     row(ln2_g[l]), row(ln2_b[l]), tiles.rows, tiles.ffn_rows, tiles.ffn_cols)
    return h.reshape(batch, seq, D_MODEL)
```
